```python
import jax, jax.numpy as jnp
from jax import lax
import numpy as np

D_MODEL = 1024
BATCH = 4
SEQ = 4096
DEPTH = 2

MLA_HEADS = 8
MLA_NOPE = 64
MLA_ROPE = 32
MLA_QK = MLA_NOPE + MLA_ROPE
MLA_V = 64
MLA_Q_RANK = 256
MLA_KV_RANK = 128
ROPE_BASE = 10000.0
ATTN_BLOCK = 128
SG_WIDTH = 512
SG_GROUPS = 8
SG_CHUNK = 128
RW_HEADS = 8
RW_HEAD = 64
RW_WIDTH = RW_HEADS * RW_HEAD
RW_DECAY_RANK = 64
RW_A_RANK = 64
RW_MV_RANK = 32
RW_GATE_RANK = 128
RW_GN_EPS = 64e-5
D_FF = 2816
N_BRANCH = 3
EPS = 1e-6

MLA_COLS = MLA_Q_RANK + MLA_KV_RANK + MLA_ROPE
SG_COLS = 2 * SG_WIDTH
RW_COLS = 3 * RW_WIDTH + RW_DECAY_RANK + RW_A_RANK + RW_GATE_RANK
GATE_COLS = N_BRANCH * D_MODEL
IN_COLS = MLA_COLS + SG_COLS + RW_COLS + GATE_COLS

kernel_name = "hybrid_mla_sgu_rwkv7_block"


def rms_norm(x, g, eps=EPS):
    xf = x.astype(jnp.float32)
    y = xf * lax.rsqrt(jnp.mean(xf * xf, axis=-1, keepdims=True) + eps)
    return (y * g.astype(jnp.float32)).astype(x.dtype)


def swiglu(x, w_gate, w_up, w_down):
    return (jax.nn.silu(x @ w_gate) * (x @ w_up)) @ w_down


def split_cols(p, sizes):
    out, off = [], 0
    for s in sizes:
        out.append(p[..., off:off + s])
        off += s
    return out


def rope_tables(positions):
    inv_freq = ROPE_BASE ** (-jnp.arange(0, MLA_ROPE, 2, dtype=jnp.float32) / MLA_ROPE)
    ang = positions.astype(jnp.float32)[..., None] * inv_freq
    return jnp.cos(ang)[:, :, None, :], jnp.sin(ang)[:, :, None, :]


def apply_rope(x, cos, sin):
    half = x.shape[-1] // 2
    x1, x2 = x[..., :half], x[..., half:]
    c, s = cos.astype(x.dtype), sin.astype(x.dtype)
    return jnp.concatenate([x1 * c - x2 * s, x2 * c + x1 * s], axis=-1)


def causal_block_attention(q, k, v):
    B, S, H, Dq = q.shape
    nb = S // ATTN_BLOCK
    scale = Dq ** -0.5
    qb = q.reshape(B, nb, ATTN_BLOCK, H, Dq).transpose(1, 0, 2, 3, 4)
    kpos = jnp.arange(S)

    def one_block(args):
        q_i, b_i = args
        qpos = b_i * ATTN_BLOCK + jnp.arange(ATTN_BLOCK)
        s = jnp.einsum('bqhd,bkhd->bhqk', q_i, k).astype(jnp.float32) * scale
        s = jnp.where(kpos[None, :] <= qpos[:, None], s, -jnp.inf)
        pr = jax.nn.softmax(s, axis=-1).astype(v.dtype)
        return jnp.einsum('bhqk,bkhd->bqhd', pr, v)

    out = lax.map(one_block, (qb, jnp.arange(nb)))
    return out.transpose(1, 0, 2, 3, 4).reshape(B, S, H, v.shape[-1])


def mla_branch(c_q, c_kv, k_pe, cos, sin, q_a_norm, w_uq, kv_a_norm, w_ukv, q_norm, k_norm):
    B, S, _ = c_q.shape
    q = (rms_norm(c_q, q_a_norm) @ w_uq).reshape(B, S, MLA_HEADS, MLA_QK)
    kv = (rms_norm(c_kv, kv_a_norm) @ w_ukv).reshape(B, S, MLA_HEADS, MLA_NOPE + MLA_V)
    k_nope, v = kv[..., :MLA_NOPE], kv[..., MLA_NOPE:]
    k_rope = jnp.broadcast_to(k_pe[:, :, None, :], (B, S, MLA_HEADS, MLA_ROPE))
    k = jnp.concatenate([k_nope, k_rope], axis=-1)
    q = rms_norm(q, q_norm)
    k = rms_norm(k, k_norm)
    q = jnp.concatenate([q[..., :MLA_NOPE], apply_rope(q[..., MLA_NOPE:], cos, sin)], axis=-1)
    k = jnp.concatenate([k[..., :MLA_NOPE], apply_rope(k[..., MLA_NOPE:], cos, sin)], axis=-1)
    o = causal_block_attention(q, k, v)
    return o.reshape(B, S, MLA_HEADS * MLA_V)


def sgu_branch(p, v_norm, w_s, b_s):
    B, S, _ = p.shape
    p = jax.nn.gelu(p)
    u, v = p[..., :SG_WIDTH], p[..., SG_WIDTH:]
    v = rms_norm(v, v_norm)
    v = v.reshape(B, S // SG_CHUNK, SG_CHUNK, SG_GROUPS, SG_WIDTH // SG_GROUPS)
    mask = jnp.tril(jnp.ones((SG_CHUNK, SG_CHUNK), dtype=bool))
    w = jnp.where(mask[None], w_s, jnp.zeros_like(w_s))
    mixed = jnp.einsum('gts,bnsgc->bntgc', w, v) + b_s.T[None, None, :, :, None]
    return u * mixed.reshape(B, S, SG_WIDTH)


def token_shift(p, mu):
    prev = jnp.pad(p[:, :-1], ((0, 0), (1, 0), (0, 0)))
    return p + (prev - p) * mu


def rwkv7_scan(r, decay, k, v, a, b):
    B, S, H, N = r.shape
    xs = tuple(t.astype(jnp.float32).transpose(1, 0, 2, 3) for t in (r, decay, k, v, a, b))

    def step(state, inp):
        r_t, w_t, k_t, v_t, a_t, b_t = inp
        sa = jnp.einsum('bhvk,bhk->bhv', state, a_t)
        state = (state * w_t[:, :, None, :] + sa[..., None] * b_t[:, :, None, :]
                 + v_t[..., None] * k_t[:, :, None, :])
        return state, jnp.einsum('bhvk,bhk->bhv', state, r_t)

    s0 = jnp.zeros((B, H, N, N), jnp.float32)
    _, ys = lax.scan(step, s0, xs)
    return ys.transpose(1, 0, 2, 3)


def rwkv7_branch(p_rw, mu, w0, w2, a0, a2, g2, k_k, k_a, r_k, ln_g, ln_b, v_first, vres):
    B, S, _ = p_rw.shape
    dt = p_rw.dtype
    p_rw = token_shift(p_rw, mu)
    r, k, v, xw, xa, xg = split_cols(p_rw, [RW_WIDTH, RW_WIDTH, RW_WIDTH,
                                            RW_DECAY_RANK, RW_A_RANK, RW_GATE_RANK])
    w = -jax.nn.softplus(-(w0 + jnp.tanh(xw) @ w2).astype(jnp.float32)) - 0.5
    decay = jnp.exp(-jnp.exp(w))
    a = jax.nn.sigmoid(a0 + xa @ a2)
    g = jax.nn.sigmoid(xg) @ g2
    if vres is None:
        v_first = v
    else:
        v0, v1, v2 = vres
        v = v + (v_first - v) * jax.nn.sigmoid(v0 + (v @ v1) @ v2)
    hs = (B, S, RW_HEADS, RW_HEAD)
    kk = (k * k_k).reshape(hs).astype(jnp.float32)
    kk = kk * lax.rsqrt(jnp.maximum(jnp.sum(kk * kk, axis=-1, keepdims=True), 1e-24))
    k = k * (1.0 + (a - 1.0) * k_a)
    a_h = a.reshape(hs).astype(jnp.float32)
    r_h, k_h, v_h = r.reshape(hs), k.reshape(hs), v.reshape(hs)
    y = rwkv7_scan(r_h, decay.reshape(hs), k_h, v_h, -kk, kk * a_h)
    mean = jnp.mean(y, axis=-1, keepdims=True)
    var = jnp.mean(jnp.square(y - mean), axis=-1, keepdims=True)
    y = ((y - mean) * lax.rsqrt(var + RW_GN_EPS)).reshape(B, S, RW_WIDTH)
    y = (y * ln_g.astype(jnp.float32) + ln_b.astype(jnp.float32)).astype(dt)
    bonus = jnp.sum(r_h * k_h * r_k, axis=-1, keepdims=True) * v_h
    y = (y + bonus.reshape(B, S, RW_WIDTH)) * g
    return y, v_first


def setup_inputs(seed: int = 0) -> dict:
    key = jax.random.key(seed)
    ks = iter(jax.random.split(key, 64))
    f32 = jnp.float32

    def nrm(shape, scale):
        return jax.random.normal(next(ks), shape, f32) * scale

    def gain(shape):
        return 1.0 + 0.02 * jax.random.normal(next(ks), shape, f32)

    L = DEPTH
    x = jax.random.normal(next(ks), (BATCH, SEQ, D_MODEL), f32)
    offs = jax.random.randint(next(ks), (BATCH, 1), 0, 1024, dtype=jnp.int32)
    positions = offs + jnp.arange(SEQ, dtype=jnp.int32)[None, :]
    return {
        "x": x,
        "positions": positions,
        "ffn1_norm": gain((L, D_MODEL)),
        "ffn1_w_gate": nrm((L, D_MODEL, D_FF), D_MODEL ** -0.5),
        "ffn1_w_up": nrm((L, D_MODEL, D_FF), D_MODEL ** -0.5),
        "ffn1_w_down": nrm((L, D_FF, D_MODEL), D_FF ** -0.5),
        "mix_norm": gain((L, D_MODEL)),
        "w_in": nrm((L, D_MODEL, IN_COLS), D_MODEL ** -0.5),
        "mla_q_a_norm": gain((L, MLA_Q_RANK)),
        "mla_w_uq": nrm((L, MLA_Q_RANK, MLA_HEADS * MLA_QK), MLA_Q_RANK ** -0.5),
        "mla_kv_a_norm": gain((L, MLA_KV_RANK)),
        "mla_w_ukv": nrm((L, MLA_KV_RANK, MLA_HEADS * (MLA_NOPE + MLA_V)), MLA_KV_RANK ** -0.5),
        "mla_q_norm": gain((L, MLA_QK)),
        "mla_k_norm": gain((L, MLA_QK)),
        "sg_v_norm": gain((L, SG_WIDTH)),
        "sg_w_s": nrm((L, SG_GROUPS, SG_CHUNK, SG_CHUNK), SG_CHUNK ** -0.5),
        "sg_b_s": gain((L, SG_GROUPS, SG_CHUNK)),
        "rw_mu": jax.random.uniform(next(ks), (L, RW_COLS), f32),
        "rw_w0": jax.random.uniform(next(ks), (L, RW_WIDTH), f32, -6.5, -1.5),
        "rw_w2": nrm((L, RW_DECAY_RANK, RW_WIDTH), 0.1),
        "rw_a0": nrm((L, RW_WIDTH), 0.1),
        "rw_a2": nrm((L, RW_A_RANK, RW_WIDTH), 0.5 * RW_A_RANK ** -0.5),
        "rw_g2": nrm((L, RW_GATE_RANK, RW_WIDTH), RW_GATE_RANK ** -0.5),
        "rw_k_k": 0.85 + 0.02 * jax.random.normal(next(ks), (L, RW_WIDTH), f32),
        "rw_k_a": gain((L, RW_WIDTH)),
        "rw_r_k": nrm((L, RW_HEADS, RW_HEAD), 0.1),
        "rw_ln_g": gain((L, RW_WIDTH)),
        "rw_ln_b": nrm((L, RW_WIDTH), 0.02),
        "rw_v0": gain((L - 1, RW_WIDTH)),
        "rw_v1": nrm((L - 1, RW_WIDTH, RW_MV_RANK), RW_WIDTH ** -0.5),
        "rw_v2": nrm((L - 1, RW_MV_RANK, RW_WIDTH), RW_MV_RANK ** -0.5),
        "w_out_mla": nrm((L, MLA_HEADS * MLA_V, D_MODEL), (MLA_HEADS * MLA_V) ** -0.5),
        "w_out_sg": nrm((L, SG_WIDTH, D_MODEL), SG_WIDTH ** -0.5),
        "w_out_rw": nrm((L, RW_WIDTH, D_MODEL), RW_WIDTH ** -0.5),
        "w_o": nrm((L, D_MODEL, D_MODEL), D_MODEL ** -0.5),
        "ffn2_norm": gain((L, D_MODEL)),
        "ffn2_w_gate": nrm((L, D_MODEL, D_FF), D_MODEL ** -0.5),
        "ffn2_w_up": nrm((L, D_MODEL, D_FF), D_MODEL ** -0.5),
        "ffn2_w_down": nrm((L, D_FF, D_MODEL), D_FF ** -0.5),
    }


def reference(x, positions, ffn1_norm, ffn1_w_gate, ffn1_w_up, ffn1_w_down, mix_norm, w_in,
              mla_q_a_norm, mla_w_uq, mla_kv_a_norm, mla_w_ukv, mla_q_norm, mla_k_norm,
              sg_v_norm, sg_w_s, sg_b_s,
              rw_mu, rw_w0, rw_w2, rw_a0, rw_a2, rw_g2, rw_k_k, rw_k_a, rw_r_k, rw_ln_g, rw_ln_b,
              rw_v0, rw_v1, rw_v2,
              w_out_mla, w_out_sg, w_out_rw, w_o,
              ffn2_norm, ffn2_w_gate, ffn2_w_up, ffn2_w_down):
    B, S, D = x.shape
    cos, sin = rope_tables(positions)
    h = x
    v_first = None
    for i in range(DEPTH):
        h = h + 0.5 * swiglu(rms_norm(h, ffn1_norm[i]), ffn1_w_gate[i], ffn1_w_up[i], ffn1_w_down[i])
        z = rms_norm(h, mix_norm[i])
        p = z @ w_in[i]
        c_q, c_kv, k_pe, p_sg, p_rw, p_gate = split_cols(
            p, [MLA_Q_RANK, MLA_KV_RANK, MLA_ROPE, SG_COLS, RW_COLS, GATE_COLS])
        y_a = mla_branch(c_q, c_kv, k_pe, cos, sin, mla_q_a_norm[i], mla_w_uq[i],
                         mla_kv_a_norm[i], mla_w_ukv[i], mla_q_norm[i], mla_k_norm[i])
        y_b = sgu_branch(p_sg, sg_v_norm[i], sg_w_s[i], sg_b_s[i])
        vres = None if i == 0 else (rw_v0[i - 1], rw_v1[i - 1], rw_v2[i - 1])
        y_c, v_first = rwkv7_branch(p_rw, rw_mu[i], rw_w0[i], rw_w2[i], rw_a0[i], rw_a2[i], rw_g2[i],
                                    rw_k_k[i], rw_k_a[i], rw_r_k[i], rw_ln_g[i], rw_ln_b[i],
                                    v_first, vres)
        gates = jax.nn.sigmoid(p_gate).reshape(B, S, N_BRANCH, D)
        merged = (gates[:, :, 0] * (y_a @ w_out_mla[i])
                  + gates[:, :, 1] * (y_b @ w_out_sg[i])
                  + gates[:, :, 2] * (y_c @ w_out_rw[i]))
        h = h + merged @ w_o[i]
        h = h + 0.5 * swiglu(rms_norm(h, ffn2_norm[i]), ffn2_w_gate[i], ffn2_w_up[i], ffn2_w_down[i])
    return h
```

```python
import functools

import jax
import jax.numpy as jnp
from jax import lax
from jax.experimental import pallas as pl
from jax.experimental.pallas import tpu as pltpu

F32 = jnp.float32
BF16 = jnp.bfloat16
HIGHEST = lax.Precision.HIGHEST

LANES = 128
EPS = 1e-6
RW_GN_EPS = 64e-5
ROPE_BASE = 10000.0

MLA_HEADS = 8
MLA_NOPE = 64
MLA_ROPE = 32
MLA_QK = MLA_NOPE + MLA_ROPE
MLA_V = 64
MLA_Q_RANK = 256
MLA_KV_RANK = 128
SG_WIDTH = 512
SG_GROUPS = 8
SG_CHUNK = 128
RW_HEADS = 8
RW_HEAD = 64
RW_WIDTH = RW_HEADS * RW_HEAD
RW_DECAY_RANK = 64
RW_A_RANK = 64
RW_GATE_RANK = 128
RW_COLS = 3 * RW_WIDTH + RW_DECAY_RANK + RW_A_RANK + RW_GATE_RANK

VMEM_LIMIT = 56 * 1024 * 1024
SCAN_CHUNK = 64
SCAN_STEP = 256


def _params(*sem):
    return pltpu.CompilerParams(dimension_semantics=sem, vmem_limit_bytes=VMEM_LIMIT)


def _dot(a, b):
    return jnp.dot(a.astype(BF16), b.astype(BF16), preferred_element_type=F32)


def _dot_f32(a, b):
    return jnp.dot(a, b, preferred_element_type=F32, precision=HIGHEST)


def _dot_nt(a, b, precision=None):
    return lax.dot_general(a, b, (((1,), (1,)), ((), ())), preferred_element_type=F32,
                           precision=precision)


def _dot_tn(a, b, precision=None):
    return lax.dot_general(a, b, (((0,), (0,)), ((), ())), preferred_element_type=F32,
                           precision=precision)


def _rms(x, g, n=None):
    n = x.shape[-1] if n is None else n
    ms = jnp.sum(x * x, axis=-1, keepdims=True) * (1.0 / n)
    return x * lax.rsqrt(ms + EPS) * g


def _segsum(x, bd):
    hi = x.astype(BF16)
    lo = (x - hi.astype(F32)).astype(BF16)
    return (jnp.dot(hi, bd, preferred_element_type=F32) + jnp.dot(lo, bd, preferred_element_type=F32))


def _rms_matmul_kernel(h_ref, g_ref, w_ref, o_ref, z_ref):
    @pl.when(pl.program_id(1) == 0)
    def _():
        z_ref[...] = _rms(h_ref[...], g_ref[...]).astype(BF16)

    o_ref[...] = jnp.dot(z_ref[...], w_ref[...], preferred_element_type=F32).astype(o_ref.dtype)


def rms_matmul(h, g, w, *, tm, tn, out_dtype=F32):
    t, d = h.shape
    n = w.shape[1]
    return pl.pallas_call(
        _rms_matmul_kernel,
        grid=(t // tm, n // tn),
        in_specs=[pl.BlockSpec((tm, d), lambda i, j: (i, 0)),
                  pl.BlockSpec((1, d), lambda i, j: (0, 0)),
                  pl.BlockSpec((d, tn), lambda i, j: (0, j))],
        out_specs=pl.BlockSpec((tm, tn), lambda i, j: (i, j)),
        out_shape=jax.ShapeDtypeStruct((t, n), out_dtype),
        scratch_shapes=[pltpu.VMEM((tm, d), BF16)],
        compiler_params=_params("parallel", "arbitrary"),
        name="rms_matmul",
    )(h, g.reshape(1, d), w)


def _ffn_kernel(h_ref, g_ref, wg_ref, wu_ref, wd_ref, o_ref, z_ref, acc_ref):
    j = pl.program_id(1)

    @pl.when(j == 0)
    def _():
        z_ref[...] = _rms(h_ref[...], g_ref[...]).astype(BF16)
        acc_ref[...] = jnp.zeros_like(acc_ref)

    z = z_ref[...]
    a = jnp.dot(z, wg_ref[...], preferred_element_type=F32)
    b = jnp.dot(z, wu_ref[...], preferred_element_type=F32)
    t = (a * jax.nn.sigmoid(a) * b).astype(BF16)
    acc_ref[...] += jnp.dot(t, wd_ref[...], preferred_element_type=F32)

    @pl.when(j == pl.num_programs(1) - 1)
    def _():
        o_ref[...] = h_ref[...] + 0.5 * acc_ref[...]


def ffn_half_step(h, g, wg, wu, wd, *, tm=1024, tf=256):
    t, d = h.shape
    f = wg.shape[1]
    return pl.pallas_call(
        _ffn_kernel,
        grid=(t // tm, f // tf),
        in_specs=[pl.BlockSpec((tm, d), lambda i, j: (i, 0)),
                  pl.BlockSpec((1, d), lambda i, j: (0, 0)),
                  pl.BlockSpec((d, tf), lambda i, j: (0, j)),
                  pl.BlockSpec((d, tf), lambda i, j: (0, j)),
                  pl.BlockSpec((tf, d), lambda i, j: (j, 0))],
        out_specs=pl.BlockSpec((tm, d), lambda i, j: (i, 0)),
        out_shape=jax.ShapeDtypeStruct((t, d), F32),
        scratch_shapes=[pltpu.VMEM((tm, d), BF16), pltpu.VMEM((tm, d), F32)],
        compiler_params=_params("parallel", "arbitrary"),
        name="ffn_half_step",
    )(h, g.reshape(1, d), wg, wu, wd)


def _rope(x, cos_full, sin_signed, lane):
    rot = jnp.where(lane < MLA_NOPE + MLA_ROPE // 2,
                    pltpu.roll(x, LANES - MLA_ROPE // 2, axis=1),
                    pltpu.roll(x, MLA_ROPE // 2, axis=1))
    return x * cos_full + rot * sin_signed


def _mla_prep_kernel(c_ref, cos_ref, sin_ref, qa_ref, kva_ref, wuq_ref, wuk_ref, wuv_ref, qn_ref, kn_ref,
                     q_ref, k_ref, v_ref):
    c = c_ref[...]
    zq = _rms(c[:, :MLA_Q_RANK], qa_ref[...]).astype(BF16)
    zkv = _rms(c[:, MLA_Q_RANK:MLA_Q_RANK + MLA_KV_RANK], kva_ref[...]).astype(BF16)
    kpe = c[:, MLA_Q_RANK + MLA_KV_RANK:]
    q = jnp.dot(zq, wuq_ref[...], preferred_element_type=F32)
    kn = jnp.dot(zkv, wuk_ref[...], preferred_element_type=F32)
    v_ref[...] = jnp.dot(zkv, wuv_ref[...], preferred_element_type=F32).astype(BF16)
    cos_full = cos_ref[...]
    sin_signed = sin_ref[...]
    lane = lax.broadcasted_iota(jnp.int32, (c.shape[0], LANES), 1)
    scale = MLA_QK ** -0.5
    for h in range(MLA_HEADS):
        sl = slice(h * LANES, (h + 1) * LANES)
        qh = _rope(_rms(q[:, sl], qn_ref[...], MLA_QK), cos_full, sin_signed, lane)
        q_ref[:, sl] = (qh * scale).astype(BF16)
        kh = _rope(_rms(kn[:, sl] + kpe, kn_ref[...], MLA_QK), cos_full, sin_signed, lane)
        k_ref[:, sl] = kh.astype(BF16)


def mla_prep(c, cos_full, sin_signed, qa, kva, wuq, wuk, wuv, qn, kn, *, tm=512):
    t = c.shape[0]
    hw = MLA_HEADS * LANES
    row = lambda i: (i, 0)
    fix = lambda i: (0, 0)
    return pl.pallas_call(
        _mla_prep_kernel,
        grid=(t // tm,),
        in_specs=[pl.BlockSpec((tm, c.shape[1]), row),
                  pl.BlockSpec((tm, LANES), row),
                  pl.BlockSpec((tm, LANES), row),
                  pl.BlockSpec((1, MLA_Q_RANK), fix),
                  pl.BlockSpec((1, MLA_KV_RANK), fix),
                  pl.BlockSpec(wuq.shape, fix),
                  pl.BlockSpec(wuk.shape, fix),
                  pl.BlockSpec(wuv.shape, fix),
                  pl.BlockSpec((1, LANES), fix),
                  pl.BlockSpec((1, LANES), fix)],
        out_specs=[pl.BlockSpec((tm, hw), row),
                   pl.BlockSpec((tm, hw), row),
                   pl.BlockSpec((tm, MLA_HEADS * MLA_V), row)],
        out_shape=[jax.ShapeDtypeStruct((t, hw), BF16),
                   jax.ShapeDtypeStruct((t, hw), BF16),
                   jax.ShapeDtypeStruct((t, MLA_HEADS * MLA_V), BF16)],
        compiler_params=_params("parallel"),
        name="mla_prep",
    )(c, cos_full, sin_signed, qa, kva, wuq, wuk, wuv, qn, kn)


def _attn_kernel(q_ref, k_ref, v_ref, o_ref, *, tq):
    i = pl.program_id(2)
    q = q_ref[...]
    row = lax.broadcasted_iota(jnp.int32, (tq, tq), 0)
    col = lax.broadcasted_iota(jnp.int32, (tq, tq), 1)

    def step(j, carry, masked):
        start = pl.multiple_of(j * tq, tq)
        ks = k_ref[pl.ds(start, tq), :]
        vs = v_ref[pl.ds(start, tq), :]
        out = []
        for hh in range(2):
            m, l, acc = carry[hh]
            s = _dot_nt(q[:, hh * LANES:(hh + 1) * LANES], ks[:, hh * LANES:(hh + 1) * LANES])
            if masked:
                s = jnp.where(col <= row, s, -jnp.inf)
            m_new = jnp.maximum(m, jnp.max(s, axis=-1, keepdims=True))
            alpha = jnp.exp(m - m_new)
            p = jnp.exp(s - m_new)
            l = alpha * l + jnp.sum(p, axis=-1, keepdims=True)
            acc = alpha * acc + jnp.dot(p.astype(BF16), vs, preferred_element_type=F32)
            out.append((m_new, l, acc))
        return tuple(out)

    init = tuple((jnp.full((tq, 1), -jnp.inf, F32), jnp.zeros((tq, 1), F32), jnp.zeros((tq, LANES), F32))
                 for _ in range(2))
    carry = lax.fori_loop(0, i, lambda j, c: step(j, c, False), init)
    (_, l0, a0), (_, l1, a1) = step(i, carry, True)
    lane = lax.broadcasted_iota(jnp.int32, (tq, LANES), 1)
    o_ref[...] = jnp.where(lane < MLA_V, a0 / l0, a1 / l1).astype(o_ref.dtype)


def mla_attention(q, k, v, *, batch, seq, tq=256):
    t = q.shape[0]
    nq = seq // tq
    return pl.pallas_call(
        functools.partial(_attn_kernel, tq=tq),
        grid=(batch, MLA_HEADS // 2, nq),
        in_specs=[pl.BlockSpec((tq, 2 * LANES), lambda b, p, i: (b * nq + i, p)),
                  pl.BlockSpec((seq, 2 * LANES), lambda b, p, i: (b, p)),
                  pl.BlockSpec((seq, LANES), lambda b, p, i: (b, p))],
        out_specs=pl.BlockSpec((tq, LANES), lambda b, p, i: (b * nq + i, p)),
        out_shape=jax.ShapeDtypeStruct((t, MLA_HEADS * MLA_V), BF16),
        compiler_params=_params("parallel", "parallel", "arbitrary"),
        name="mla_attention",
    )(q, k, v)


def _sgu_kernel(p_ref, vn_ref, w_ref, b_ref, o_ref):
    p = p_ref[...]
    gl = jax.nn.gelu(p, approximate=True)
    u = gl[:, :SG_WIDTH]
    v = _rms(gl[:, SG_WIDTH:], vn_ref[...])
    gw = SG_WIDTH // SG_GROUPS
    lane_group = lax.broadcasted_iota(jnp.int32, (SG_CHUNK, SG_WIDTH), 1) // gw
    wrow = lax.broadcasted_iota(jnp.int32, (SG_CHUNK, SG_GROUPS * SG_CHUNK), 0)
    wcol = lax.broadcasted_iota(jnp.int32, (SG_CHUNK, SG_GROUPS * SG_CHUNK), 1) % SG_CHUNK
    w = jnp.where(wcol <= wrow, w_ref[...], 0.0).astype(BF16)
    for c in range(p.shape[0] // SG_CHUNK):
        rows = slice(c * SG_CHUNK, (c + 1) * SG_CHUNK)
        vc = v[rows]
        stacked = jnp.concatenate(
            [jnp.where(lane_group == g, vc, 0.0).astype(BF16) for g in range(SG_GROUPS)], axis=0)
        mixed = jnp.dot(w, stacked, preferred_element_type=F32) + b_ref[...]
        o_ref[rows, :] = (u[rows] * mixed).astype(o_ref.dtype)


def sgu(p_sg, v_norm, w_cat, b_full, *, tm=512):
    t = p_sg.shape[0]
    fix = lambda i: (0, 0)
    return pl.pallas_call(
        _sgu_kernel,
        grid=(t // tm,),
        in_specs=[pl.BlockSpec((tm, 2 * SG_WIDTH), lambda i: (i, 0)),
                  pl.BlockSpec((1, SG_WIDTH), fix),
                  pl.BlockSpec(w_cat.shape, fix),
                  pl.BlockSpec(b_full.shape, fix)],
        out_specs=pl.BlockSpec((tm, SG_WIDTH), lambda i: (i, 0)),
        out_shape=jax.ShapeDtypeStruct((t, SG_WIDTH), BF16),
        compiler_params=_params("parallel"),
        name="sgu",
    )(p_sg, v_norm, w_cat, b_full)


def _rw_prep_kernel(*refs, tiles_per_seq, has_vres):
    if has_vres:
        (p_ref, prev_ref, mu_ref, w0_ref, w2_ref, a0_ref, a2_ref, g2_ref, kk_ref, ka_ref, rk_ref, bd_ref,
         vf_ref, v0_ref, v1_ref, v2_ref,
         r_o, lw_o, k_o, v_o, a_o, b_o, g_o, bonus_o) = refs
    else:
        (p_ref, prev_ref, mu_ref, w0_ref, w2_ref, a0_ref, a2_ref, g2_ref, kk_ref, ka_ref, rk_ref, bd_ref,
         r_o, lw_o, k_o, v_o, a_o, b_o, g_o, bonus_o) = refs
    cur = p_ref[...]
    tm = cur.shape[0]
    first = (pl.program_id(0) % tiles_per_seq) == 0
    last_prev = jnp.where(first, 0.0, prev_ref[7:8, :])
    rowi = lax.broadcasted_iota(jnp.int32, cur.shape, 0)
    prev = jnp.where(rowi == 0, last_prev, pltpu.roll(cur, 1, axis=0))
    x = cur + (prev - cur) * mu_ref[...]
    w3 = 3 * RW_WIDTH
    r = x[:, :RW_WIDTH]
    k = x[:, RW_WIDTH:2 * RW_WIDTH]
    v = x[:, 2 * RW_WIDTH:w3]
    xwa = x[:, w3:w3 + LANES]
    xg = x[:, w3 + LANES:]
    w = -jax.nn.softplus(-(w0_ref[...] + _dot_f32(jnp.tanh(xwa), w2_ref[...]))) - 0.5
    lw_o[...] = -jnp.exp(w)
    a = jax.nn.sigmoid(a0_ref[...] + _dot_f32(xwa, a2_ref[...]))
    g_o[...] = _dot(jax.nn.sigmoid(xg), g2_ref[...])
    if has_vres:
        mix = jax.nn.sigmoid(v0_ref[...] + _dot_f32(_dot_f32(v, v1_ref[...]), v2_ref[...]))
        v = v + (vf_ref[...] - v) * mix
    bd = bd_ref[...]
    kk = k * kk_ref[...]
    kk = kk * lax.rsqrt(jnp.maximum(_segsum(kk * kk, bd), 1e-24))
    k = k * (1.0 + (a - 1.0) * ka_ref[...])
    r_o[...] = r
    k_o[...] = k
    v_o[...] = v
    a_o[...] = -kk
    b_o[...] = kk * a
    bonus_o[...] = _segsum(r * k * rk_ref[...], bd) * v


def rw_prep(p_rw, lp, v_first, *, seq, tm=256):
    t = p_rw.shape[0]
    has_vres = v_first is not None
    row = lambda i: (i, 0)
    fix = lambda i: (0, 0)
    vec = pl.BlockSpec((1, RW_WIDTH), fix)
    in_specs = [pl.BlockSpec((tm, RW_COLS), row),
                pl.BlockSpec((8, RW_COLS), lambda i: (jnp.maximum(i * (tm // 8) - 1, 0), 0)),
                pl.BlockSpec((1, RW_COLS), fix),
                vec, pl.BlockSpec((LANES, RW_WIDTH), fix),
                vec, pl.BlockSpec((LANES, RW_WIDTH), fix),
                pl.BlockSpec((RW_GATE_RANK, RW_WIDTH), fix),
                vec, vec, vec, pl.BlockSpec((RW_WIDTH, RW_WIDTH), fix)]
    args = [p_rw, p_rw, lp["rw_mu"], lp["rw_w0"], lp["rw_w2"], lp["rw_a0"], lp["rw_a2"], lp["rw_g2"],
            lp["rw_k_k"], lp["rw_k_a"], lp["rw_r_k"], lp["rw_bd"]]
    if has_vres:
        in_specs += [pl.BlockSpec((tm, RW_WIDTH), row), vec,
                     pl.BlockSpec(lp["rw_v1"].shape, fix), pl.BlockSpec(lp["rw_v2"].shape, fix)]
        args += [v_first, lp["rw_v0"], lp["rw_v1"], lp["rw_v2"]]
    out = jax.ShapeDtypeStruct((t, RW_WIDTH), F32)
    return pl.pallas_call(
        functools.partial(_rw_prep_kernel, tiles_per_seq=seq // tm, has_vres=has_vres),
        grid=(t // tm,),
        in_specs=in_specs,
        out_specs=[pl.BlockSpec((tm, RW_WIDTH), row)] * 8,
        out_shape=[out] * 8,
        compiler_params=_params("parallel"),
        name="rw_prep",
    )(*args)


def _tri_inverse(l_mat, eye):
    inv = eye + l_mat
    lp = l_mat
    n = l_mat.shape[0]
    p = 2
    while p < n:
        lp = _dot_f32(lp, lp)
        inv = inv + _dot_f32(inv, lp)
        p *= 2
    return inv


def _rw_scan_kernel(r_ref, lw_ref, k_ref, v_ref, a_ref, b_ref, y_ref, s_ref):
    c = SCAN_CHUNK

    @pl.when(pl.program_id(2) == 0)
    def _():
        s_ref[...] = jnp.zeros_like(s_ref)

    lane = lax.broadcasted_iota(jnp.int32, (c, LANES), 1)
    head0 = lane < RW_HEAD
    tr = lax.broadcasted_iota(jnp.int32, (c, c), 0)
    tc = lax.broadcasted_iota(jnp.int32, (c, c), 1)
    tri_incl = (tc <= tr).astype(F32)
    eye = (tc == tr).astype(F32)
    pr = lax.broadcasted_iota(jnp.int32, (c, 2 * c), 0)
    pc = lax.broadcasted_iota(jnp.int32, (c, 2 * c), 1)
    strict = (pc % c) < pr
    incl = (pc % c) <= pr
    sr = lax.broadcasted_iota(jnp.int32, (LANES, LANES), 0)
    sc = lax.broadcasted_iota(jnp.int32, (LANES, LANES), 1)
    same_head = (sr // RW_HEAD) == (sc // RW_HEAD)

    for ci in range(r_ref.shape[0] // c):
        rows = slice(ci * c, (ci + 1) * c)
        lw = lw_ref[rows, :]
        cum = _dot_f32(tri_incl, lw)
        g_in = jnp.exp(cum)
        g_inv = jnp.exp(-cum)
        g_to_end = jnp.exp(cum[c - 1:c, :] - cum)
        rt = r_ref[rows, :] * g_in
        at = a_ref[rows, :] * jnp.exp(cum - lw)
        kraw = k_ref[rows, :]
        braw = b_ref[rows, :]
        kt = kraw * g_inv
        bt = braw * g_inv
        v = v_ref[rows, :]
        lhs = jnp.concatenate([jnp.where(head0, at, 0.0), jnp.where(head0, rt, 0.0),
                               jnp.where(head0, 0.0, at), jnp.where(head0, 0.0, rt)], axis=0)
        rhs = jnp.concatenate([bt, kt], axis=0)
        pm = _dot_nt(lhs, rhs, HIGHEST)
        s0 = s_ref[...]
        x = _dot_nt(at, s0, HIGHEST)
        y = _dot_nt(rt, s0, HIGHEST)
        vv = jnp.concatenate([v, v], axis=0)
        u_heads = []
        for hh in range(2):
            pa = jnp.where(strict, pm[(2 * hh) * c:(2 * hh + 1) * c], 0.0)
            inv = _tri_inverse(pa[:, :c], eye)
            xh = x + _dot_f32(jnp.where(pc >= c, pa, 0.0), vv)
            u_heads.append(_dot_f32(inv, xh))
        u = jnp.where(head0, u_heads[0], u_heads[1])
        uv = jnp.concatenate([u, v], axis=0)
        y_heads = []
        for hh in range(2):
            prk = jnp.where(incl, pm[(2 * hh + 1) * c:(2 * hh + 2) * c], 0.0)
            y_heads.append(_dot_f32(prk, uv))
        y_ref[rows, :] = y + jnp.where(head0, y_heads[0], y_heads[1])
        bk = jnp.concatenate([braw * g_to_end, kraw * g_to_end], axis=0)
        upd = _dot_tn(uv, bk, HIGHEST)
        s_ref[...] = s0 * g_in[c - 1:c, :] + jnp.where(same_head, upd, 0.0)


def rw_scan(r, lw, k, v, a, b, *, batch, seq):
    t = r.shape[0]
    ns = seq // SCAN_STEP
    spec = pl.BlockSpec((SCAN_STEP, LANES), lambda bi, p, i: (bi * ns + i, p))
    return pl.pallas_call(
        _rw_scan_kernel,
        grid=(batch, RW_WIDTH // LANES, ns),
        in_specs=[spec] * 6,
        out_specs=spec,
        out_shape=jax.ShapeDtypeStruct((t, RW_WIDTH), F32),
        scratch_shapes=[pltpu.VMEM((LANES, LANES), F32)],
        compiler_params=_params("parallel", "parallel", "arbitrary"),
        name="rw_scan",
    )(r, lw, k, v, a, b)


def _rw_post_kernel(y_ref, bonus_ref, g_ref, lng_ref, lnb_ref, bd_ref, o_ref):
    y = y_ref[...]
    bd = bd_ref[...]
    mean = _segsum(y, bd) * (1.0 / RW_HEAD)
    d = y - mean
    var = _segsum(d * d, bd) * (1.0 / RW_HEAD)
    yn = d * lax.rsqrt(var + RW_GN_EPS) * lng_ref[...] + lnb_ref[...]
    o_ref[...] = ((yn + bonus_ref[...]) * g_ref[...]).astype(o_ref.dtype)


def rw_post(y, bonus, g, ln_g, ln_b, bd, *, tm=512):
    t = y.shape[0]
    row = pl.BlockSpec((tm, RW_WIDTH), lambda i: (i, 0))
    vec = pl.BlockSpec((1, RW_WIDTH), lambda i: (0, 0))
    return pl.pallas_call(
        _rw_post_kernel,
        grid=(t // tm,),
        in_specs=[row, row, row, vec, vec, pl.BlockSpec((RW_WIDTH, RW_WIDTH), lambda i: (0, 0))],
        out_specs=row,
        out_shape=jax.ShapeDtypeStruct((t, RW_WIDTH), BF16),
        compiler_params=_params("parallel"),
        name="rw_post",
    )(y, bonus, g, ln_g, ln_b, bd)


def _merge_kernel(h_ref, ya_ref, yb_ref, yc_ref, pg_ref, wa_ref, wb_ref, wc_ref, wo_ref, o_ref):
    d = h_ref.shape[1]
    merged = (jax.nn.sigmoid(pg_ref[:, :d]) * jnp.dot(ya_ref[...], wa_ref[...], preferred_element_type=F32)
              + jax.nn.sigmoid(pg_ref[:, d:2 * d]) * jnp.dot(yb_ref[...], wb_ref[...], preferred_element_type=F32)
              + jax.nn.sigmoid(pg_ref[:, 2 * d:]) * jnp.dot(yc_ref[...], wc_ref[...], preferred_element_type=F32))
    o_ref[...] = h_ref[...] + jnp.dot(merged.astype(BF16), wo_ref[...], preferred_element_type=F32)


def merge(h, ya, yb, yc, p_gate, wa, wb, wc, wo, *, tm=512):
    t, d = h.shape
    row = lambda i: (i, 0)
    fix = lambda i: (0, 0)
    return pl.pallas_call(
        _merge_kernel,
        grid=(t // tm,),
        in_specs=[pl.BlockSpec((tm, d), row),
                  pl.BlockSpec((tm, ya.shape[1]), row),
                  pl.BlockSpec((tm, yb.shape[1]), row),
                  pl.BlockSpec((tm, yc.shape[1]), row),
                  pl.BlockSpec((tm, 3 * d), row),
                  pl.BlockSpec(wa.shape, fix), pl.BlockSpec(wb.shape, fix),
                  pl.BlockSpec(wc.shape, fix), pl.BlockSpec(wo.shape, fix)],
        out_specs=pl.BlockSpec((tm, d), row),
        out_shape=jax.ShapeDtypeStruct((t, d), F32),
        compiler_params=_params("parallel"),
        name="merge",
    )(h, ya, yb, yc, p_gate, wa, wb, wc, wo)


def _pad_cols(w, n):
    return jnp.pad(w, ((0, 0), (0, n - w.shape[1])))


def _layer_params(i, p):
    d = p["w_in"].shape[1]
    w_in = p["w_in"][i]
    o_kv = MLA_Q_RANK + MLA_KV_RANK
    o_sg = o_kv + MLA_ROPE
    o_rw = o_sg + 2 * SG_WIDTH
    o_gate = o_rw + RW_COLS
    zeros = lambda n: jnp.zeros((d, n), F32)
    w_mla = jnp.concatenate([w_in[:, :o_kv], zeros(MLA_NOPE), w_in[:, o_kv:o_sg],
                             zeros(LANES - MLA_QK)], axis=1)
    wuq = p["mla_w_uq"][i].reshape(MLA_Q_RANK, MLA_HEADS, MLA_QK)
    wuq = jnp.pad(wuq, ((0, 0), (0, 0), (0, LANES - MLA_QK))).reshape(MLA_Q_RANK, MLA_HEADS * LANES)
    wukv = p["mla_w_ukv"][i].reshape(MLA_KV_RANK, MLA_HEADS, MLA_NOPE + MLA_V)
    wuk = jnp.pad(wukv[:, :, :MLA_NOPE], ((0, 0), (0, 0), (0, LANES - MLA_NOPE)))
    wuk = wuk.reshape(MLA_KV_RANK, MLA_HEADS * LANES)
    wuv = wukv[:, :, MLA_NOPE:].reshape(MLA_KV_RANK, MLA_HEADS * MLA_V)
    head_of = jnp.arange(RW_WIDTH) // RW_HEAD
    lp = {
        "w_mla": w_mla.astype(BF16),
        "w_sg": w_in[:, o_sg:o_rw].astype(BF16),
        "w_rw": w_in[:, o_rw:o_gate].astype(BF16),
        "w_gate": w_in[:, o_gate:].astype(BF16),
        "qa": p["mla_q_a_norm"][i].reshape(1, -1),
        "kva": p["mla_kv_a_norm"][i].reshape(1, -1),
        "wuq": wuq.astype(BF16), "wuk": wuk.astype(BF16), "wuv": wuv.astype(BF16),
        "qn": _pad_cols(p["mla_q_norm"][i].reshape(1, -1), LANES),
        "kn": _pad_cols(p["mla_k_norm"][i].reshape(1, -1), LANES),
        "sg_vn": p["sg_v_norm"][i].reshape(1, -1),
        "sg_w": p["sg_w_s"][i].transpose(1, 0, 2).reshape(SG_CHUNK, SG_GROUPS * SG_CHUNK),
        "sg_b": jnp.repeat(p["sg_b_s"][i].T, SG_WIDTH // SG_GROUPS, axis=1),
        "rw_mu": p["rw_mu"][i].reshape(1, -1),
        "rw_w0": p["rw_w0"][i].reshape(1, -1),
        "rw_w2": jnp.pad(p["rw_w2"][i], ((0, RW_A_RANK), (0, 0))),
        "rw_a0": p["rw_a0"][i].reshape(1, -1),
        "rw_a2": jnp.pad(p["rw_a2"][i], ((RW_DECAY_RANK, 0), (0, 0))),
        "rw_g2": p["rw_g2"][i].astype(BF16),
        "rw_k_k": p["rw_k_k"][i].reshape(1, -1),
        "rw_k_a": p["rw_k_a"][i].reshape(1, -1),
        "rw_r_k": p["rw_r_k"][i].reshape(1, -1),
        "rw_ln_g": p["rw_ln_g"][i].reshape(1, -1),
        "rw_ln_b": p["rw_ln_b"][i].reshape(1, -1),
        "rw_bd": (head_of[:, None] == head_of[None, :]).astype(BF16),
        "w_out_mla": p["w_out_mla"][i].astype(BF16),
        "w_out_sg": p["w_out_sg"][i].astype(BF16),
        "w_out_rw": p["w_out_rw"][i].astype(BF16),
        "w_o": p["w_o"][i].astype(BF16),
    }
    if i > 0:
        lp["rw_v0"] = p["rw_v0"][i - 1].reshape(1, -1)
        lp["rw_v1"] = _pad_cols(p["rw_v1"][i - 1], LANES)
        lp["rw_v2"] = jnp.pad(p["rw_v2"][i - 1], ((0, LANES - p["rw_v2"].shape[1]), (0, 0)))
    for name in ("ffn1", "ffn2"):
        lp[name + "_norm"] = p[name + "_norm"][i]
        for w in ("w_gate", "w_up", "w_down"):
            lp[f"{name}_{w}"] = p[f"{name}_{w}"][i].astype(BF16)
    return lp


def _rope_tables(positions):
    inv_freq = ROPE_BASE ** (-jnp.arange(0, MLA_ROPE, 2, dtype=F32) / MLA_ROPE)
    ang = positions.astype(F32).reshape(-1, 1) * inv_freq
    cos, sin = jnp.cos(ang), jnp.sin(ang)
    t = ang.shape[0]
    cos_full = jnp.concatenate([jnp.ones((t, MLA_NOPE), F32), cos, cos,
                                jnp.ones((t, LANES - MLA_QK), F32)], axis=1)
    sin_signed = jnp.concatenate([jnp.zeros((t, MLA_NOPE), F32), -sin, sin,
                                  jnp.zeros((t, LANES - MLA_QK), F32)], axis=1)
    return cos_full, sin_signed


def kernel(x, positions, ffn1_norm, ffn1_w_gate, ffn1_w_up, ffn1_w_down, mix_norm, w_in, mla_q_a_norm, mla_w_uq, mla_kv_a_norm, mla_w_ukv, mla_q_norm, mla_k_norm, sg_v_norm, sg_w_s, sg_b_s, rw_mu, rw_w0, rw_w2, rw_a0, rw_a2, rw_g2, rw_k_k, rw_k_a, rw_r_k, rw_ln_g, rw_ln_b, rw_v0, rw_v1, rw_v2, w_out_mla, w_out_sg, w_out_rw, w_o, ffn2_norm, ffn2_w_gate, ffn2_w_up, ffn2_w_down):
    params = dict(ffn1_norm=ffn1_norm, ffn1_w_gate=ffn1_w_gate, ffn1_w_up=ffn1_w_up, ffn1_w_down=ffn1_w_down,
                  mix_norm=mix_norm, w_in=w_in, mla_q_a_norm=mla_q_a_norm, mla_w_uq=mla_w_uq,
                  mla_kv_a_norm=mla_kv_a_norm, mla_w_ukv=mla_w_ukv, mla_q_norm=mla_q_norm, mla_k_norm=mla_k_norm,
                  sg_v_norm=sg_v_norm, sg_w_s=sg_w_s, sg_b_s=sg_b_s, rw_mu=rw_mu, rw_w0=rw_w0, rw_w2=rw_w2,
                  rw_a0=rw_a0, rw_a2=rw_a2, rw_g2=rw_g2, rw_k_k=rw_k_k, rw_k_a=rw_k_a, rw_r_k=rw_r_k,
                  rw_ln_g=rw_ln_g, rw_ln_b=rw_ln_b, rw_v0=rw_v0, rw_v1=rw_v1, rw_v2=rw_v2,
                  w_out_mla=w_out_mla, w_out_sg=w_out_sg, w_out_rw=w_out_rw, w_o=w_o,
                  ffn2_norm=ffn2_norm, ffn2_w_gate=ffn2_w_gate, ffn2_w_up=ffn2_w_up, ffn2_w_down=ffn2_w_down)
    batch, seq, d = x.shape
    depth = w_in.shape[0]
    cos_full, sin_signed = _rope_tables(positions)
    h = x.reshape(batch * seq, d)
    v_first = None
    for i in range(depth):
        lp = _layer_params(i, params)
        h = ffn_half_step(h, lp["ffn1_norm"], lp["ffn1_w_gate"], lp["ffn1_w_up"], lp["ffn1_w_down"])
        g_mix = mix_norm[i]
        c_mla = rms_matmul(h, g_mix, lp["w_mla"], tm=1024, tn=lp["w_mla"].shape[1])
        p_sg = rms_matmul(h, g_mix, lp["w_sg"], tm=1024, tn=512)
        p_rw = rms_matmul(h, g_mix, lp["w_rw"], tm=1024, tn=RW_COLS // 2)
        p_gate = rms_matmul(h, g_mix, lp["w_gate"], tm=1024, tn=512)
        q, k, v = mla_prep(c_mla, cos_full, sin_signed, lp["qa"], lp["kva"], lp["wuq"], lp["wuk"], lp["wuv"],
                           lp["qn"], lp["kn"])
        y_a = mla_attention(q, k, v, batch=batch, seq=seq)
        y_b = sgu(p_sg, lp["sg_vn"], lp["sg_w"], lp["sg_b"])
        r, lw, kr, vr, a, b, g, bonus = rw_prep(p_rw, lp, v_first, seq=seq)
        if v_first is None:
            v_first = vr
        y_scan = rw_scan(r, lw, kr, vr, a, b, batch=batch, seq=seq)
        y_c = rw_post(y_scan, bonus, g, lp["rw_ln_g"], lp["rw_ln_b"], lp["rw_bd"])
        h = merge(h, y_a, y_b, y_c, p_gate, lp["w_out_mla"], lp["w_out_sg"], lp["w_out_rw"], lp["w_o"])
        h = ffn_half_step(h, lp["ffn2_norm"], lp["ffn2_w_gate"], lp["ffn2_w_up"], lp["ffn2_w_down"])
    return h.reshape(batch, seq, d)
```

```python
import functools

import jax
import jax.numpy as jnp
from jax import lax
from jax.experimental import pallas as pl
from jax.experimental.pallas import tpu as pltpu

F32 = jnp.float32
BF16 = jnp.bfloat16

LANES = 128
EPS = 1e-6
RW_GN_EPS = 64e-5
ROPE_BASE = 10000.0

MLA_HEADS = 8
MLA_NOPE = 64
MLA_ROPE = 32
MLA_QK = MLA_NOPE + MLA_ROPE
MLA_V = 64
MLA_Q_RANK = 256
MLA_KV_RANK = 128
SG_WIDTH = 512
SG_GROUPS = 8
SG_CHUNK = 128
RW_HEADS = 8
RW_HEAD = 64
RW_WIDTH = RW_HEADS * RW_HEAD
RW_DECAY_RANK = 64
RW_A_RANK = 64
RW_GATE_RANK = 128
RW_COLS = 3 * RW_WIDTH + RW_DECAY_RANK + RW_A_RANK + RW_GATE_RANK

VMEM_LIMIT = 56 * 1024 * 1024
SCAN_CHUNK = 128
SCAN_STEP = 256
ATTN_BLOCK = 256
PROJ_TILE = 512
PREV_ROWS = 16
PROJ_GATE = 2048
PROJ_MLA = PROJ_GATE + 3 * 1024
PROJ_SG = PROJ_MLA + 512
PROJ_COLS = PROJ_SG + 2 * SG_WIDTH
LOG2E = 1.4426950408889634


def _params(*sem):
    return pltpu.CompilerParams(dimension_semantics=sem, vmem_limit_bytes=VMEM_LIMIT)


def _dot(a, b):
    return jnp.dot(a.astype(BF16), b.astype(BF16), preferred_element_type=F32)


def _dot_x3(a, b):
    ah = a.astype(BF16)
    al = (a - ah.astype(F32)).astype(BF16)
    bh = b.astype(BF16)
    bl = (b - bh.astype(F32)).astype(BF16)
    f = lambda x, y: jnp.dot(x, y, preferred_element_type=F32)
    return f(ah, bh) + f(al, bh) + f(ah, bl)


def _dot_nt(a, b):
    return lax.dot_general(a, b, (((1,), (1,)), ((), ())), preferred_element_type=F32)


def _rms(x, g, n=None):
    n = x.shape[-1] if n is None else n
    ms = jnp.sum(x * x, axis=-1, keepdims=True) * (1.0 / n)
    return x * lax.rsqrt(ms + EPS) * g


def _segsum(x, bd):
    hi = x.astype(BF16)
    lo = (x - hi.astype(F32)).astype(BF16)
    return (jnp.dot(hi, bd, preferred_element_type=F32) + jnp.dot(lo, bd, preferred_element_type=F32))


def _rms_matmul_kernel(h_ref, g_ref, w_ref, o_ref, z_ref):
    @pl.when(pl.program_id(1) == 0)
    def _():
        z_ref[...] = _rms(h_ref[...], g_ref[...]).astype(BF16)

    o_ref[...] = jnp.dot(z_ref[...], w_ref[...], preferred_element_type=F32).astype(o_ref.dtype)


def rms_matmul(h, g, w, *, tm, tn, out_dtype=F32):
    t, d = h.shape
    n = w.shape[1]
    return pl.pallas_call(
        _rms_matmul_kernel,
        grid=(t // tm, n // tn),
        in_specs=[pl.BlockSpec((tm, d), lambda i, j: (i, 0)),
                  pl.BlockSpec((1, d), lambda i, j: (0, 0)),
                  pl.BlockSpec((d, tn), lambda i, j: (0, j))],
        out_specs=pl.BlockSpec((tm, tn), lambda i, j: (i, j)),
        out_shape=jax.ShapeDtypeStruct((t, n), out_dtype),
        scratch_shapes=[pltpu.VMEM((tm, d), BF16)],
        compiler_params=_params("parallel", "arbitrary"),
        name="rms_matmul",
    )(h, g.reshape(1, d), w)


def _ffn_kernel(h_ref, g_ref, wg_ref, wu_ref, wd_ref, o_ref, z_ref, acc_ref):
    j = pl.program_id(1)

    @pl.when(j == 0)
    def _():
        z_ref[...] = _rms(h_ref[...], g_ref[...]).astype(BF16)
        acc_ref[...] = jnp.zeros_like(acc_ref)

    z = z_ref[...]
    a = jnp.dot(z, wg_ref[...], preferred_element_type=F32)
    b = jnp.dot(z, wu_ref[...], preferred_element_type=F32)
    t = (a * jax.nn.sigmoid(a) * b).astype(BF16)
    acc_ref[...] += jnp.dot(t, wd_ref[...], preferred_element_type=F32)

    @pl.when(j == pl.num_programs(1) - 1)
    def _():
        o_ref[...] = h_ref[...] + 0.5 * acc_ref[...]


def ffn_half_step(h, g, wg, wu, wd, *, tm=1024, tf=256):
    t, d = h.shape
    f = wg.shape[1]
    return pl.pallas_call(
        _ffn_kernel,
        grid=(t // tm, f // tf),
        in_specs=[pl.BlockSpec((tm, d), lambda i, j: (i, 0)),
                  pl.BlockSpec((1, d), lambda i, j: (0, 0)),
                  pl.BlockSpec((d, tf), lambda i, j: (0, j)),
                  pl.BlockSpec((d, tf), lambda i, j: (0, j)),
                  pl.BlockSpec((tf, d), lambda i, j: (j, 0))],
        out_specs=pl.BlockSpec((tm, d), lambda i, j: (i, 0)),
        out_shape=jax.ShapeDtypeStruct((t, d), F32),
        scratch_shapes=[pltpu.VMEM((tm, d), BF16), pltpu.VMEM((tm, d), F32)],
        compiler_params=_params("parallel", "arbitrary"),
        name="ffn_half_step",
    )(h, g.reshape(1, d), wg, wu, wd)


def _rope(x, cos_full, sin_signed, lane):
    rot = jnp.where(lane < MLA_NOPE + MLA_ROPE // 2,
                    pltpu.roll(x, LANES - MLA_ROPE // 2, axis=1),
                    pltpu.roll(x, MLA_ROPE // 2, axis=1))
    return x * cos_full + rot * sin_signed


def _mla_prep_kernel(c_ref, cos_ref, sin_ref, qa_ref, kva_ref, wuq_ref, wuk_ref, wuvt_ref, qn_ref, kn_ref,
                     q_ref, k_ref, vt_ref):
    c = c_ref[...].astype(F32)
    zq = _rms(c[:, :MLA_Q_RANK], qa_ref[...]).astype(BF16)
    zkv = _rms(c[:, MLA_Q_RANK:MLA_Q_RANK + MLA_KV_RANK], kva_ref[...]).astype(BF16)
    kpe = c[:, MLA_Q_RANK + MLA_KV_RANK:]
    q = jnp.dot(zq, wuq_ref[...], preferred_element_type=F32)
    kn = jnp.dot(zkv, wuk_ref[...], preferred_element_type=F32)
    for j in range(vt_ref.shape[0]):
        rows = slice(j * ATTN_BLOCK, (j + 1) * ATTN_BLOCK)
        vt_ref[j] = _dot_nt(wuvt_ref[...], zkv[rows]).astype(BF16)
    cos_full = cos_ref[...]
    sin_signed = sin_ref[...]
    lane = lax.broadcasted_iota(jnp.int32, (c.shape[0], LANES), 1)
    scale = MLA_QK ** -0.5 * LOG2E
    for h in range(MLA_HEADS):
        sl = slice(h * LANES, (h + 1) * LANES)
        qh = _rope(_rms(q[:, sl], qn_ref[...], MLA_QK), cos_full, sin_signed, lane)
        q_ref[:, sl] = (qh * scale).astype(BF16)
        kh = _rope(_rms(kn[:, sl] + kpe, kn_ref[...], MLA_QK), cos_full, sin_signed, lane)
        k_ref[:, sl] = kh.astype(BF16)


def mla_prep(c, cos_full, sin_signed, qa, kva, wuq, wuk, wuvt, qn, kn, *, tm=512):
    t = c.shape[0]
    hw = MLA_HEADS * LANES
    vw = MLA_HEADS * MLA_V
    row = lambda i: (i, 0)
    fix = lambda i: (0, 0)
    return pl.pallas_call(
        _mla_prep_kernel,
        grid=(t // tm,),
        in_specs=[pl.BlockSpec((tm, PROJ_SG - PROJ_MLA), lambda i: (i, PROJ_MLA // (PROJ_SG - PROJ_MLA))),
                  pl.BlockSpec((tm, LANES), row),
                  pl.BlockSpec((tm, LANES), row),
                  pl.BlockSpec((1, MLA_Q_RANK), fix),
                  pl.BlockSpec((1, MLA_KV_RANK), fix),
                  pl.BlockSpec(wuq.shape, fix),
                  pl.BlockSpec(wuk.shape, fix),
                  pl.BlockSpec(wuvt.shape, fix),
                  pl.BlockSpec((1, LANES), fix),
                  pl.BlockSpec((1, LANES), fix)],
        out_specs=[pl.BlockSpec((tm, hw), row),
                   pl.BlockSpec((tm, hw), row),
                   pl.BlockSpec((tm // ATTN_BLOCK, vw, ATTN_BLOCK), lambda i: (i, 0, 0))],
        out_shape=[jax.ShapeDtypeStruct((t, hw), BF16),
                   jax.ShapeDtypeStruct((t, hw), BF16),
                   jax.ShapeDtypeStruct((t // ATTN_BLOCK, vw, ATTN_BLOCK), BF16)],
        compiler_params=_params("parallel"),
        name="mla_prep",
    )(c, cos_full, sin_signed, qa, kva, wuq, wuk, wuvt, qn, kn)


def _attn_kernel(q_ref, k_ref, vt_ref, o_ref):
    i = pl.program_id(2)
    blk = ATTN_BLOCK
    q = q_ref[...]
    krow = lax.broadcasted_iota(jnp.int32, (blk, blk), 0)
    qcol = lax.broadcasted_iota(jnp.int32, (blk, blk), 1)

    def scores(j):
        ks = k_ref[pl.ds(pl.multiple_of(j * blk, blk), blk), :]
        return tuple(_dot_nt(ks[:, hh * LANES:(hh + 1) * LANES], q[:, hh * LANES:(hh + 1) * LANES])
                     for hh in range(2))

    def softmax_step(j, sts, stats, masked):
        vts = vt_ref[j]
        if masked:
            sts = [jnp.where(krow <= qcol, st, -jnp.inf) for st in sts]
        m_new = [jnp.maximum(stats[hh][0], jnp.max(sts[hh], axis=0, keepdims=True)) for hh in range(2)]
        ps = [jnp.exp2(sts[hh] - m_new[hh]) for hh in range(2)]
        pv = [jnp.dot(vts[hh * MLA_V:(hh + 1) * MLA_V], ps[hh].astype(BF16), preferred_element_type=F32)
              for hh in range(2)]
        out = []
        for hh in range(2):
            m, l, acc = stats[hh]
            alpha = jnp.exp2(m - m_new[hh])
            l = alpha * l + jnp.sum(ps[hh], axis=0, keepdims=True)
            out.append((m_new[hh], l, alpha * acc + pv[hh]))
        return tuple(out)

    def body(j, carry):
        sts, stats = carry
        nxt = scores(j + 1)
        return nxt, softmax_step(j, sts, stats, False)

    init = tuple((jnp.full((1, blk), -jnp.inf, F32), jnp.zeros((1, blk), F32), jnp.zeros((MLA_V, blk), F32))
                 for _ in range(2))
    sts, stats = lax.fori_loop(0, i, body, (scores(0), init))
    (_, l0, a0), (_, l1, a1) = softmax_step(i, sts, stats, True)
    o_ref[...] = jnp.concatenate([a0 / l0, a1 / l1], axis=0).T.astype(o_ref.dtype)


def mla_attention(q, k, vt, *, batch, seq):
    t = q.shape[0]
    nq = seq // ATTN_BLOCK
    return pl.pallas_call(
        _attn_kernel,
        grid=(batch, MLA_HEADS // 2, nq),
        in_specs=[pl.BlockSpec((ATTN_BLOCK, 2 * LANES), lambda b, p, i: (b * nq + i, p)),
                  pl.BlockSpec((seq, 2 * LANES), lambda b, p, i: (b, p)),
                  pl.BlockSpec((nq, 2 * MLA_V, ATTN_BLOCK), lambda b, p, i: (b, p, 0))],
        out_specs=pl.BlockSpec((ATTN_BLOCK, LANES), lambda b, p, i: (b * nq + i, p)),
        out_shape=jax.ShapeDtypeStruct((t, MLA_HEADS * MLA_V), BF16),
        compiler_params=_params("parallel", "parallel", "arbitrary"),
        name="mla_attention",
    )(q, k, vt)


def _sgu_kernel(pu_ref, pv_ref, vn_ref, w_ref, b_ref, o_ref):
    u = jax.nn.gelu(pu_ref[...].astype(F32), approximate=True)
    v = _rms(jax.nn.gelu(pv_ref[...].astype(F32), approximate=True), vn_ref[...])
    gw = SG_WIDTH // SG_GROUPS
    lane_group = lax.broadcasted_iota(jnp.int32, (SG_CHUNK, SG_WIDTH), 1) // gw
    wrow = lax.broadcasted_iota(jnp.int32, (SG_CHUNK, SG_GROUPS * SG_CHUNK), 0)
    wcol = lax.broadcasted_iota(jnp.int32, (SG_CHUNK, SG_GROUPS * SG_CHUNK), 1) % SG_CHUNK
    w = jnp.where(wcol <= wrow, w_ref[...], 0.0).astype(BF16)
    for c in range(u.shape[0] // SG_CHUNK):
        rows = slice(c * SG_CHUNK, (c + 1) * SG_CHUNK)
        vc = v[rows]
        stacked = jnp.concatenate(
            [jnp.where(lane_group == g, vc, 0.0).astype(BF16) for g in range(SG_GROUPS)], axis=0)
        mixed = jnp.dot(w, stacked, preferred_element_type=F32) + b_ref[...]
        o_ref[rows, :] = (u[rows] * mixed).astype(o_ref.dtype)


def sgu(proj, v_norm, w_cat, b_full, *, tm=512):
    t = proj.shape[0]
    fix = lambda i: (0, 0)
    return pl.pallas_call(
        _sgu_kernel,
        grid=(t // tm,),
        in_specs=[pl.BlockSpec((tm, SG_WIDTH), lambda i: (i, PROJ_SG // SG_WIDTH)),
                  pl.BlockSpec((tm, SG_WIDTH), lambda i: (i, PROJ_SG // SG_WIDTH + 1)),
                  pl.BlockSpec((1, SG_WIDTH), fix),
                  pl.BlockSpec(w_cat.shape, fix),
                  pl.BlockSpec(b_full.shape, fix)],
        out_specs=pl.BlockSpec((tm, SG_WIDTH), lambda i: (i, 0)),
        out_shape=jax.ShapeDtypeStruct((t, SG_WIDTH), BF16),
        compiler_params=_params("parallel"),
        name="sgu",
    )(proj, proj, v_norm, w_cat, b_full)


def _rw_prep_kernel(*refs, tiles_per_seq, has_vres):
    if has_vres:
        (p_ref, prev_ref, mu_ref, w0_ref, w2_ref, a0_ref, a2_ref, g2_ref, kk_ref, ka_ref, rk_ref, bd_ref,
         vf_ref, v0_ref, v1_ref, v2_ref,
         r_o, lw_o, k_o, v_o, a_o, b_o, g_o, bonus_o) = refs
    else:
        (p_ref, prev_ref, mu_ref, w0_ref, w2_ref, a0_ref, a2_ref, g2_ref, kk_ref, ka_ref, rk_ref, bd_ref,
         r_o, lw_o, k_o, v_o, a_o, b_o, g_o, bonus_o) = refs
    cur = p_ref[...].astype(F32)
    tm = cur.shape[0]
    first = (pl.program_id(0) % tiles_per_seq) == 0
    last_prev = jnp.where(first, 0.0, prev_ref[PREV_ROWS - 1:PREV_ROWS, :].astype(F32))
    rowi = lax.broadcasted_iota(jnp.int32, cur.shape, 0)
    prev = jnp.where(rowi == 0, last_prev, pltpu.roll(cur, 1, axis=0))
    x = cur + (prev - cur) * mu_ref[...]
    w3 = 3 * RW_WIDTH
    r = x[:, :RW_WIDTH]
    k = x[:, RW_WIDTH:2 * RW_WIDTH]
    v = x[:, 2 * RW_WIDTH:w3]
    xwa = x[:, w3:w3 + LANES]
    xg = x[:, w3 + LANES:]
    w = -jax.nn.softplus(-(w0_ref[...] + _dot_x3(jnp.tanh(xwa), w2_ref[...]))) - 0.5
    lw_o[...] = -jnp.exp(w)
    a = jax.nn.sigmoid(a0_ref[...] + _dot_x3(xwa, a2_ref[...]))
    g_o[...] = _dot(jax.nn.sigmoid(xg), g2_ref[...]).astype(g_o.dtype)
    if has_vres:
        mix = jax.nn.sigmoid(v0_ref[...] + _dot_x3(_dot_x3(v, v1_ref[...]), v2_ref[...]))
        v = v + (vf_ref[...].astype(F32) - v) * mix
    bd = bd_ref[...]
    kk = k * kk_ref[...]
    kk = kk * lax.rsqrt(jnp.maximum(_segsum(kk * kk, bd), 1e-24))
    k = k * (1.0 + (a - 1.0) * ka_ref[...])
    r_o[...] = r.astype(r_o.dtype)
    k_o[...] = k.astype(k_o.dtype)
    v_o[...] = v.astype(v_o.dtype)
    a_o[...] = (-kk).astype(a_o.dtype)
    b_o[...] = (kk * a).astype(b_o.dtype)
    bonus_o[...] = (_segsum(r * k * rk_ref[...], bd) * v).astype(bonus_o.dtype)


def rw_prep(proj, lp, v_first, *, seq, tm=256):
    t = proj.shape[0]
    has_vres = v_first is not None
    row = lambda i: (i, 0)
    fix = lambda i: (0, 0)
    vec = pl.BlockSpec((1, RW_WIDTH), fix)
    in_specs = [pl.BlockSpec((tm, RW_COLS), row),
                pl.BlockSpec((PREV_ROWS, RW_COLS), lambda i: (jnp.maximum(i * (tm // PREV_ROWS) - 1, 0), 0)),
                pl.BlockSpec((1, RW_COLS), fix),
                vec, pl.BlockSpec((LANES, RW_WIDTH), fix),
                vec, pl.BlockSpec((LANES, RW_WIDTH), fix),
                pl.BlockSpec((RW_GATE_RANK, RW_WIDTH), fix),
                vec, vec, vec, pl.BlockSpec((RW_WIDTH, RW_WIDTH), fix)]
    args = [proj, proj, lp["rw_mu"], lp["rw_w0"], lp["rw_w2"], lp["rw_a0"], lp["rw_a2"], lp["rw_g2"],
            lp["rw_k_k"], lp["rw_k_a"], lp["rw_r_k"], lp["rw_bd"]]
    if has_vres:
        in_specs += [pl.BlockSpec((tm, RW_WIDTH), row), vec,
                     pl.BlockSpec(lp["rw_v1"].shape, fix), pl.BlockSpec(lp["rw_v2"].shape, fix)]
        args += [v_first, lp["rw_v0"], lp["rw_v1"], lp["rw_v2"]]
    dtypes = [BF16, F32, BF16, BF16, BF16, BF16, BF16, BF16]
    return pl.pallas_call(
        functools.partial(_rw_prep_kernel, tiles_per_seq=seq // tm, has_vres=has_vres),
        grid=(t // tm,),
        in_specs=in_specs,
        out_specs=[pl.BlockSpec((tm, RW_WIDTH), row)] * 8,
        out_shape=[jax.ShapeDtypeStruct((t, RW_WIDTH), dt) for dt in dtypes],
        compiler_params=_params("parallel"),
        name="rw_prep",
    )(*args)


def _cumsum_rows(tri, x):
    h1 = x.astype(BF16)
    r1 = x - h1.astype(F32)
    h2 = r1.astype(BF16)
    h3 = (r1 - h2.astype(F32)).astype(BF16)
    f = lambda y: jnp.dot(tri, y, preferred_element_type=F32)
    return f(h1) + f(h2) + f(h3)


def _rows_bcast(x, c, which):
    n = x.shape[0] // c
    return jnp.concatenate([jnp.broadcast_to(x[j * c + which:j * c + which + 1], (c, x.shape[1]))
                            for j in range(n)], axis=0)


def _rw_scan_kernel(r_ref, lw_ref, k_ref, v_ref, a_ref, b_ref, y_ref, s_ref):
    c = SCAN_CHUNK
    n_tok = r_ref.shape[0]

    @pl.when(pl.program_id(2) == 0)
    def _():
        s_ref[...] = jnp.zeros_like(s_ref)

    ri = lax.broadcasted_iota(jnp.int32, (n_tok, n_tok), 0)
    ci = lax.broadcasted_iota(jnp.int32, (n_tok, n_tok), 1)
    same = (ri // c) == (ci // c)
    incl = jnp.logical_and(same, ci <= ri)
    strict = jnp.logical_and(same, ci < ri)
    eye = jnp.where(ri == ci, 1.0, 0.0)
    tri = jnp.where(incl, 1.0, 0.0).astype(BF16)
    head0 = lax.broadcasted_iota(jnp.int32, (n_tok, LANES), 1) < RW_HEAD
    sr = lax.broadcasted_iota(jnp.int32, (LANES, LANES), 0)
    sc = lax.broadcasted_iota(jnp.int32, (LANES, LANES), 1)
    same_head = (sr // RW_HEAD) == (sc // RW_HEAD)
    eye_s = jnp.where(sr == sc, 1.0, 0.0).astype(BF16)

    lw = lw_ref[...]
    cum = _cumsum_rows(tri, lw)
    cum_mid = _rows_bcast(cum, c, c // 2 - 1)
    cum_end = _rows_bcast(cum, c, c - 1)
    r = r_ref[...].astype(F32)
    a = a_ref[...].astype(F32)
    k = k_ref[...].astype(F32)
    b = b_ref[...].astype(F32)
    v = v_ref[...].astype(F32)
    r_abs = r * jnp.exp(cum)
    a_abs = a * jnp.exp(cum - lw)
    g_end = jnp.exp(cum_end - cum)
    b_end = (b * g_end).astype(BF16)
    k_end = (k * g_end).astype(BF16)
    g_bwd = jnp.exp(cum_mid - cum)
    r_mid = r * jnp.exp(cum - cum_mid)
    a_mid = a * jnp.exp(cum - lw - cum_mid)
    rhs = jnp.concatenate([b * g_bwd, k * g_bwd], axis=0).astype(BF16)
    w1 = jnp.zeros((n_tok, LANES), F32)
    w2 = jnp.zeros((n_tok, LANES), F32)
    y2 = jnp.zeros((n_tok, LANES), F32)
    mrb = []
    for hh in range(2):
        msk = head0 if hh == 0 else jnp.logical_not(head0)
        lhs = jnp.concatenate([jnp.where(msk, a_mid, 0.0), jnp.where(msk, r_mid, 0.0)], axis=0)
        p = _dot_nt(lhs.astype(BF16), rhs)
        lab = jnp.where(strict, p[:n_tok, :n_tok], 0.0)
        lak = jnp.where(strict, p[:n_tok, n_tok:], 0.0)
        mrb.append(jnp.where(incl, p[n_tok:, :n_tok], 0.0).astype(BF16))
        mrk = jnp.where(incl, p[n_tok:, n_tok:], 0.0)
        vh = jnp.where(msk, v, 0.0)
        inv = eye + lab
        lp = lab
        q = 2
        while q < c:
            lp = _dot(lp, lp)
            inv = inv + _dot(inv, lp)
            q *= 2
        tw = _dot(inv, jnp.concatenate([jnp.where(msk, a_abs, 0.0), _dot(lak, vh)], axis=1))
        w1 = w1 + tw[:, :LANES]
        w2 = w2 + tw[:, LANES:]
        y2 = y2 + _dot(mrk, vh)
    w12 = jnp.concatenate([w1, w2], axis=1).astype(BF16)
    vb = v.astype(BF16)
    zeros_c = jnp.zeros((c, LANES), BF16)
    u_rows = []
    y0_rows = []
    for j in range(n_tok // c):
        rows = slice(j * c, (j + 1) * c)
        s0 = s_ref[...]
        s0b = s0.astype(BF16)
        xy = _dot_nt(jnp.concatenate([w1[rows], r_abs[rows]], axis=0).astype(BF16), s0b)
        u_rows.append(xy[:c] + w2[rows])
        y0_rows.append(xy[c:])
        saug = jnp.concatenate([s0b, eye_s, eye_s], axis=1)
        rhs_t = jnp.concatenate([jnp.concatenate([w12[rows], zeros_c], axis=1),
                                 jnp.concatenate([zeros_c, zeros_c, vb[rows]], axis=1)], axis=0)
        uvt = _dot_nt(saug, rhs_t)
        upd = _dot(uvt, jnp.concatenate([b_end[rows], k_end[rows]], axis=0))
        s_ref[...] = s0 * jnp.exp(cum_end[j * c:j * c + 1, :]) + jnp.where(same_head, upd, 0.0)
    ub = jnp.concatenate(u_rows, axis=0).astype(BF16)
    y1 = [jnp.dot(mrb[hh], ub, preferred_element_type=F32) for hh in range(2)]
    y_ref[...] = jnp.concatenate(y0_rows, axis=0) + y2 + jnp.where(head0, y1[0], y1[1])


def rw_scan(r, lw, k, v, a, b, *, batch, seq):
    t = r.shape[0]
    ns = seq // SCAN_STEP
    spec = pl.BlockSpec((SCAN_STEP, LANES), lambda bi, p, i: (bi * ns + i, p))
    return pl.pallas_call(
        _rw_scan_kernel,
        grid=(batch, RW_WIDTH // LANES, ns),
        in_specs=[spec] * 6,
        out_specs=spec,
        out_shape=jax.ShapeDtypeStruct((t, RW_WIDTH), F32),
        scratch_shapes=[pltpu.VMEM((LANES, LANES), F32)],
        compiler_params=_params("parallel", "parallel", "arbitrary"),
        name="rw_scan",
    )(r, lw, k, v, a, b)


def _rw_post_kernel(y_ref, bonus_ref, g_ref, lng_ref, lnb_ref, bd_ref, o_ref):
    y = y_ref[...]
    bd = bd_ref[...]
    mean = _segsum(y, bd) * (1.0 / RW_HEAD)
    d = y - mean
    var = _segsum(d * d, bd) * (1.0 / RW_HEAD)
    yn = d * lax.rsqrt(var + RW_GN_EPS) * lng_ref[...] + lnb_ref[...]
    o_ref[...] = ((yn + bonus_ref[...].astype(F32)) * g_ref[...].astype(F32)).astype(o_ref.dtype)


def rw_post(y, bonus, g, ln_g, ln_b, bd, *, tm=512):
    t = y.shape[0]
    row = pl.BlockSpec((tm, RW_WIDTH), lambda i: (i, 0))
    vec = pl.BlockSpec((1, RW_WIDTH), lambda i: (0, 0))
    return pl.pallas_call(
        _rw_post_kernel,
        grid=(t // tm,),
        in_specs=[row, row, row, vec, vec, pl.BlockSpec((RW_WIDTH, RW_WIDTH), lambda i: (0, 0))],
        out_specs=row,
        out_shape=jax.ShapeDtypeStruct((t, RW_WIDTH), BF16),
        compiler_params=_params("parallel"),
        name="rw_post",
    )(y, bonus, g, ln_g, ln_b, bd)


def _merge_kernel(h_ref, ya_ref, yb_ref, yc_ref, ga_ref, gb_ref, gc_ref, wa_ref, wb_ref, wc_ref, wo_ref, o_ref):
    gate = lambda g_ref: jax.nn.sigmoid(g_ref[...].astype(F32))
    merged = (gate(ga_ref) * jnp.dot(ya_ref[...], wa_ref[...], preferred_element_type=F32)
              + gate(gb_ref) * jnp.dot(yb_ref[...], wb_ref[...], preferred_element_type=F32)
              + gate(gc_ref) * jnp.dot(yc_ref[...], wc_ref[...], preferred_element_type=F32))
    o_ref[...] = h_ref[...] + jnp.dot(merged.astype(BF16), wo_ref[...], preferred_element_type=F32)


def merge(h, ya, yb, yc, proj, wa, wb, wc, wo, *, tm=512):
    t, d = h.shape
    row = lambda i: (i, 0)
    fix = lambda i: (0, 0)
    gate_specs = [pl.BlockSpec((tm, d), functools.partial(lambda i, n: (i, PROJ_GATE // d + n), n=n))
                  for n in range(3)]
    return pl.pallas_call(
        _merge_kernel,
        grid=(t // tm,),
        in_specs=[pl.BlockSpec((tm, d), row),
                  pl.BlockSpec((tm, ya.shape[1]), row),
                  pl.BlockSpec((tm, yb.shape[1]), row),
                  pl.BlockSpec((tm, yc.shape[1]), row),
                  *gate_specs,
                  pl.BlockSpec(wa.shape, fix), pl.BlockSpec(wb.shape, fix),
                  pl.BlockSpec(wc.shape, fix), pl.BlockSpec(wo.shape, fix)],
        out_specs=pl.BlockSpec((tm, d), row),
        out_shape=jax.ShapeDtypeStruct((t, d), F32),
        compiler_params=_params("parallel"),
        name="merge",
    )(h, ya, yb, yc, proj, proj, proj, wa, wb, wc, wo)


def _pad_cols(w, n):
    return jnp.pad(w, ((0, 0), (0, n - w.shape[1])))


def _layer_params(i, p):
    d = p["w_in"].shape[1]
    w_in = p["w_in"][i]
    o_kv = MLA_Q_RANK + MLA_KV_RANK
    o_sg = o_kv + MLA_ROPE
    o_rw = o_sg + 2 * SG_WIDTH
    o_gate = o_rw + RW_COLS
    zeros = lambda n: jnp.zeros((d, n), F32)
    w_mla = jnp.concatenate([w_in[:, :o_kv], zeros(MLA_NOPE), w_in[:, o_kv:o_sg],
                             zeros(LANES - MLA_QK)], axis=1)
    wuq = p["mla_w_uq"][i].reshape(MLA_Q_RANK, MLA_HEADS, MLA_QK)
    wuq = jnp.pad(wuq, ((0, 0), (0, 0), (0, LANES - MLA_QK))).reshape(MLA_Q_RANK, MLA_HEADS * LANES)
    wukv = p["mla_w_ukv"][i].reshape(MLA_KV_RANK, MLA_HEADS, MLA_NOPE + MLA_V)
    wuk = jnp.pad(wukv[:, :, :MLA_NOPE], ((0, 0), (0, 0), (0, LANES - MLA_NOPE)))
    wuk = wuk.reshape(MLA_KV_RANK, MLA_HEADS * LANES)
    wuv = wukv[:, :, MLA_NOPE:].reshape(MLA_KV_RANK, MLA_HEADS * MLA_V)
    head_of = jnp.arange(RW_WIDTH) // RW_HEAD
    w_proj = jnp.concatenate([w_in[:, o_rw:o_gate], zeros(PROJ_GATE - RW_COLS), w_in[:, o_gate:],
                              w_mla, w_in[:, o_sg:o_rw]], axis=1)
    lp = {
        "w_proj": w_proj.astype(BF16),
        "qa": p["mla_q_a_norm"][i].reshape(1, -1),
        "kva": p["mla_kv_a_norm"][i].reshape(1, -1),
        "wuq": wuq.astype(BF16), "wuk": wuk.astype(BF16), "wuvt": wuv.T.astype(BF16),
        "qn": _pad_cols(p["mla_q_norm"][i].reshape(1, -1), LANES),
        "kn": _pad_cols(p["mla_k_norm"][i].reshape(1, -1), LANES),
        "sg_vn": p["sg_v_norm"][i].reshape(1, -1),
        "sg_w": p["sg_w_s"][i].transpose(1, 0, 2).reshape(SG_CHUNK, SG_GROUPS * SG_CHUNK),
        "sg_b": jnp.repeat(p["sg_b_s"][i].T, SG_WIDTH // SG_GROUPS, axis=1),
        "rw_mu": p["rw_mu"][i].reshape(1, -1),
        "rw_w0": p["rw_w0"][i].reshape(1, -1),
        "rw_w2": jnp.pad(p["rw_w2"][i], ((0, RW_A_RANK), (0, 0))),
        "rw_a0": p["rw_a0"][i].reshape(1, -1),
        "rw_a2": jnp.pad(p["rw_a2"][i], ((RW_DECAY_RANK, 0), (0, 0))),
        "rw_g2": p["rw_g2"][i].astype(BF16),
        "rw_k_k": p["rw_k_k"][i].reshape(1, -1),
        "rw_k_a": p["rw_k_a"][i].reshape(1, -1),
        "rw_r_k": p["rw_r_k"][i].reshape(1, -1),
        "rw_ln_g": p["rw_ln_g"][i].reshape(1, -1),
        "rw_ln_b": p["rw_ln_b"][i].reshape(1, -1),
        "rw_bd": (head_of[:, None] == head_of[None, :]).astype(BF16),
        "w_out_mla": p["w_out_mla"][i].astype(BF16),
        "w_out_sg": p["w_out_sg"][i].astype(BF16),
        "w_out_rw": p["w_out_rw"][i].astype(BF16),
        "w_o": p["w_o"][i].astype(BF16),
    }
    if i > 0:
        lp["rw_v0"] = p["rw_v0"][i - 1].reshape(1, -1)
        lp["rw_v1"] = _pad_cols(p["rw_v1"][i - 1], LANES)
        lp["rw_v2"] = jnp.pad(p["rw_v2"][i - 1], ((0, LANES - p["rw_v2"].shape[1]), (0, 0)))
    for name in ("ffn1", "ffn2"):
        lp[name + "_norm"] = p[name + "_norm"][i]
        for w in ("w_gate", "w_up", "w_down"):
            lp[f"{name}_{w}"] = p[f"{name}_{w}"][i].astype(BF16)
    return lp


def _rope_tables(positions):
    inv_freq = ROPE_BASE ** (-jnp.arange(0, MLA_ROPE, 2, dtype=F32) / MLA_ROPE)
    ang = positions.astype(F32).reshape(-1, 1) * inv_freq
    cos, sin = jnp.cos(ang), jnp.sin(ang)
    t = ang.shape[0]
    cos_full = jnp.concatenate([jnp.ones((t, MLA_NOPE), F32), cos, cos,
                                jnp.ones((t, LANES - MLA_QK), F32)], axis=1)
    sin_signed = jnp.concatenate([jnp.zeros((t, MLA_NOPE), F32), -sin, sin,
                                  jnp.zeros((t, LANES - MLA_QK), F32)], axis=1)
    return cos_full, sin_signed


def kernel(x, positions, ffn1_norm, ffn1_w_gate, ffn1_w_up, ffn1_w_down, mix_norm, w_in, mla_q_a_norm, mla_w_uq, mla_kv_a_norm, mla_w_ukv, mla_q_norm, mla_k_norm, sg_v_norm, sg_w_s, sg_b_s, rw_mu, rw_w0, rw_w2, rw_a0, rw_a2, rw_g2, rw_k_k, rw_k_a, rw_r_k, rw_ln_g, rw_ln_b, rw_v0, rw_v1, rw_v2, w_out_mla, w_out_sg, w_out_rw, w_o, ffn2_norm, ffn2_w_gate, ffn2_w_up, ffn2_w_down):
    params = dict(ffn1_norm=ffn1_norm, ffn1_w_gate=ffn1_w_gate, ffn1_w_up=ffn1_w_up, ffn1_w_down=ffn1_w_down,
                  mix_norm=mix_norm, w_in=w_in, mla_q_a_norm=mla_q_a_norm, mla_w_uq=mla_w_uq,
                  mla_kv_a_norm=mla_kv_a_norm, mla_w_ukv=mla_w_ukv, mla_q_norm=mla_q_norm, mla_k_norm=mla_k_norm,
                  sg_v_norm=sg_v_norm, sg_w_s=sg_w_s, sg_b_s=sg_b_s, rw_mu=rw_mu, rw_w0=rw_w0, rw_w2=rw_w2,
                  rw_a0=rw_a0, rw_a2=rw_a2, rw_g2=rw_g2, rw_k_k=rw_k_k, rw_k_a=rw_k_a, rw_r_k=rw_r_k,
                  rw_ln_g=rw_ln_g, rw_ln_b=rw_ln_b, rw_v0=rw_v0, rw_v1=rw_v1, rw_v2=rw_v2,
                  w_out_mla=w_out_mla, w_out_sg=w_out_sg, w_out_rw=w_out_rw, w_o=w_o,
                  ffn2_norm=ffn2_norm, ffn2_w_gate=ffn2_w_gate, ffn2_w_up=ffn2_w_up, ffn2_w_down=ffn2_w_down)
    batch, seq, d = x.shape
    depth = w_in.shape[0]
    cos_full, sin_signed = _rope_tables(positions)
    h = x.reshape(batch * seq, d)
    v_first = None
    for i in range(depth):
        lp = _layer_params(i, params)
        h = ffn_half_step(h, lp["ffn1_norm"], lp["ffn1_w_gate"], lp["ffn1_w_up"], lp["ffn1_w_down"])
        proj = rms_matmul(h, mix_norm[i], lp["w_proj"], tm=1024, tn=PROJ_TILE, out_dtype=BF16)
        q, k, vt = mla_prep(proj, cos_full, sin_signed, lp["qa"], lp["kva"], lp["wuq"], lp["wuk"], lp["wuvt"],
                           lp["qn"], lp["kn"])
        y_a = mla_attention(q, k, vt, batch=batch, seq=seq)
        y_b = sgu(proj, lp["sg_vn"], lp["sg_w"], lp["sg_b"])
        r, lw, kr, vr, a, b, g, bonus = rw_prep(proj, lp, v_first, seq=seq)
        if v_first is None:
            v_first = vr
        y_scan = rw_scan(r, lw, kr, vr, a, b, batch=batch, seq=seq)
        y_c = rw_post(y_scan, bonus, g, lp["rw_ln_g"], lp["rw_ln_b"], lp["rw_bd"])
        h = merge(h, y_a, y_b, y_c, proj, lp["w_out_mla"], lp["w_out_sg"], lp["w_out_rw"], lp["w_o"])
        h = ffn_half_step(h, lp["ffn2_norm"], lp["ffn2_w_gate"], lp["ffn2_w_up"], lp["ffn2_w_down"])
    return h.reshape(batch, seq, d)
```

```python
import functools

import jax
import jax.numpy as jnp
from jax import lax
from jax.experimental import pallas as pl
from jax.experimental.pallas import tpu as pltpu

F32 = jnp.float32
BF16 = jnp.bfloat16

LANES = 128
EPS = 1e-6
RW_GN_EPS = 64e-5
ROPE_BASE = 10000.0

MLA_HEADS = 8
MLA_NOPE = 64
MLA_ROPE = 32
MLA_QK = MLA_NOPE + MLA_ROPE
MLA_V = 64
MLA_Q_RANK = 256
MLA_KV_RANK = 128
SG_WIDTH = 512
SG_GROUPS = 8
SG_CHUNK = 128
RW_HEADS = 8
RW_HEAD = 64
RW_WIDTH = RW_HEADS * RW_HEAD
RW_DECAY_RANK = 64
RW_A_RANK = 64
RW_GATE_RANK = 128
RW_COLS = 3 * RW_WIDTH + RW_DECAY_RANK + RW_A_RANK + RW_GATE_RANK

VMEM_LIMIT = 56 * 1024 * 1024
SCAN_CHUNK = 128
SCAN_STEP = 256
SCAN_PAIRS = 4
ATTN_BLOCK = 256
PROJ_TILE = 512
PREV_ROWS = 16
PROJ_GATE = 2048
PROJ_MLA = PROJ_GATE + 3 * 1024
PROJ_SG = PROJ_MLA + 512
PROJ_COLS = PROJ_SG + 2 * SG_WIDTH
SUM_ROWS = 16
LOG2E = 1.4426950408889634


def _params(*sem):
    return pltpu.CompilerParams(dimension_semantics=sem, vmem_limit_bytes=VMEM_LIMIT)


def _dot(a, b):
    return jnp.dot(a.astype(BF16), b.astype(BF16), preferred_element_type=F32)


def _dot_x3(a, b):
    ah = a.astype(BF16)
    al = (a - ah.astype(F32)).astype(BF16)
    bh = b.astype(BF16)
    bl = (b - bh.astype(F32)).astype(BF16)
    f = lambda x, y: jnp.dot(x, y, preferred_element_type=F32)
    return f(ah, bh) + f(al, bh) + f(ah, bl)


def _dot_nt(a, b):
    return lax.dot_general(a, b, (((1,), (1,)), ((), ())), preferred_element_type=F32)


def _rms(x, g, n=None):
    n = x.shape[-1] if n is None else n
    ms = jnp.sum(x * x, axis=-1, keepdims=True) * (1.0 / n)
    return x * lax.rsqrt(ms + EPS) * g


def _segsum(x, bd):
    hi = x.astype(BF16)
    lo = (x - hi.astype(F32)).astype(BF16)
    return (jnp.dot(hi, bd, preferred_element_type=F32) + jnp.dot(lo, bd, preferred_element_type=F32))


def _rms_matmul_kernel(h_ref, g_ref, w_ref, o_ref, z_ref):
    @pl.when(pl.program_id(1) == 0)
    def _():
        z_ref[...] = _rms(h_ref[...], g_ref[...]).astype(BF16)

    o_ref[...] = jnp.dot(z_ref[...], w_ref[...], preferred_element_type=F32).astype(o_ref.dtype)


def rms_matmul(h, g, w, *, tm, tn, out_dtype=F32):
    t, d = h.shape
    n = w.shape[1]
    return pl.pallas_call(
        _rms_matmul_kernel,
        grid=(t // tm, n // tn),
        in_specs=[pl.BlockSpec((tm, d), lambda i, j: (i, 0)),
                  pl.BlockSpec((1, d), lambda i, j: (0, 0)),
                  pl.BlockSpec((d, tn), lambda i, j: (0, j))],
        out_specs=pl.BlockSpec((tm, tn), lambda i, j: (i, j)),
        out_shape=jax.ShapeDtypeStruct((t, n), out_dtype),
        scratch_shapes=[pltpu.VMEM((tm, d), BF16)],
        compiler_params=_params("parallel", "arbitrary"),
        name="rms_matmul",
    )(h, g.reshape(1, d), w)


def _ffn_kernel(h_ref, g_ref, wg_ref, wu_ref, wd_ref, o_ref, z_ref, acc_ref):
    j = pl.program_id(1)

    @pl.when(j == 0)
    def _():
        z_ref[...] = _rms(h_ref[...], g_ref[...]).astype(BF16)
        acc_ref[...] = jnp.zeros_like(acc_ref)

    z = z_ref[...]
    a = jnp.dot(z, wg_ref[...].astype(BF16), preferred_element_type=F32)
    b = jnp.dot(z, wu_ref[...].astype(BF16), preferred_element_type=F32)
    t = (a * jax.nn.sigmoid(a) * b).astype(BF16)
    acc_ref[...] += jnp.dot(t, wd_ref[...].astype(BF16), preferred_element_type=F32)

    @pl.when(j == pl.num_programs(1) - 1)
    def _():
        o_ref[...] = h_ref[...] + 0.5 * acc_ref[...]


def ffn_half_step(h, g, wg, wu, wd, layer, *, tm=1024, tf=256):
    t, d = h.shape
    f = wg.shape[2]
    return pl.pallas_call(
        _ffn_kernel,
        grid=(t // tm, f // tf),
        in_specs=[pl.BlockSpec((tm, d), lambda i, j: (i, 0)),
                  pl.BlockSpec((None, 1, d), lambda i, j: (layer, 0, 0)),
                  pl.BlockSpec((None, d, tf), lambda i, j: (layer, 0, j)),
                  pl.BlockSpec((None, d, tf), lambda i, j: (layer, 0, j)),
                  pl.BlockSpec((None, tf, d), lambda i, j: (layer, j, 0))],
        out_specs=pl.BlockSpec((tm, d), lambda i, j: (i, 0)),
        out_shape=jax.ShapeDtypeStruct((t, d), F32),
        scratch_shapes=[pltpu.VMEM((tm, d), BF16), pltpu.VMEM((tm, d), F32)],
        compiler_params=_params("parallel", "arbitrary"),
        name="ffn_half_step",
    )(h, g.reshape(g.shape[0], 1, d), wg, wu, wd)


def _rope(x, cos_full, sin_signed, lane):
    rot = jnp.where(lane < MLA_NOPE + MLA_ROPE // 2,
                    pltpu.roll(x, LANES - MLA_ROPE // 2, axis=1),
                    pltpu.roll(x, MLA_ROPE // 2, axis=1))
    return x * cos_full + rot * sin_signed


def _mla_prep_kernel(c_ref, cos_ref, sin_ref, qa_ref, kva_ref, wuq_ref, wuk_ref, wuvt_ref, qn_ref, kn_ref,
                     q_ref, k_ref, vt_ref):
    c = c_ref[...].astype(F32)
    zq = _rms(c[:, :MLA_Q_RANK], qa_ref[...]).astype(BF16)
    zkv = _rms(c[:, MLA_Q_RANK:MLA_Q_RANK + MLA_KV_RANK], kva_ref[...]).astype(BF16)
    kpe = c[:, MLA_Q_RANK + MLA_KV_RANK:]
    q = jnp.dot(zq, wuq_ref[...], preferred_element_type=F32)
    kn = jnp.dot(zkv, wuk_ref[...], preferred_element_type=F32)
    for j in range(vt_ref.shape[0]):
        rows = slice(j * ATTN_BLOCK, (j + 1) * ATTN_BLOCK)
        vt_ref[j] = _dot_nt(wuvt_ref[...], zkv[rows]).astype(BF16)
    cos_full = cos_ref[...]
    sin_signed = sin_ref[...]
    lane = lax.broadcasted_iota(jnp.int32, (c.shape[0], LANES), 1)
    scale = MLA_QK ** -0.5 * LOG2E
    for h in range(MLA_HEADS):
        sl = slice(h * LANES, (h + 1) * LANES)
        qh = _rope(_rms(q[:, sl], qn_ref[...], MLA_QK), cos_full, sin_signed, lane)
        q_ref[:, sl] = (qh * scale).astype(BF16)
        kh = _rope(_rms(kn[:, sl] + kpe, kn_ref[...], MLA_QK), cos_full, sin_signed, lane)
        k_ref[:, sl] = kh.astype(BF16)


def mla_prep(c, cos_full, sin_signed, qa, kva, wuq, wuk, wuvt, qn, kn, *, tm=512):
    t = c.shape[0]
    hw = MLA_HEADS * LANES
    vw = MLA_HEADS * MLA_V
    row = lambda i: (i, 0)
    fix = lambda i: (0, 0)
    return pl.pallas_call(
        _mla_prep_kernel,
        grid=(t // tm,),
        in_specs=[pl.BlockSpec((tm, PROJ_SG - PROJ_MLA), lambda i: (i, PROJ_MLA // (PROJ_SG - PROJ_MLA))),
                  pl.BlockSpec((tm, LANES), row),
                  pl.BlockSpec((tm, LANES), row),
                  pl.BlockSpec((1, MLA_Q_RANK), fix),
                  pl.BlockSpec((1, MLA_KV_RANK), fix),
                  pl.BlockSpec(wuq.shape, fix),
                  pl.BlockSpec(wuk.shape, fix),
                  pl.BlockSpec(wuvt.shape, fix),
                  pl.BlockSpec((1, LANES), fix),
                  pl.BlockSpec((1, LANES), fix)],
        out_specs=[pl.BlockSpec((tm, hw), row),
                   pl.BlockSpec((tm, hw), row),
                   pl.BlockSpec((tm // ATTN_BLOCK, vw, ATTN_BLOCK), lambda i: (i, 0, 0))],
        out_shape=[jax.ShapeDtypeStruct((t, hw), BF16),
                   jax.ShapeDtypeStruct((t, hw), BF16),
                   jax.ShapeDtypeStruct((t // ATTN_BLOCK, vw, ATTN_BLOCK), BF16)],
        compiler_params=_params("parallel"),
        name="mla_prep",
    )(c, cos_full, sin_signed, qa, kva, wuq, wuk, wuvt, qn, kn)


def _attn_kernel(q_ref, k_ref, vt_ref, o_ref, st_scr, p_scr, acc_scr):
    i = pl.program_id(2)
    blk = ATTN_BLOCK
    q = q_ref[...]
    krow = lax.broadcasted_iota(jnp.int32, (blk, blk), 0)
    qcol = lax.broadcasted_iota(jnp.int32, (blk, blk), 1)
    ones_rows = jnp.ones((SUM_ROWS, blk), BF16)

    def put_scores(j, slot):
        ks = k_ref[pl.ds(pl.multiple_of(j * blk, blk), blk), :]
        for hh in range(2):
            st_scr[slot, hh] = _dot_nt(ks[:, hh * LANES:(hh + 1) * LANES], q[:, hh * LANES:(hh + 1) * LANES])

    def weighted_values(j, slot):
        vts = vt_ref[j]
        return [jnp.dot(jnp.concatenate([vts[hh * MLA_V:(hh + 1) * MLA_V], ones_rows], axis=0), p_scr[slot, hh],
                        preferred_element_type=F32) for hh in range(2)]

    def step(j, slot, alphas_prev, ms, masked, prefetch):
        if prefetch:
            put_scores(j + 1, 1 - slot)
        pv = weighted_values(jnp.maximum(j - 1, 0), 1 - slot)
        sts = [st_scr[slot, hh] for hh in range(2)]
        if masked:
            keep = krow + (j - i) * blk <= qcol
            sts = [jnp.where(keep, st, -jnp.inf) for st in sts]
        m_new = tuple(jnp.maximum(ms[hh], jnp.max(sts[hh], axis=0, keepdims=True)) for hh in range(2))
        for hh in range(2):
            p_scr[slot, hh] = jnp.exp2(sts[hh] - m_new[hh]).astype(BF16)
            acc_scr[hh] = alphas_prev[hh] * acc_scr[hh] + pv[hh]
        return tuple(jnp.exp2(ms[hh] - m_new[hh]) for hh in range(2)), m_new

    def pair(t, carry, masked):
        alphas, ms = step(2 * t, 0, *carry, masked, True)
        return step(2 * t + 1, 1, alphas, ms, masked, not masked)

    p_scr[1] = jnp.zeros_like(p_scr[1])
    acc_scr[...] = jnp.zeros_like(acc_scr)
    put_scores(0, 0)
    init = (tuple(jnp.ones((1, blk), F32) for _ in range(2)), tuple(jnp.full((1, blk), -jnp.inf, F32) for _ in range(2)))
    carry = lax.fori_loop(0, i // 2, lambda t, c: pair(t, c, False), init)
    alphas, _ = pair(i // 2, carry, True)
    pv = weighted_values(2 * (i // 2) + 1, 1)
    outs = []
    for hh in range(2):
        acc = alphas[hh] * acc_scr[hh] + pv[hh]
        outs.append(acc[:MLA_V] / acc[MLA_V:MLA_V + 1])
    o_ref[...] = jnp.concatenate(outs, axis=0).T.astype(o_ref.dtype)


def mla_attention(q, k, vt, *, batch, seq):
    t = q.shape[0]
    nq = seq // ATTN_BLOCK
    return pl.pallas_call(
        _attn_kernel,
        grid=(batch, MLA_HEADS // 2, nq),
        in_specs=[pl.BlockSpec((ATTN_BLOCK, 2 * LANES), lambda b, p, i: (b * nq + i, p)),
                  pl.BlockSpec((seq, 2 * LANES), lambda b, p, i: (b, p)),
                  pl.BlockSpec((nq, 2 * MLA_V, ATTN_BLOCK), lambda b, p, i: (b, p, 0))],
        out_specs=pl.BlockSpec((ATTN_BLOCK, LANES), lambda b, p, i: (b * nq + i, p)),
        out_shape=jax.ShapeDtypeStruct((t, MLA_HEADS * MLA_V), BF16),
        scratch_shapes=[pltpu.VMEM((2, 2, ATTN_BLOCK, ATTN_BLOCK), F32),
                        pltpu.VMEM((2, 2, ATTN_BLOCK, ATTN_BLOCK), BF16),
                        pltpu.VMEM((2, MLA_V + SUM_ROWS, ATTN_BLOCK), F32)],
        compiler_params=_params("parallel", "parallel", "arbitrary"),
        name="mla_attention",
    )(q, k, vt)


def _sgu_kernel(pu_ref, pv_ref, vn_ref, w_ref, b_ref, o_ref):
    u = jax.nn.gelu(pu_ref[...].astype(F32), approximate=True)
    v = _rms(jax.nn.gelu(pv_ref[...].astype(F32), approximate=True), vn_ref[...])
    gw = SG_WIDTH // SG_GROUPS
    lane_group = lax.broadcasted_iota(jnp.int32, (SG_CHUNK, SG_WIDTH), 1) // gw
    wrow = lax.broadcasted_iota(jnp.int32, (SG_CHUNK, SG_GROUPS * SG_CHUNK), 0)
    wcol = lax.broadcasted_iota(jnp.int32, (SG_CHUNK, SG_GROUPS * SG_CHUNK), 1) % SG_CHUNK
    w = jnp.where(wcol <= wrow, w_ref[...], 0.0).astype(BF16)
    for c in range(u.shape[0] // SG_CHUNK):
        rows = slice(c * SG_CHUNK, (c + 1) * SG_CHUNK)
        vc = v[rows]
        stacked = jnp.concatenate(
            [jnp.where(lane_group == g, vc, 0.0).astype(BF16) for g in range(SG_GROUPS)], axis=0)
        mixed = jnp.dot(w, stacked, preferred_element_type=F32) + b_ref[...]
        o_ref[rows, :] = (u[rows] * mixed).astype(o_ref.dtype)


def sgu(proj, v_norm, w_cat, b_full, *, tm=512):
    t = proj.shape[0]
    fix = lambda i: (0, 0)
    return pl.pallas_call(
        _sgu_kernel,
        grid=(t // tm,),
        in_specs=[pl.BlockSpec((tm, SG_WIDTH), lambda i: (i, PROJ_SG // SG_WIDTH)),
                  pl.BlockSpec((tm, SG_WIDTH), lambda i: (i, PROJ_SG // SG_WIDTH + 1)),
                  pl.BlockSpec((1, SG_WIDTH), fix),
                  pl.BlockSpec(w_cat.shape, fix),
                  pl.BlockSpec(b_full.shape, fix)],
        out_specs=pl.BlockSpec((tm, SG_WIDTH), lambda i: (i, 0)),
        out_shape=jax.ShapeDtypeStruct((t, SG_WIDTH), BF16),
        compiler_params=_params("parallel"),
        name="sgu",
    )(proj, proj, v_norm, w_cat, b_full)


def _rw_prep_kernel(*refs, tiles_per_seq, has_vres):
    if has_vres:
        (p_ref, prev_ref, mu_ref, w0_ref, w2_ref, a0_ref, a2_ref, g2_ref, kk_ref, ka_ref, rk_ref, bd_ref,
         vf_ref, v0_ref, v1_ref, v2_ref,
         r_o, lw_o, k_o, v_o, a_o, b_o, g_o, bonus_o) = refs
    else:
        (p_ref, prev_ref, mu_ref, w0_ref, w2_ref, a0_ref, a2_ref, g2_ref, kk_ref, ka_ref, rk_ref, bd_ref,
         r_o, lw_o, k_o, v_o, a_o, b_o, g_o, bonus_o) = refs
    cur = p_ref[...].astype(F32)
    tm = cur.shape[0]
    first = (pl.program_id(0) % tiles_per_seq) == 0
    last_prev = jnp.where(first, 0.0, prev_ref[PREV_ROWS - 1:PREV_ROWS, :].astype(F32))
    rowi = lax.broadcasted_iota(jnp.int32, cur.shape, 0)
    prev = jnp.where(rowi == 0, last_prev, pltpu.roll(cur, 1, axis=0))
    x = cur + (prev - cur) * mu_ref[...]
    w3 = 3 * RW_WIDTH
    r = x[:, :RW_WIDTH]
    k = x[:, RW_WIDTH:2 * RW_WIDTH]
    v = x[:, 2 * RW_WIDTH:w3]
    xwa = x[:, w3:w3 + LANES]
    xg = x[:, w3 + LANES:]
    w = -jax.nn.softplus(-(w0_ref[...] + _dot_x3(jnp.tanh(xwa), w2_ref[...]))) - 0.5
    lw_o[...] = -jnp.exp(w)
    a = jax.nn.sigmoid(a0_ref[...] + _dot_x3(xwa, a2_ref[...]))
    g_o[...] = _dot(jax.nn.sigmoid(xg), g2_ref[...]).astype(g_o.dtype)
    if has_vres:
        mix = jax.nn.sigmoid(v0_ref[...] + _dot_x3(_dot_x3(v, v1_ref[...]), v2_ref[...]))
        v = v + (vf_ref[...].astype(F32) - v) * mix
    bd = bd_ref[...]
    kk = k * kk_ref[...]
    kk = kk * lax.rsqrt(jnp.maximum(_segsum(kk * kk, bd), 1e-24))
    k = k * (1.0 + (a - 1.0) * ka_ref[...])
    r_o[...] = r.astype(r_o.dtype)
    k_o[...] = k.astype(k_o.dtype)
    v_o[...] = v.astype(v_o.dtype)
    a_o[...] = (-kk).astype(a_o.dtype)
    b_o[...] = (kk * a).astype(b_o.dtype)
    bonus_o[...] = (_segsum(r * k * rk_ref[...], bd) * v).astype(bonus_o.dtype)


def rw_prep(proj, lp, v_first, *, seq, tm=256):
    t = proj.shape[0]
    has_vres = v_first is not None
    row = lambda i: (i, 0)
    fix = lambda i: (0, 0)
    vec = pl.BlockSpec((1, RW_WIDTH), fix)
    in_specs = [pl.BlockSpec((tm, RW_COLS), row),
                pl.BlockSpec((PREV_ROWS, RW_COLS), lambda i: (jnp.maximum(i * (tm // PREV_ROWS) - 1, 0), 0)),
                pl.BlockSpec((1, RW_COLS), fix),
                vec, pl.BlockSpec((LANES, RW_WIDTH), fix),
                vec, pl.BlockSpec((LANES, RW_WIDTH), fix),
                pl.BlockSpec((RW_GATE_RANK, RW_WIDTH), fix),
                vec, vec, vec, pl.BlockSpec((RW_WIDTH, RW_WIDTH), fix)]
    args = [proj, proj, lp["rw_mu"], lp["rw_w0"], lp["rw_w2"], lp["rw_a0"], lp["rw_a2"], lp["rw_g2"],
            lp["rw_k_k"], lp["rw_k_a"], lp["rw_r_k"], lp["rw_bd"]]
    if has_vres:
        in_specs += [pl.BlockSpec((tm, RW_WIDTH), row), vec,
                     pl.BlockSpec(lp["rw_v1"].shape, fix), pl.BlockSpec(lp["rw_v2"].shape, fix)]
        args += [v_first, lp["rw_v0"], lp["rw_v1"], lp["rw_v2"]]
    dtypes = [BF16, F32, BF16, BF16, BF16, BF16, BF16, BF16]
    return pl.pallas_call(
        functools.partial(_rw_prep_kernel, tiles_per_seq=seq // tm, has_vres=has_vres),
        grid=(t // tm,),
        in_specs=in_specs,
        out_specs=[pl.BlockSpec((tm, RW_WIDTH), row)] * 8,
        out_shape=[jax.ShapeDtypeStruct((t, RW_WIDTH), dt) for dt in dtypes],
        compiler_params=_params("parallel"),
        name="rw_prep",
    )(*args)


def _cumsum_rows(tri, x):
    h1 = x.astype(BF16)
    r1 = x - h1.astype(F32)
    h2 = r1.astype(BF16)
    h3 = (r1 - h2.astype(F32)).astype(BF16)
    f = lambda y: jnp.dot(tri, y, preferred_element_type=F32)
    return f(h1) + f(h2) + f(h3)


def _rows_bcast(x, c, which):
    n = x.shape[0] // c
    return jnp.concatenate([jnp.broadcast_to(x[j * c + which:j * c + which + 1], (c, x.shape[1]))
                            for j in range(n)], axis=0)


def _rw_scan_kernel(r_ref, lw_ref, k_ref, v_ref, a_ref, b_ref, y_ref, s_ref):
    c = SCAN_CHUNK
    n_tok = r_ref.shape[0]
    pairs = range(r_ref.shape[1] // LANES)
    heads = [(p, hh) for p in pairs for hh in range(2)]

    @pl.when(pl.program_id(2) == 0)
    def _():
        s_ref[...] = jnp.zeros_like(s_ref)

    ri = lax.broadcasted_iota(jnp.int32, (n_tok, n_tok), 0)
    ci = lax.broadcasted_iota(jnp.int32, (n_tok, n_tok), 1)
    same = (ri // c) == (ci // c)
    incl = jnp.logical_and(same, ci <= ri)
    strict = jnp.logical_and(same, ci < ri)
    eye = jnp.where(ri == ci, 1.0, 0.0)
    tri = jnp.where(incl, 1.0, 0.0).astype(BF16)
    head0 = lax.broadcasted_iota(jnp.int32, (n_tok, LANES), 1) < RW_HEAD
    in_head = (head0, jnp.logical_not(head0))
    sr = lax.broadcasted_iota(jnp.int32, (LANES, LANES), 0)
    sc = lax.broadcasted_iota(jnp.int32, (LANES, LANES), 1)
    same_head = (sr // RW_HEAD) == (sc // RW_HEAD)
    eye_s = jnp.where(sr == sc, 1.0, 0.0).astype(BF16)

    cols = lambda ref, p: ref[:, p * LANES:(p + 1) * LANES]
    lw = [cols(lw_ref, p) for p in pairs]
    cum = [_cumsum_rows(tri, x) for x in lw]
    cum_mid = [_rows_bcast(x, c, c // 2 - 1) for x in cum]
    cum_end = [_rows_bcast(x, c, c - 1) for x in cum]
    r = [cols(r_ref, p).astype(F32) for p in pairs]
    a = [cols(a_ref, p).astype(F32) for p in pairs]
    k = [cols(k_ref, p).astype(F32) for p in pairs]
    b = [cols(b_ref, p).astype(F32) for p in pairs]
    v = [cols(v_ref, p).astype(F32) for p in pairs]
    r_abs = [r[p] * jnp.exp(cum[p]) for p in pairs]
    a_abs = [a[p] * jnp.exp(cum[p] - lw[p]) for p in pairs]
    g_end = [jnp.exp(cum_end[p] - cum[p]) for p in pairs]
    bk_end = [[jnp.concatenate([(b[p] * g_end[p])[j * c:(j + 1) * c], (k[p] * g_end[p])[j * c:(j + 1) * c]],
                               axis=0).astype(BF16) for j in range(n_tok // c)] for p in pairs]
    g_bwd = [jnp.exp(cum_mid[p] - cum[p]) for p in pairs]
    r_mid = [r[p] * jnp.exp(cum[p] - cum_mid[p]) for p in pairs]
    a_mid = [a[p] * jnp.exp(cum[p] - lw[p] - cum_mid[p]) for p in pairs]
    rhs = [jnp.concatenate([b[p] * g_bwd[p], k[p] * g_bwd[p]], axis=0).astype(BF16) for p in pairs]
    pm = [_dot_nt(jnp.concatenate([jnp.where(in_head[hh], a_mid[p], 0.0), jnp.where(in_head[hh], r_mid[p], 0.0)],
                                  axis=0).astype(BF16), rhs[p]) for p, hh in heads]
    lab = [jnp.where(strict, x[:n_tok, :n_tok], 0.0) for x in pm]
    lak = [jnp.where(strict, x[:n_tok, n_tok:], 0.0).astype(BF16) for x in pm]
    mrb = [jnp.where(incl, x[n_tok:, :n_tok], 0.0).astype(BF16) for x in pm]
    mrk = [jnp.where(incl, x[n_tok:, n_tok:], 0.0).astype(BF16) for x in pm]
    vh = [jnp.where(in_head[hh], v[p], 0.0).astype(BF16) for p, hh in heads]
    inv = [eye + x for x in lab]
    lp = lab
    q = 2
    while q < c:
        lp = [_dot(x, x) for x in lp]
        inv = [t + _dot(t, x) for t, x in zip(inv, lp)]
        q *= 2
    xv = [jnp.dot(lak[n], vh[n], preferred_element_type=F32) for n in range(len(heads))]
    tw = [_dot(inv[n], jnp.concatenate([jnp.where(in_head[hh], a_abs[p], 0.0), xv[n]], axis=1))
          for n, (p, hh) in enumerate(heads)]
    y2h = [jnp.dot(mrk[n], vh[n], preferred_element_type=F32) for n in range(len(heads))]
    w1 = [tw[2 * p][:, :LANES] + tw[2 * p + 1][:, :LANES] for p in pairs]
    w2 = [tw[2 * p][:, LANES:] + tw[2 * p + 1][:, LANES:] for p in pairs]
    w12 = [jnp.concatenate([w1[p], w2[p]], axis=1).astype(BF16) for p in pairs]
    lhs_xy = [jnp.concatenate([jnp.concatenate([w1[p][j * c:(j + 1) * c], r_abs[p][j * c:(j + 1) * c]], axis=0)
                               for j in range(n_tok // c)], axis=0).astype(BF16) for p in pairs]
    vb = [x.astype(BF16) for x in v]
    zeros_c = jnp.zeros((c, LANES), BF16)
    u_rows = [[] for _ in pairs]
    y0_rows = [[] for _ in pairs]
    for j in range(n_tok // c):
        rows = slice(j * c, (j + 1) * c)
        s0 = [s_ref[p] for p in pairs]
        s0b = [x.astype(BF16) for x in s0]
        xy = [_dot_nt(lhs_xy[p][2 * j * c:2 * (j + 1) * c], s0b[p]) for p in pairs]
        uvt = [_dot_nt(jnp.concatenate([s0b[p], eye_s], axis=1),
                       jnp.concatenate([w12[p][rows], jnp.concatenate([zeros_c, vb[p][rows]], axis=1)], axis=0))
               for p in pairs]
        upd = [jnp.dot(uvt[p].astype(BF16), bk_end[p][j], preferred_element_type=F32) for p in pairs]
        for p in pairs:
            u_rows[p].append(xy[p][:c] + w2[p][rows])
            y0_rows[p].append(xy[p][c:])
            s_ref[p] = s0[p] * jnp.exp(cum_end[p][j * c:j * c + 1, :]) + jnp.where(same_head, upd[p], 0.0)
    ub = [jnp.concatenate(u_rows[p], axis=0).astype(BF16) for p in pairs]
    y1 = [jnp.dot(mrb[n], ub[p], preferred_element_type=F32) for n, (p, hh) in enumerate(heads)]
    for p in pairs:
        y_ref[:, p * LANES:(p + 1) * LANES] = (jnp.concatenate(y0_rows[p], axis=0) + y2h[2 * p] + y2h[2 * p + 1]
                                               + jnp.where(head0, y1[2 * p], y1[2 * p + 1]))


def rw_scan(r, lw, k, v, a, b, *, batch, seq):
    t = r.shape[0]
    ns = seq // SCAN_STEP
    width = SCAN_PAIRS * LANES
    spec = pl.BlockSpec((SCAN_STEP, width), lambda bi, p, i: (bi * ns + i, p))
    return pl.pallas_call(
        _rw_scan_kernel,
        grid=(batch, RW_WIDTH // width, ns),
        in_specs=[spec] * 6,
        out_specs=spec,
        out_shape=jax.ShapeDtypeStruct((t, RW_WIDTH), F32),
        scratch_shapes=[pltpu.VMEM((SCAN_PAIRS, LANES, LANES), F32)],
        compiler_params=_params("parallel", "parallel", "arbitrary"),
        name="rw_scan",
    )(r, lw, k, v, a, b)


def _rw_post_kernel(y_ref, bonus_ref, g_ref, lng_ref, lnb_ref, bd_ref, o_ref):
    y = y_ref[...]
    bd = bd_ref[...]
    mean = _segsum(y, bd) * (1.0 / RW_HEAD)
    d = y - mean
    var = _segsum(d * d, bd) * (1.0 / RW_HEAD)
    yn = d * lax.rsqrt(var + RW_GN_EPS) * lng_ref[...] + lnb_ref[...]
    o_ref[...] = ((yn + bonus_ref[...].astype(F32)) * g_ref[...].astype(F32)).astype(o_ref.dtype)


def rw_post(y, bonus, g, ln_g, ln_b, bd, *, tm=512):
    t = y.shape[0]
    row = pl.BlockSpec((tm, RW_WIDTH), lambda i: (i, 0))
    vec = pl.BlockSpec((1, RW_WIDTH), lambda i: (0, 0))
    return pl.pallas_call(
        _rw_post_kernel,
        grid=(t // tm,),
        in_specs=[row, row, row, vec, vec, pl.BlockSpec((RW_WIDTH, RW_WIDTH), lambda i: (0, 0))],
        out_specs=row,
        out_shape=jax.ShapeDtypeStruct((t, RW_WIDTH), BF16),
        compiler_params=_params("parallel"),
        name="rw_post",
    )(y, bonus, g, ln_g, ln_b, bd)


def _merge_kernel(h_ref, ya_ref, yb_ref, yc_ref, ga_ref, gb_ref, gc_ref, wa_ref, wb_ref, wc_ref, wo_ref, o_ref):
    gate = lambda g_ref: jax.nn.sigmoid(g_ref[...].astype(F32))
    merged = (gate(ga_ref) * jnp.dot(ya_ref[...], wa_ref[...], preferred_element_type=F32)
              + gate(gb_ref) * jnp.dot(yb_ref[...], wb_ref[...], preferred_element_type=F32)
              + gate(gc_ref) * jnp.dot(yc_ref[...], wc_ref[...], preferred_element_type=F32))
    o_ref[...] = h_ref[...] + jnp.dot(merged.astype(BF16), wo_ref[...], preferred_element_type=F32)


def merge(h, ya, yb, yc, proj, wa, wb, wc, wo, *, tm=512):
    t, d = h.shape
    row = lambda i: (i, 0)
    fix = lambda i: (0, 0)
    gate_specs = [pl.BlockSpec((tm, d), functools.partial(lambda i, n: (i, PROJ_GATE // d + n), n=n))
                  for n in range(3)]
    return pl.pallas_call(
        _merge_kernel,
        grid=(t // tm,),
        in_specs=[pl.BlockSpec((tm, d), row),
                  pl.BlockSpec((tm, ya.shape[1]), row),
                  pl.BlockSpec((tm, yb.shape[1]), row),
                  pl.BlockSpec((tm, yc.shape[1]), row),
                  *gate_specs,
                  pl.BlockSpec(wa.shape, fix), pl.BlockSpec(wb.shape, fix),
                  pl.BlockSpec(wc.shape, fix), pl.BlockSpec(wo.shape, fix)],
        out_specs=pl.BlockSpec((tm, d), row),
        out_shape=jax.ShapeDtypeStruct((t, d), F32),
        compiler_params=_params("parallel"),
        name="merge",
    )(h, ya, yb, yc, proj, proj, proj, wa, wb, wc, wo)


def _pad_cols(w, n):
    return jnp.pad(w, ((0, 0), (0, n - w.shape[1])))


def _layer_params(i, p):
    d = p["w_in"].shape[1]
    w_in = p["w_in"][i]
    o_kv = MLA_Q_RANK + MLA_KV_RANK
    o_sg = o_kv + MLA_ROPE
    o_rw = o_sg + 2 * SG_WIDTH
    o_gate = o_rw + RW_COLS
    zeros = lambda n: jnp.zeros((d, n), F32)
    w_mla = jnp.concatenate([w_in[:, :o_kv], zeros(MLA_NOPE), w_in[:, o_kv:o_sg],
                             zeros(LANES - MLA_QK)], axis=1)
    wuq = p["mla_w_uq"][i].reshape(MLA_Q_RANK, MLA_HEADS, MLA_QK)
    wuq = jnp.pad(wuq, ((0, 0), (0, 0), (0, LANES - MLA_QK))).reshape(MLA_Q_RANK, MLA_HEADS * LANES)
    wukv = p["mla_w_ukv"][i].reshape(MLA_KV_RANK, MLA_HEADS, MLA_NOPE + MLA_V)
    wuk = jnp.pad(wukv[:, :, :MLA_NOPE], ((0, 0), (0, 0), (0, LANES - MLA_NOPE)))
    wuk = wuk.reshape(MLA_KV_RANK, MLA_HEADS * LANES)
    wuv = wukv[:, :, MLA_NOPE:].reshape(MLA_KV_RANK, MLA_HEADS * MLA_V)
    head_of = jnp.arange(RW_WIDTH) // RW_HEAD
    w_proj = jnp.concatenate([w_in[:, o_rw:o_gate], zeros(PROJ_GATE - RW_COLS), w_in[:, o_gate:],
                              w_mla, w_in[:, o_sg:o_rw]], axis=1)
    lp = {
        "w_proj": w_proj.astype(BF16),
        "qa": p["mla_q_a_norm"][i].reshape(1, -1),
        "kva": p["mla_kv_a_norm"][i].reshape(1, -1),
        "wuq": wuq.astype(BF16), "wuk": wuk.astype(BF16), "wuvt": wuv.T.astype(BF16),
        "qn": _pad_cols(p["mla_q_norm"][i].reshape(1, -1), LANES),
        "kn": _pad_cols(p["mla_k_norm"][i].reshape(1, -1), LANES),
        "sg_vn": p["sg_v_norm"][i].reshape(1, -1),
        "sg_w": p["sg_w_s"][i].transpose(1, 0, 2).reshape(SG_CHUNK, SG_GROUPS * SG_CHUNK),
        "sg_b": jnp.repeat(p["sg_b_s"][i].T, SG_WIDTH // SG_GROUPS, axis=1),
        "rw_mu": p["rw_mu"][i].reshape(1, -1),
        "rw_w0": p["rw_w0"][i].reshape(1, -1),
        "rw_w2": jnp.pad(p["rw_w2"][i], ((0, RW_A_RANK), (0, 0))),
        "rw_a0": p["rw_a0"][i].reshape(1, -1),
        "rw_a2": jnp.pad(p["rw_a2"][i], ((RW_DECAY_RANK, 0), (0, 0))),
        "rw_g2": p["rw_g2"][i].astype(BF16),
        "rw_k_k": p["rw_k_k"][i].reshape(1, -1),
        "rw_k_a": p["rw_k_a"][i].reshape(1, -1),
        "rw_r_k": p["rw_r_k"][i].reshape(1, -1),
        "rw_ln_g": p["rw_ln_g"][i].reshape(1, -1),
        "rw_ln_b": p["rw_ln_b"][i].reshape(1, -1),
        "rw_bd": (head_of[:, None] == head_of[None, :]).astype(BF16),
        "w_out_mla": p["w_out_mla"][i].astype(BF16),
        "w_out_sg": p["w_out_sg"][i].astype(BF16),
        "w_out_rw": p["w_out_rw"][i].astype(BF16),
        "w_o": p["w_o"][i].astype(BF16),
    }
    if i > 0:
        lp["rw_v0"] = p["rw_v0"][i - 1].reshape(1, -1)
        lp["rw_v1"] = _pad_cols(p["rw_v1"][i - 1], LANES)
        lp["rw_v2"] = jnp.pad(p["rw_v2"][i - 1], ((0, LANES - p["rw_v2"].shape[1]), (0, 0)))
    return lp


def _rope_tables(positions):
    inv_freq = ROPE_BASE ** (-jnp.arange(0, MLA_ROPE, 2, dtype=F32) / MLA_ROPE)
    ang = positions.astype(F32).reshape(-1, 1) * inv_freq
    cos, sin = jnp.cos(ang), jnp.sin(ang)
    t = ang.shape[0]
    cos_full = jnp.concatenate([jnp.ones((t, MLA_NOPE), F32), cos, cos,
                                jnp.ones((t, LANES - MLA_QK), F32)], axis=1)
    sin_signed = jnp.concatenate([jnp.zeros((t, MLA_NOPE), F32), -sin, sin,
                                  jnp.zeros((t, LANES - MLA_QK), F32)], axis=1)
    return cos_full, sin_signed


def kernel(x, positions, ffn1_norm, ffn1_w_gate, ffn1_w_up, ffn1_w_down, mix_norm, w_in, mla_q_a_norm, mla_w_uq, mla_kv_a_norm, mla_w_ukv, mla_q_norm, mla_k_norm, sg_v_norm, sg_w_s, sg_b_s, rw_mu, rw_w0, rw_w2, rw_a0, rw_a2, rw_g2, rw_k_k, rw_k_a, rw_r_k, rw_ln_g, rw_ln_b, rw_v0, rw_v1, rw_v2, w_out_mla, w_out_sg, w_out_rw, w_o, ffn2_norm, ffn2_w_gate, ffn2_w_up, ffn2_w_down):
    params = dict(ffn1_norm=ffn1_norm, ffn1_w_gate=ffn1_w_gate, ffn1_w_up=ffn1_w_up, ffn1_w_down=ffn1_w_down,
                  mix_norm=mix_norm, w_in=w_in, mla_q_a_norm=mla_q_a_norm, mla_w_uq=mla_w_uq,
                  mla_kv_a_norm=mla_kv_a_norm, mla_w_ukv=mla_w_ukv, mla_q_norm=mla_q_norm, mla_k_norm=mla_k_norm,
                  sg_v_norm=sg_v_norm, sg_w_s=sg_w_s, sg_b_s=sg_b_s, rw_mu=rw_mu, rw_w0=rw_w0, rw_w2=rw_w2,
                  rw_a0=rw_a0, rw_a2=rw_a2, rw_g2=rw_g2, rw_k_k=rw_k_k, rw_k_a=rw_k_a, rw_r_k=rw_r_k,
                  rw_ln_g=rw_ln_g, rw_ln_b=rw_ln_b, rw_v0=rw_v0, rw_v1=rw_v1, rw_v2=rw_v2,
                  w_out_mla=w_out_mla, w_out_sg=w_out_sg, w_out_rw=w_out_rw, w_o=w_o,
                  ffn2_norm=ffn2_norm, ffn2_w_gate=ffn2_w_gate, ffn2_w_up=ffn2_w_up, ffn2_w_down=ffn2_w_down)
    batch, seq, d = x.shape
    depth = w_in.shape[0]
    cos_full, sin_signed = _rope_tables(positions)
    h = x.reshape(batch * seq, d)
    v_first = None
    for i in range(depth):
        lp = _layer_params(i, params)
        h = ffn_half_step(h, ffn1_norm, ffn1_w_gate, ffn1_w_up, ffn1_w_down, i)
        proj = rms_matmul(h, mix_norm[i], lp["w_proj"], tm=1024, tn=PROJ_TILE, out_dtype=BF16)
        q, k, vt = mla_prep(proj, cos_full, sin_signed, lp["qa"], lp["kva"], lp["wuq"], lp["wuk"], lp["wuvt"],
                           lp["qn"], lp["kn"])
        y_a = mla_attention(q, k, vt, batch=batch, seq=seq)
        y_b = sgu(proj, lp["sg_vn"], lp["sg_w"], lp["sg_b"])
        r, lw, kr, vr, a, b, g, bonus = rw_prep(proj, lp, v_first, seq=seq)
        if v_first is None:
            v_first = vr
        y_scan = rw_scan(r, lw, kr, vr, a, b, batch=batch, seq=seq)
        y_c = rw_post(y_scan, bonus, g, lp["rw_ln_g"], lp["rw_ln_b"], lp["rw_bd"])
        h = merge(h, y_a, y_b, y_c, proj, lp["w_out_mla"], lp["w_out_sg"], lp["w_out_rw"], lp["w_o"])
        h = ffn_half_step(h, ffn2_norm, ffn2_w_gate, ffn2_w_up, ffn2_w_down, i)
    return h.reshape(batch, seq, d)
```

```python
import functools

import jax
import jax.numpy as jnp
from jax import lax
from jax.experimental import pallas as pl
from jax.experimental.pallas import tpu as pltpu

F32 = jnp.float32
BF16 = jnp.bfloat16

LANES = 128
EPS = 1e-6
RW_GN_EPS = 64e-5
ROPE_BASE = 10000.0

MLA_HEADS = 8
MLA_NOPE = 64
MLA_ROPE = 32
MLA_QK = MLA_NOPE + MLA_ROPE
MLA_V = 64
MLA_Q_RANK = 256
MLA_KV_RANK = 128
SG_WIDTH = 512
SG_GROUPS = 8
SG_CHUNK = 128
RW_HEADS = 8
RW_HEAD = 64
RW_WIDTH = RW_HEADS * RW_HEAD
RW_DECAY_RANK = 64
RW_A_RANK = 64
RW_GATE_RANK = 128
RW_COLS = 3 * RW_WIDTH + RW_DECAY_RANK + RW_A_RANK + RW_GATE_RANK

VMEM_LIMIT = 56 * 1024 * 1024
SCAN_CHUNK = 128
SCAN_STEP = 256
SCAN_PAIRS = 4
ATTN_BLOCK = 256
ATTN_HEADS = 8
PROJ_TILE = 3328
PREV_ROWS = 16
PROJ_GATE = 2048
PROJ_MLA = PROJ_GATE + 3 * 1024
PROJ_SG = PROJ_MLA + 512
PROJ_COLS = PROJ_SG + 2 * SG_WIDTH
SUM_ROWS = 16
LOG2E = 1.4426950408889634


def _params(*sem):
    return pltpu.CompilerParams(dimension_semantics=sem, vmem_limit_bytes=VMEM_LIMIT)


def _dot(a, b):
    return jnp.dot(a.astype(BF16), b.astype(BF16), preferred_element_type=F32)


def _dot_x3(a, b):
    ah = a.astype(BF16)
    al = (a - ah.astype(F32)).astype(BF16)
    bh = b.astype(BF16)
    bl = (b - bh.astype(F32)).astype(BF16)
    f = lambda x, y: jnp.dot(x, y, preferred_element_type=F32)
    return f(ah, bh) + f(al, bh) + f(ah, bl)


def _dot_nt(a, b):
    return lax.dot_general(a, b, (((1,), (1,)), ((), ())), preferred_element_type=F32)


def _rms(x, g, n=None):
    n = x.shape[-1] if n is None else n
    ms = jnp.sum(x * x, axis=-1, keepdims=True) * (1.0 / n)
    return x * lax.rsqrt(ms + EPS) * g


def _segsum(x, bd):
    hi = x.astype(BF16)
    lo = (x - hi.astype(F32)).astype(BF16)
    return (jnp.dot(hi, bd, preferred_element_type=F32) + jnp.dot(lo, bd, preferred_element_type=F32))


def _rms_matmul_kernel(h_ref, g_ref, w_ref, o_ref, z_ref):
    @pl.when(pl.program_id(1) == 0)
    def _():
        z_ref[...] = _rms(h_ref[...], g_ref[...]).astype(BF16)

    o_ref[...] = jnp.dot(z_ref[...], w_ref[...], preferred_element_type=F32).astype(o_ref.dtype)


def rms_matmul(h, g, w, *, tm, tn, out_dtype=F32):
    t, d = h.shape
    n = w.shape[1]
    return pl.pallas_call(
        _rms_matmul_kernel,
        grid=(t // tm, n // tn),
        in_specs=[pl.BlockSpec((tm, d), lambda i, j: (i, 0)),
                  pl.BlockSpec((1, d), lambda i, j: (0, 0)),
                  pl.BlockSpec((d, tn), lambda i, j: (0, j))],
        out_specs=pl.BlockSpec((tm, tn), lambda i, j: (i, j)),
        out_shape=jax.ShapeDtypeStruct((t, n), out_dtype),
        scratch_shapes=[pltpu.VMEM((tm, d), BF16)],
        compiler_params=_params("parallel", "arbitrary"),
        name="rms_matmul",
    )(h, g.reshape(1, d), w)


def _ffn_kernel(h_ref, g_ref, wg_ref, wu_ref, wd_ref, o_ref, z_ref, acc_ref):
    j = pl.program_id(1)

    @pl.when(j == 0)
    def _():
        z_ref[...] = _rms(h_ref[...], g_ref[...]).astype(BF16)
        acc_ref[...] = jnp.zeros_like(acc_ref)

    z = z_ref[...]
    a = jnp.dot(z, wg_ref[...].astype(BF16), preferred_element_type=F32)
    b = jnp.dot(z, wu_ref[...].astype(BF16), preferred_element_type=F32)
    t = (a * jax.nn.sigmoid(a) * b).astype(BF16)
    acc_ref[...] += jnp.dot(t, wd_ref[...].astype(BF16), preferred_element_type=F32)

    @pl.when(j == pl.num_programs(1) - 1)
    def _():
        o_ref[...] = h_ref[...] + 0.5 * acc_ref[...]


def ffn_half_step(h, g, wg, wu, wd, layer, *, tm=2048, tf=256):
    t, d = h.shape
    f = wg.shape[2]
    return pl.pallas_call(
        _ffn_kernel,
        grid=(t // tm, f // tf),
        in_specs=[pl.BlockSpec((tm, d), lambda i, j: (i, 0)),
                  pl.BlockSpec((None, 1, d), lambda i, j: (layer, 0, 0)),
                  pl.BlockSpec((None, d, tf), lambda i, j: (layer, 0, j)),
                  pl.BlockSpec((None, d, tf), lambda i, j: (layer, 0, j)),
                  pl.BlockSpec((None, tf, d), lambda i, j: (layer, j, 0))],
        out_specs=pl.BlockSpec((tm, d), lambda i, j: (i, 0)),
        out_shape=jax.ShapeDtypeStruct((t, d), F32),
        scratch_shapes=[pltpu.VMEM((tm, d), BF16), pltpu.VMEM((tm, d), F32)],
        compiler_params=_params("parallel", "arbitrary"),
        name="ffn_half_step",
    )(h, g.reshape(g.shape[0], 1, d), wg, wu, wd)


def _rope_partner(x):
    half = MLA_ROPE // 2
    zeros = jnp.zeros_like(x[..., :MLA_NOPE])
    return jnp.concatenate([zeros, x[..., MLA_NOPE + half:MLA_QK], x[..., MLA_NOPE:MLA_NOPE + half],
                            jnp.zeros_like(x[..., MLA_QK:])], axis=-1)


def _mla_prep_kernel(c_ref, cos_ref, sin_ref, qa_ref, kva_ref, wuq_ref, wuk_ref, wuvt_ref, gains_ref,
                     q_ref, k_ref, vt_ref):
    c = c_ref[...].astype(F32)
    hw = MLA_HEADS * LANES
    zq = _rms(c[:, :MLA_Q_RANK], qa_ref[...]).astype(BF16)
    zkv = _rms(c[:, MLA_Q_RANK:MLA_Q_RANK + MLA_KV_RANK], kva_ref[...]).astype(BF16)
    kpe = c[:, MLA_Q_RANK + MLA_KV_RANK:]
    q = jnp.dot(zq, wuq_ref[...], preferred_element_type=F32)
    kn = jnp.dot(zkv, wuk_ref[...], preferred_element_type=F32)
    for j in range(vt_ref.shape[0]):
        rows = slice(j * ATTN_BLOCK, (j + 1) * ATTN_BLOCK)
        vt_ref[j] = _dot_nt(wuvt_ref[...], zkv[rows]).astype(BF16)
    cos_full = cos_ref[...]
    sin_signed = sin_ref[...]
    q_cos = gains_ref[0:1, :] * cos_full
    q_sin = gains_ref[1:2, :] * sin_signed
    k_cos = gains_ref[2:3, :] * cos_full
    lane = lax.broadcasted_iota(jnp.int32, kpe.shape, 1)
    kpe_partner = jnp.where(lane < MLA_NOPE + MLA_ROPE // 2,
                            pltpu.roll(kpe, LANES - MLA_ROPE // 2, axis=1),
                            pltpu.roll(kpe, MLA_ROPE // 2, axis=1))
    k_rot = kpe_partner * (gains_ref[3:4, :] * sin_signed)
    scale = MLA_QK ** -0.5 * LOG2E
    inv_n = 1.0 / MLA_QK
    for h in range(MLA_HEADS):
        sl = slice(h * LANES, (h + 1) * LANES)
        x = q[:, sl]
        r = lax.rsqrt(jnp.sum(x * x, axis=-1, keepdims=True) * inv_n + EPS) * scale
        q_ref[:, sl] = ((x * q_cos + q[:, hw + h * LANES:hw + (h + 1) * LANES] * q_sin) * r).astype(BF16)
        y = kn[:, sl] + kpe
        r = lax.rsqrt(jnp.sum(y * y, axis=-1, keepdims=True) * inv_n + EPS)
        k_ref[:, sl] = ((y * k_cos + k_rot) * r).astype(BF16)


def mla_prep(c, cos_full, sin_signed, qa, kva, wuq, wuk, wuvt, gains, *, tm=512):
    t = c.shape[0]
    hw = MLA_HEADS * LANES
    vw = MLA_HEADS * MLA_V
    row = lambda i: (i, 0)
    fix = lambda i: (0, 0)
    return pl.pallas_call(
        _mla_prep_kernel,
        grid=(t // tm,),
        in_specs=[pl.BlockSpec((tm, PROJ_SG - PROJ_MLA), lambda i: (i, PROJ_MLA // (PROJ_SG - PROJ_MLA))),
                  pl.BlockSpec((tm, LANES), row),
                  pl.BlockSpec((tm, LANES), row),
                  pl.BlockSpec((1, MLA_Q_RANK), fix),
                  pl.BlockSpec((1, MLA_KV_RANK), fix),
                  pl.BlockSpec(wuq.shape, fix),
                  pl.BlockSpec(wuk.shape, fix),
                  pl.BlockSpec(wuvt.shape, fix),
                  pl.BlockSpec(gains.shape, fix)],
        out_specs=[pl.BlockSpec((tm, hw), row),
                   pl.BlockSpec((tm, hw), row),
                   pl.BlockSpec((tm // ATTN_BLOCK, vw, ATTN_BLOCK), lambda i: (i, 0, 0))],
        out_shape=[jax.ShapeDtypeStruct((t, hw), BF16),
                   jax.ShapeDtypeStruct((t, hw), BF16),
                   jax.ShapeDtypeStruct((t // ATTN_BLOCK, vw, ATTN_BLOCK), BF16)],
        compiler_params=_params("parallel"),
        name="mla_prep",
    )(c, cos_full, sin_signed, qa, kva, wuq, wuk, wuvt, gains)


def _attn_kernel(q_ref, k_ref, vt_ref, o_ref, st_scr, p_scr, acc_scr):
    i = pl.program_id(2)
    blk = ATTN_BLOCK
    heads = range(q_ref.shape[1] // LANES)
    q = q_ref[...]
    krow = lax.broadcasted_iota(jnp.int32, (blk, blk), 0)
    qcol = lax.broadcasted_iota(jnp.int32, (blk, blk), 1)
    ones_rows = jnp.ones((SUM_ROWS, blk), BF16)

    def put_scores(j, slot):
        ks = k_ref[pl.ds(pl.multiple_of(j * blk, blk), blk), :]
        for hh in heads:
            st_scr[slot, hh] = _dot_nt(ks[:, hh * LANES:(hh + 1) * LANES], q[:, hh * LANES:(hh + 1) * LANES])

    def weighted_values(j, slot):
        vts = vt_ref[j]
        return [jnp.dot(jnp.concatenate([vts[hh * MLA_V:(hh + 1) * MLA_V], ones_rows], axis=0), p_scr[slot, hh],
                        preferred_element_type=F32) for hh in heads]

    def step(j, slot, alphas_prev, ms, masked, prefetch):
        if prefetch:
            put_scores(j + 1, 1 - slot)
        pv = weighted_values(jnp.maximum(j - 1, 0), 1 - slot)
        sts = [st_scr[slot, hh] for hh in heads]
        if masked:
            keep = krow + (j - i) * blk <= qcol
            sts = [jnp.where(keep, st, -jnp.inf) for st in sts]
        m_new = tuple(jnp.maximum(ms[hh], jnp.max(sts[hh], axis=0, keepdims=True)) for hh in heads)
        for hh in heads:
            p_scr[slot, hh] = jnp.exp2(sts[hh] - m_new[hh]).astype(BF16)
            acc_scr[hh] = alphas_prev[hh] * acc_scr[hh] + pv[hh]
        return tuple(jnp.exp2(ms[hh] - m_new[hh]) for hh in heads), m_new

    def pair(t, carry, masked):
        alphas, ms = step(2 * t, 0, *carry, masked, True)
        return step(2 * t + 1, 1, alphas, ms, masked, not masked)

    p_scr[1] = jnp.zeros_like(p_scr[1])
    acc_scr[...] = jnp.zeros_like(acc_scr)
    put_scores(0, 0)
    init = (tuple(jnp.ones((1, blk), F32) for _ in heads), tuple(jnp.full((1, blk), -jnp.inf, F32) for _ in heads))
    carry = lax.fori_loop(0, i // 2, lambda t, c: pair(t, c, False), init)
    alphas, _ = pair(i // 2, carry, True)
    pv = weighted_values(2 * (i // 2) + 1, 1)
    outs = []
    for hh in heads:
        acc = alphas[hh] * acc_scr[hh] + pv[hh]
        outs.append(acc[:MLA_V] / acc[MLA_V:MLA_V + 1])
    o_ref[...] = jnp.concatenate(outs, axis=0).T.astype(o_ref.dtype)


def mla_attention(q, k, vt, *, batch, seq):
    t = q.shape[0]
    nq = seq // ATTN_BLOCK
    return pl.pallas_call(
        _attn_kernel,
        grid=(batch, MLA_HEADS // ATTN_HEADS, nq),
        in_specs=[pl.BlockSpec((ATTN_BLOCK, ATTN_HEADS * LANES), lambda b, p, i: (b * nq + i, p)),
                  pl.BlockSpec((seq, ATTN_HEADS * LANES), lambda b, p, i: (b, p)),
                  pl.BlockSpec((nq, ATTN_HEADS * MLA_V, ATTN_BLOCK), lambda b, p, i: (b, p, 0))],
        out_specs=pl.BlockSpec((ATTN_BLOCK, ATTN_HEADS * MLA_V), lambda b, p, i: (b * nq + i, p)),
        out_shape=jax.ShapeDtypeStruct((t, MLA_HEADS * MLA_V), BF16),
        scratch_shapes=[pltpu.VMEM((2, ATTN_HEADS, ATTN_BLOCK, ATTN_BLOCK), F32),
                        pltpu.VMEM((2, ATTN_HEADS, ATTN_BLOCK, ATTN_BLOCK), BF16),
                        pltpu.VMEM((ATTN_HEADS, MLA_V + SUM_ROWS, ATTN_BLOCK), F32)],
        compiler_params=_params("parallel", "parallel", "arbitrary"),
        name="mla_attention",
    )(q, k, vt)


def _sgu_kernel(pu_ref, pv_ref, vn_ref, w_ref, b_ref, o_ref):
    u = jax.nn.gelu(pu_ref[...].astype(F32), approximate=True)
    v = _rms(jax.nn.gelu(pv_ref[...].astype(F32), approximate=True), vn_ref[...])
    gw = SG_WIDTH // SG_GROUPS
    lane_group = lax.broadcasted_iota(jnp.int32, (SG_CHUNK, SG_WIDTH), 1) // gw
    wrow = lax.broadcasted_iota(jnp.int32, (SG_CHUNK, SG_GROUPS * SG_CHUNK), 0)
    wcol = lax.broadcasted_iota(jnp.int32, (SG_CHUNK, SG_GROUPS * SG_CHUNK), 1) % SG_CHUNK
    w = jnp.where(wcol <= wrow, w_ref[...], 0.0).astype(BF16)
    for c in range(u.shape[0] // SG_CHUNK):
        rows = slice(c * SG_CHUNK, (c + 1) * SG_CHUNK)
        vc = v[rows]
        stacked = jnp.concatenate(
            [jnp.where(lane_group == g, vc, 0.0).astype(BF16) for g in range(SG_GROUPS)], axis=0)
        mixed = jnp.dot(w, stacked, preferred_element_type=F32) + b_ref[...]
        o_ref[rows, :] = (u[rows] * mixed).astype(o_ref.dtype)


def sgu(proj, v_norm, w_cat, b_full, *, tm=512):
    t = proj.shape[0]
    fix = lambda i: (0, 0)
    return pl.pallas_call(
        _sgu_kernel,
        grid=(t // tm,),
        in_specs=[pl.BlockSpec((tm, SG_WIDTH), lambda i: (i, PROJ_SG // SG_WIDTH)),
                  pl.BlockSpec((tm, SG_WIDTH), lambda i: (i, PROJ_SG // SG_WIDTH + 1)),
                  pl.BlockSpec((1, SG_WIDTH), fix),
                  pl.BlockSpec(w_cat.shape, fix),
                  pl.BlockSpec(b_full.shape, fix)],
        out_specs=pl.BlockSpec((tm, SG_WIDTH), lambda i: (i, 0)),
        out_shape=jax.ShapeDtypeStruct((t, SG_WIDTH), BF16),
        compiler_params=_params("parallel"),
        name="sgu",
    )(proj, proj, v_norm, w_cat, b_full)


def _rw_prep_kernel(*refs, tiles_per_seq, has_vres):
    if has_vres:
        (p_ref, prev_ref, mu_ref, w0_ref, w2_ref, a0_ref, a2_ref, g2_ref, kk_ref, ka_ref, rk_ref, bd_ref,
         vf_ref, v0_ref, v1_ref, v2_ref,
         r_o, lw_o, k_o, v_o, a_o, b_o, g_o, bonus_o) = refs
    else:
        (p_ref, prev_ref, mu_ref, w0_ref, w2_ref, a0_ref, a2_ref, g2_ref, kk_ref, ka_ref, rk_ref, bd_ref,
         r_o, lw_o, k_o, v_o, a_o, b_o, g_o, bonus_o) = refs
    cur = p_ref[...].astype(F32)
    tm = cur.shape[0]
    first = (pl.program_id(0) % tiles_per_seq) == 0
    last_prev = jnp.where(first, 0.0, prev_ref[PREV_ROWS - 1:PREV_ROWS, :].astype(F32))
    rowi = lax.broadcasted_iota(jnp.int32, cur.shape, 0)
    prev = jnp.where(rowi == 0, last_prev, pltpu.roll(cur, 1, axis=0))
    x = cur + (prev - cur) * mu_ref[...]
    w3 = 3 * RW_WIDTH
    r = x[:, :RW_WIDTH]
    k = x[:, RW_WIDTH:2 * RW_WIDTH]
    v = x[:, 2 * RW_WIDTH:w3]
    xwa = x[:, w3:w3 + LANES]
    xg = x[:, w3 + LANES:]
    w = -jax.nn.softplus(-(w0_ref[...] + _dot_x3(jnp.tanh(xwa), w2_ref[...]))) - 0.5
    lw_o[...] = -jnp.exp(w)
    a = jax.nn.sigmoid(a0_ref[...] + _dot_x3(xwa, a2_ref[...]))
    g_o[...] = _dot(jax.nn.sigmoid(xg), g2_ref[...]).astype(g_o.dtype)
    if has_vres:
        mix = jax.nn.sigmoid(v0_ref[...] + _dot_x3(_dot_x3(v, v1_ref[...]), v2_ref[...]))
        v = v + (vf_ref[...].astype(F32) - v) * mix
    bd = bd_ref[...]
    kk = k * kk_ref[...]
    kk = kk * lax.rsqrt(jnp.maximum(_segsum(kk * kk, bd), 1e-24))
    k = k * (1.0 + (a - 1.0) * ka_ref[...])
    r_o[...] = r.astype(r_o.dtype)
    k_o[...] = k.astype(k_o.dtype)
    v_o[...] = v.astype(v_o.dtype)
    a_o[...] = (-kk).astype(a_o.dtype)
    b_o[...] = (kk * a).astype(b_o.dtype)
    bonus_o[...] = (_segsum(r * k * rk_ref[...], bd) * v).astype(bonus_o.dtype)


def rw_prep(proj, lp, v_first, *, seq, tm=256):
    t = proj.shape[0]
    has_vres = v_first is not None
    row = lambda i: (i, 0)
    fix = lambda i: (0, 0)
    vec = pl.BlockSpec((1, RW_WIDTH), fix)
    in_specs = [pl.BlockSpec((tm, RW_COLS), row),
                pl.BlockSpec((PREV_ROWS, RW_COLS), lambda i: (jnp.maximum(i * (tm // PREV_ROWS) - 1, 0), 0)),
                pl.BlockSpec((1, RW_COLS), fix),
                vec, pl.BlockSpec((LANES, RW_WIDTH), fix),
                vec, pl.BlockSpec((LANES, RW_WIDTH), fix),
                pl.BlockSpec((RW_GATE_RANK, RW_WIDTH), fix),
                vec, vec, vec, pl.BlockSpec((RW_WIDTH, RW_WIDTH), fix)]
    args = [proj, proj, lp["rw_mu"], lp["rw_w0"], lp["rw_w2"], lp["rw_a0"], lp["rw_a2"], lp["rw_g2"],
            lp["rw_k_k"], lp["rw_k_a"], lp["rw_r_k"], lp["rw_bd"]]
    if has_vres:
        in_specs += [pl.BlockSpec((tm, RW_WIDTH), row), vec,
                     pl.BlockSpec(lp["rw_v1"].shape, fix), pl.BlockSpec(lp["rw_v2"].shape, fix)]
        args += [v_first, lp["rw_v0"], lp["rw_v1"], lp["rw_v2"]]
    dtypes = [BF16, F32, BF16, BF16, BF16, BF16, BF16, BF16]
    return pl.pallas_call(
        functools.partial(_rw_prep_kernel, tiles_per_seq=seq // tm, has_vres=has_vres),
        grid=(t // tm,),
        in_specs=in_specs,
        out_specs=[pl.BlockSpec((tm, RW_WIDTH), row)] * 8,
        out_shape=[jax.ShapeDtypeStruct((t, RW_WIDTH), dt) for dt in dtypes],
        compiler_params=_params("parallel"),
        name="rw_prep",
    )(*args)


def _cumsum_rows(tri, x):
    h1 = x.astype(BF16)
    r1 = x - h1.astype(F32)
    h2 = r1.astype(BF16)
    h3 = (r1 - h2.astype(F32)).astype(BF16)
    f = lambda y: jnp.dot(tri, y, preferred_element_type=F32)
    return f(h1) + f(h2) + f(h3)


def _rows_bcast(x, c, which):
    n = x.shape[0] // c
    return jnp.concatenate([jnp.broadcast_to(x[j * c + which:j * c + which + 1], (c, x.shape[1]))
                            for j in range(n)], axis=0)


def _rw_scan_kernel(r_ref, lw_ref, k_ref, v_ref, a_ref, b_ref, y_ref, s_ref):
    c = SCAN_CHUNK
    n_tok = r_ref.shape[0]
    pairs = range(r_ref.shape[1] // LANES)
    heads = [(p, hh) for p in pairs for hh in range(2)]

    @pl.when(pl.program_id(2) == 0)
    def _():
        s_ref[...] = jnp.zeros_like(s_ref)

    ri = lax.broadcasted_iota(jnp.int32, (n_tok, n_tok), 0)
    ci = lax.broadcasted_iota(jnp.int32, (n_tok, n_tok), 1)
    same = (ri // c) == (ci // c)
    incl = jnp.logical_and(same, ci <= ri)
    strict = jnp.logical_and(same, ci < ri)
    eye = jnp.where(ri == ci, 1.0, 0.0)
    tri = jnp.where(incl, 1.0, 0.0).astype(BF16)
    head0 = lax.broadcasted_iota(jnp.int32, (n_tok, LANES), 1) < RW_HEAD
    in_head = (head0, jnp.logical_not(head0))
    sr = lax.broadcasted_iota(jnp.int32, (LANES, LANES), 0)
    sc = lax.broadcasted_iota(jnp.int32, (LANES, LANES), 1)
    same_head = (sr // RW_HEAD) == (sc // RW_HEAD)
    eye_s = jnp.where(sr == sc, 1.0, 0.0).astype(BF16)

    cols = lambda ref, p: ref[:, p * LANES:(p + 1) * LANES]
    lw = [cols(lw_ref, p) for p in pairs]
    cum = [_cumsum_rows(tri, x) for x in lw]
    cum_mid = [_rows_bcast(x, c, c // 2 - 1) for x in cum]
    cum_end = [_rows_bcast(x, c, c - 1) for x in cum]
    r = [cols(r_ref, p).astype(F32) for p in pairs]
    a = [cols(a_ref, p).astype(F32) for p in pairs]
    k = [cols(k_ref, p).astype(F32) for p in pairs]
    b = [cols(b_ref, p).astype(F32) for p in pairs]
    v = [cols(v_ref, p).astype(F32) for p in pairs]
    r_abs = [r[p] * jnp.exp(cum[p]) for p in pairs]
    a_abs = [a[p] * jnp.exp(cum[p] - lw[p]) for p in pairs]
    g_end = [jnp.exp(cum_end[p] - cum[p]) for p in pairs]
    bk_end = [[jnp.concatenate([(b[p] * g_end[p])[j * c:(j + 1) * c], (k[p] * g_end[p])[j * c:(j + 1) * c]],
                               axis=0).astype(BF16) for j in range(n_tok // c)] for p in pairs]
    g_bwd = [jnp.exp(cum_mid[p] - cum[p]) for p in pairs]
    r_mid = [r[p] * jnp.exp(cum[p] - cum_mid[p]) for p in pairs]
    a_mid = [a[p] * jnp.exp(cum[p] - lw[p] - cum_mid[p]) for p in pairs]
    rhs = [jnp.concatenate([b[p] * g_bwd[p], k[p] * g_bwd[p]], axis=0).astype(BF16) for p in pairs]
    pm = [_dot_nt(jnp.concatenate([jnp.where(in_head[hh], a_mid[p], 0.0), jnp.where(in_head[hh], r_mid[p], 0.0)],
                                  axis=0).astype(BF16), rhs[p]) for p, hh in heads]
    lab = [jnp.where(strict, x[:n_tok, :n_tok], 0.0) for x in pm]
    lak = [jnp.where(strict, x[:n_tok, n_tok:], 0.0).astype(BF16) for x in pm]
    mrb = [jnp.where(incl, x[n_tok:, :n_tok], 0.0).astype(BF16) for x in pm]
    mrk = [jnp.where(incl, x[n_tok:, n_tok:], 0.0).astype(BF16) for x in pm]
    vh = [jnp.where(in_head[hh], v[p], 0.0).astype(BF16) for p, hh in heads]
    inv = [eye + x for x in lab]
    lp = lab
    q = 2
    while q < c:
        lp = [_dot(x, x) for x in lp]
        inv = [t + _dot(t, x) for t, x in zip(inv, lp)]
        q *= 2
    xv = [jnp.dot(lak[n], vh[n], preferred_element_type=F32) for n in range(len(heads))]
    tw = [_dot(inv[n], jnp.concatenate([jnp.where(in_head[hh], a_abs[p], 0.0), xv[n]], axis=1))
          for n, (p, hh) in enumerate(heads)]
    y2h = [jnp.dot(mrk[n], vh[n], preferred_element_type=F32) for n in range(len(heads))]
    w1 = [tw[2 * p][:, :LANES] + tw[2 * p + 1][:, :LANES] for p in pairs]
    w2 = [tw[2 * p][:, LANES:] + tw[2 * p + 1][:, LANES:] for p in pairs]
    w12 = [jnp.concatenate([w1[p], w2[p]], axis=1).astype(BF16) for p in pairs]
    lhs_xy = [jnp.concatenate([jnp.concatenate([w1[p][j * c:(j + 1) * c], r_abs[p][j * c:(j + 1) * c]], axis=0)
                               for j in range(n_tok // c)], axis=0).astype(BF16) for p in pairs]
    vb = [x.astype(BF16) for x in v]
    zeros_c = jnp.zeros((c, LANES), BF16)
    u_rows = [[] for _ in pairs]
    y0_rows = [[] for _ in pairs]
    for j in range(n_tok // c):
        rows = slice(j * c, (j + 1) * c)
        s0 = [s_ref[p] for p in pairs]
        s0b = [x.astype(BF16) for x in s0]
        xy = [_dot_nt(lhs_xy[p][2 * j * c:2 * (j + 1) * c], s0b[p]) for p in pairs]
        uvt = [_dot_nt(jnp.concatenate([s0b[p], eye_s], axis=1),
                       jnp.concatenate([w12[p][rows], jnp.concatenate([zeros_c, vb[p][rows]], axis=1)], axis=0))
               for p in pairs]
        upd = [jnp.dot(uvt[p].astype(BF16), bk_end[p][j], preferred_element_type=F32) for p in pairs]
        for p in pairs:
            u_rows[p].append(xy[p][:c] + w2[p][rows])
            y0_rows[p].append(xy[p][c:])
            s_ref[p] = s0[p] * jnp.exp(cum_end[p][j * c:j * c + 1, :]) + jnp.where(same_head, upd[p], 0.0)
    ub = [jnp.concatenate(u_rows[p], axis=0).astype(BF16) for p in pairs]
    y1 = [jnp.dot(mrb[n], ub[p], preferred_element_type=F32) for n, (p, hh) in enumerate(heads)]
    for p in pairs:
        y_ref[:, p * LANES:(p + 1) * LANES] = (jnp.concatenate(y0_rows[p], axis=0) + y2h[2 * p] + y2h[2 * p + 1]
                                               + jnp.where(head0, y1[2 * p], y1[2 * p + 1]))


def rw_scan(r, lw, k, v, a, b, *, batch, seq):
    t = r.shape[0]
    ns = seq // SCAN_STEP
    width = SCAN_PAIRS * LANES
    spec = pl.BlockSpec((SCAN_STEP, width), lambda bi, p, i: (bi * ns + i, p))
    return pl.pallas_call(
        _rw_scan_kernel,
        grid=(batch, RW_WIDTH // width, ns),
        in_specs=[spec] * 6,
        out_specs=spec,
        out_shape=jax.ShapeDtypeStruct((t, RW_WIDTH), F32),
        scratch_shapes=[pltpu.VMEM((SCAN_PAIRS, LANES, LANES), F32)],
        compiler_params=_params("parallel", "parallel", "arbitrary"),
        name="rw_scan",
    )(r, lw, k, v, a, b)


def _rw_post_kernel(y_ref, bonus_ref, g_ref, lng_ref, lnb_ref, bd_ref, o_ref):
    y = y_ref[...]
    bd = bd_ref[...]
    mean = _segsum(y, bd) * (1.0 / RW_HEAD)
    d = y - mean
    var = _segsum(d * d, bd) * (1.0 / RW_HEAD)
    yn = d * lax.rsqrt(var + RW_GN_EPS) * lng_ref[...] + lnb_ref[...]
    o_ref[...] = ((yn + bonus_ref[...].astype(F32)) * g_ref[...].astype(F32)).astype(o_ref.dtype)


def rw_post(y, bonus, g, ln_g, ln_b, bd, *, tm=512):
    t = y.shape[0]
    row = pl.BlockSpec((tm, RW_WIDTH), lambda i: (i, 0))
    vec = pl.BlockSpec((1, RW_WIDTH), lambda i: (0, 0))
    return pl.pallas_call(
        _rw_post_kernel,
        grid=(t // tm,),
        in_specs=[row, row, row, vec, vec, pl.BlockSpec((RW_WIDTH, RW_WIDTH), lambda i: (0, 0))],
        out_specs=row,
        out_shape=jax.ShapeDtypeStruct((t, RW_WIDTH), BF16),
        compiler_params=_params("parallel"),
        name="rw_post",
    )(y, bonus, g, ln_g, ln_b, bd)


def _merge_kernel(h_ref, ya_ref, yb_ref, yc_ref, ga_ref, gb_ref, gc_ref, wa_ref, wb_ref, wc_ref, wo_ref, o_ref):
    gate = lambda g_ref: jax.nn.sigmoid(g_ref[...].astype(F32))
    merged = (gate(ga_ref) * jnp.dot(ya_ref[...], wa_ref[...], preferred_element_type=F32)
              + gate(gb_ref) * jnp.dot(yb_ref[...], wb_ref[...], preferred_element_type=F32)
              + gate(gc_ref) * jnp.dot(yc_ref[...], wc_ref[...], preferred_element_type=F32))
    o_ref[...] = h_ref[...] + jnp.dot(merged.astype(BF16), wo_ref[...], preferred_element_type=F32)


def merge(h, ya, yb, yc, proj, wa, wb, wc, wo, *, tm=512):
    t, d = h.shape
    row = lambda i: (i, 0)
    fix = lambda i: (0, 0)
    gate_specs = [pl.BlockSpec((tm, d), functools.partial(lambda i, n: (i, PROJ_GATE // d + n), n=n))
                  for n in range(3)]
    return pl.pallas_call(
        _merge_kernel,
        grid=(t // tm,),
        in_specs=[pl.BlockSpec((tm, d), row),
                  pl.BlockSpec((tm, ya.shape[1]), row),
                  pl.BlockSpec((tm, yb.shape[1]), row),
                  pl.BlockSpec((tm, yc.shape[1]), row),
                  *gate_specs,
                  pl.BlockSpec(wa.shape, fix), pl.BlockSpec(wb.shape, fix),
                  pl.BlockSpec(wc.shape, fix), pl.BlockSpec(wo.shape, fix)],
        out_specs=pl.BlockSpec((tm, d), row),
        out_shape=jax.ShapeDtypeStruct((t, d), F32),
        compiler_params=_params("parallel"),
        name="merge",
    )(h, ya, yb, yc, proj, proj, proj, wa, wb, wc, wo)


def _pad_cols(w, n):
    return jnp.pad(w, ((0, 0), (0, n - w.shape[1])))


def _layer_params(i, p):
    d = p["w_in"].shape[1]
    w_in = p["w_in"][i]
    o_kv = MLA_Q_RANK + MLA_KV_RANK
    o_sg = o_kv + MLA_ROPE
    o_rw = o_sg + 2 * SG_WIDTH
    o_gate = o_rw + RW_COLS
    zeros = lambda n: jnp.zeros((d, n), F32)
    w_mla = jnp.concatenate([w_in[:, :o_kv], zeros(MLA_NOPE), w_in[:, o_kv:o_sg],
                             zeros(LANES - MLA_QK)], axis=1)
    wuq = p["mla_w_uq"][i].reshape(MLA_Q_RANK, MLA_HEADS, MLA_QK)
    wuq = jnp.pad(wuq, ((0, 0), (0, 0), (0, LANES - MLA_QK)))
    wuq = jnp.concatenate([wuq.reshape(MLA_Q_RANK, -1), _rope_partner(wuq).reshape(MLA_Q_RANK, -1)], axis=1)
    qn = _pad_cols(p["mla_q_norm"][i].reshape(1, -1), LANES)
    kn = _pad_cols(p["mla_k_norm"][i].reshape(1, -1), LANES)
    gains = jnp.concatenate([qn, _rope_partner(qn), kn, _rope_partner(kn), jnp.zeros((4, LANES), F32)], axis=0)
    wukv = p["mla_w_ukv"][i].reshape(MLA_KV_RANK, MLA_HEADS, MLA_NOPE + MLA_V)
    wuk = jnp.pad(wukv[:, :, :MLA_NOPE], ((0, 0), (0, 0), (0, LANES - MLA_NOPE)))
    wuk = wuk.reshape(MLA_KV_RANK, MLA_HEADS * LANES)
    wuv = wukv[:, :, MLA_NOPE:].reshape(MLA_KV_RANK, MLA_HEADS * MLA_V)
    head_of = jnp.arange(RW_WIDTH) // RW_HEAD
    w_proj = jnp.concatenate([w_in[:, o_rw:o_gate], zeros(PROJ_GATE - RW_COLS), w_in[:, o_gate:],
                              w_mla, w_in[:, o_sg:o_rw]], axis=1)
    lp = {
        "w_proj": w_proj.astype(BF16),
        "qa": p["mla_q_a_norm"][i].reshape(1, -1),
        "kva": p["mla_kv_a_norm"][i].reshape(1, -1),
        "wuq": wuq.astype(BF16), "wuk": wuk.astype(BF16), "wuvt": wuv.T.astype(BF16),
        "mla_gains": gains,
        "sg_vn": p["sg_v_norm"][i].reshape(1, -1),
        "sg_w": p["sg_w_s"][i].transpose(1, 0, 2).reshape(SG_CHUNK, SG_GROUPS * SG_CHUNK),
        "sg_b": jnp.repeat(p["sg_b_s"][i].T, SG_WIDTH // SG_GROUPS, axis=1),
        "rw_mu": p["rw_mu"][i].reshape(1, -1),
        "rw_w0": p["rw_w0"][i].reshape(1, -1),
        "rw_w2": jnp.pad(p["rw_w2"][i], ((0, RW_A_RANK), (0, 0))),
        "rw_a0": p["rw_a0"][i].reshape(1, -1),
        "rw_a2": jnp.pad(p["rw_a2"][i], ((RW_DECAY_RANK, 0), (0, 0))),
        "rw_g2": p["rw_g2"][i].astype(BF16),
        "rw_k_k": p["rw_k_k"][i].reshape(1, -1),
        "rw_k_a": p["rw_k_a"][i].reshape(1, -1),
        "rw_r_k": p["rw_r_k"][i].reshape(1, -1),
        "rw_ln_g": p["rw_ln_g"][i].reshape(1, -1),
        "rw_ln_b": p["rw_ln_b"][i].reshape(1, -1),
        "rw_bd": (head_of[:, None] == head_of[None, :]).astype(BF16),
        "w_out_mla": p["w_out_mla"][i].astype(BF16),
        "w_out_sg": p["w_out_sg"][i].astype(BF16),
        "w_out_rw": p["w_out_rw"][i].astype(BF16),
        "w_o": p["w_o"][i].astype(BF16),
    }
    if i > 0:
        lp["rw_v0"] = p["rw_v0"][i - 1].reshape(1, -1)
        lp["rw_v1"] = _pad_cols(p["rw_v1"][i - 1], LANES)
        lp["rw_v2"] = jnp.pad(p["rw_v2"][i - 1], ((0, LANES - p["rw_v2"].shape[1]), (0, 0)))
    return lp


def _rope_tables(positions):
    inv_freq = ROPE_BASE ** (-jnp.arange(0, MLA_ROPE, 2, dtype=F32) / MLA_ROPE)
    ang = positions.astype(F32).reshape(-1, 1) * inv_freq
    cos, sin = jnp.cos(ang), jnp.sin(ang)
    t = ang.shape[0]
    cos_full = jnp.concatenate([jnp.ones((t, MLA_NOPE), F32), cos, cos,
                                jnp.ones((t, LANES - MLA_QK), F32)], axis=1)
    sin_signed = jnp.concatenate([jnp.zeros((t, MLA_NOPE), F32), -sin, sin,
                                  jnp.zeros((t, LANES - MLA_QK), F32)], axis=1)
    return cos_full, sin_signed


def kernel(x, positions, ffn1_norm, ffn1_w_gate, ffn1_w_up, ffn1_w_down, mix_norm, w_in, mla_q_a_norm, mla_w_uq, mla_kv_a_norm, mla_w_ukv, mla_q_norm, mla_k_norm, sg_v_norm, sg_w_s, sg_b_s, rw_mu, rw_w0, rw_w2, rw_a0, rw_a2, rw_g2, rw_k_k, rw_k_a, rw_r_k, rw_ln_g, rw_ln_b, rw_v0, rw_v1, rw_v2, w_out_mla, w_out_sg, w_out_rw, w_o, ffn2_norm, ffn2_w_gate, ffn2_w_up, ffn2_w_down):
    params = dict(ffn1_norm=ffn1_norm, ffn1_w_gate=ffn1_w_gate, ffn1_w_up=ffn1_w_up, ffn1_w_down=ffn1_w_down,
                  mix_norm=mix_norm, w_in=w_in, mla_q_a_norm=mla_q_a_norm, mla_w_uq=mla_w_uq,
                  mla_kv_a_norm=mla_kv_a_norm, mla_w_ukv=mla_w_ukv, mla_q_norm=mla_q_norm, mla_k_norm=mla_k_norm,
                  sg_v_norm=sg_v_norm, sg_w_s=sg_w_s, sg_b_s=sg_b_s, rw_mu=rw_mu, rw_w0=rw_w0, rw_w2=rw_w2,
                  rw_a0=rw_a0, rw_a2=rw_a2, rw_g2=rw_g2, rw_k_k=rw_k_k, rw_k_a=rw_k_a, rw_r_k=rw_r_k,
                  rw_ln_g=rw_ln_g, rw_ln_b=rw_ln_b, rw_v0=rw_v0, rw_v1=rw_v1, rw_v2=rw_v2,
                  w_out_mla=w_out_mla, w_out_sg=w_out_sg, w_out_rw=w_out_rw, w_o=w_o,
                  ffn2_norm=ffn2_norm, ffn2_w_gate=ffn2_w_gate, ffn2_w_up=ffn2_w_up, ffn2_w_down=ffn2_w_down)
    batch, seq, d = x.shape
    depth = w_in.shape[0]
    cos_full, sin_signed = _rope_tables(positions)
    h = x.reshape(batch * seq, d)
    v_first = None
    for i in range(depth):
        lp = _layer_params(i, params)
        h = ffn_half_step(h, ffn1_norm, ffn1_w_gate, ffn1_w_up, ffn1_w_down, i)
        proj = rms_matmul(h, mix_norm[i], lp["w_proj"], tm=1024, tn=PROJ_TILE, out_dtype=BF16)
        q, k, vt = mla_prep(proj, cos_full, sin_signed, lp["qa"], lp["kva"], lp["wuq"], lp["wuk"], lp["wuvt"],
                           lp["mla_gains"])
        y_a = mla_attention(q, k, vt, batch=batch, seq=seq)
        y_b = sgu(proj, lp["sg_vn"], lp["sg_w"], lp["sg_b"])
        r, lw, kr, vr, a, b, g, bonus = rw_prep(proj, lp, v_first, seq=seq)
        if v_first is None:
            v_first = vr
        y_scan = rw_scan(r, lw, kr, vr, a, b, batch=batch, seq=seq)
        y_c = rw_post(y_scan, bonus, g, lp["rw_ln_g"], lp["rw_ln_b"], lp["rw_bd"])
        h = merge(h, y_a, y_b, y_c, proj, lp["w_out_mla"], lp["w_out_sg"], lp["w_out_rw"], lp["w_o"])
        h = ffn_half_step(h, ffn2_norm, ffn2_w_gate, ffn2_w_up, ffn2_w_down, i)
    return h.reshape(batch, seq, d)
```

```python
import functools

import jax
import jax.numpy as jnp
from jax import lax
from jax.experimental import pallas as pl
from jax.experimental.pallas import tpu as pltpu

F32 = jnp.float32
BF16 = jnp.bfloat16

LANES = 128
EPS = 1e-6
RW_GN_EPS = 64e-5
ROPE_BASE = 10000.0

MLA_HEADS = 8
MLA_NOPE = 64
MLA_ROPE = 32
MLA_QK = MLA_NOPE + MLA_ROPE
MLA_V = 64
MLA_Q_RANK = 256
MLA_KV_RANK = 128
SG_WIDTH = 512
SG_GROUPS = 8
SG_CHUNK = 128
RW_HEADS = 8
RW_HEAD = 64
RW_WIDTH = RW_HEADS * RW_HEAD
RW_DECAY_RANK = 64
RW_A_RANK = 64
RW_GATE_RANK = 128
RW_COLS = 3 * RW_WIDTH + RW_DECAY_RANK + RW_A_RANK + RW_GATE_RANK

VMEM_LIMIT = 56 * 1024 * 1024
SCAN_CHUNK = 128
SCAN_STEP = 256
SCAN_PAIRS = 4
ATTN_BLOCK = 256
ATTN_HEADS = 8
PROJ_TILE = 3328
PREV_ROWS = 16
PROJ_GATE = 2048
PROJ_MLA = PROJ_GATE + 3 * 1024
PROJ_SG = PROJ_MLA + 512
PROJ_COLS = PROJ_SG + 2 * SG_WIDTH
SUM_ROWS = 16
LOG2E = 1.4426950408889634
DECAY_SCALE = 0.6065306597126334


def _params(*sem):
    return pltpu.CompilerParams(dimension_semantics=sem, vmem_limit_bytes=VMEM_LIMIT)


def _dot(a, b):
    return jnp.dot(a.astype(BF16), b.astype(BF16), preferred_element_type=F32)


def _dot_x3(a, b):
    ah = a.astype(BF16)
    al = (a - ah.astype(F32)).astype(BF16)
    bh = b.astype(BF16)
    bl = (b - bh.astype(F32)).astype(BF16)
    f = lambda x, y: jnp.dot(x, y, preferred_element_type=F32)
    return f(ah, bh) + f(al, bh) + f(ah, bl)


def _dot_nt(a, b):
    return lax.dot_general(a, b, (((1,), (1,)), ((), ())), preferred_element_type=F32)


def _rms(x, g, n=None):
    n = x.shape[-1] if n is None else n
    ms = jnp.sum(x * x, axis=-1, keepdims=True) * (1.0 / n)
    return x * lax.rsqrt(ms + EPS) * g


def _segsum(x, bd):
    hi = x.astype(BF16)
    lo = (x - hi.astype(F32)).astype(BF16)
    return (jnp.dot(hi, bd, preferred_element_type=F32) + jnp.dot(lo, bd, preferred_element_type=F32))


def _rms_matmul_kernel(h_ref, g_ref, w_ref, o_ref, z_ref):
    @pl.when(pl.program_id(1) == 0)
    def _():
        z_ref[...] = _rms(h_ref[...], g_ref[...]).astype(BF16)

    o_ref[...] = jnp.dot(z_ref[...], w_ref[...], preferred_element_type=F32).astype(o_ref.dtype)


def rms_matmul(h, g, w, *, tm, tn, out_dtype=F32):
    t, d = h.shape
    n = w.shape[1]
    return pl.pallas_call(
        _rms_matmul_kernel,
        grid=(t // tm, n // tn),
        in_specs=[pl.BlockSpec((tm, d), lambda i, j: (i, 0)),
                  pl.BlockSpec((1, d), lambda i, j: (0, 0)),
                  pl.BlockSpec((d, tn), lambda i, j: (0, j))],
        out_specs=pl.BlockSpec((tm, tn), lambda i, j: (i, j)),
        out_shape=jax.ShapeDtypeStruct((t, n), out_dtype),
        scratch_shapes=[pltpu.VMEM((tm, d), BF16)],
        compiler_params=_params("parallel", "arbitrary"),
        name="rms_matmul",
    )(h, g.reshape(1, d), w)


def _ffn_kernel(h_ref, g_ref, wg_ref, wu_ref, wd_ref, o_ref, z_ref, acc_ref):
    j = pl.program_id(1)

    @pl.when(j == 0)
    def _():
        z_ref[...] = _rms(h_ref[...], g_ref[...]).astype(BF16)
        acc_ref[...] = jnp.zeros_like(acc_ref)

    z = z_ref[...]
    a = jnp.dot(z, wg_ref[...].astype(BF16), preferred_element_type=F32)
    b = jnp.dot(z, wu_ref[...].astype(BF16), preferred_element_type=F32)
    t = (a * jax.nn.sigmoid(a) * b).astype(BF16)
    acc_ref[...] += jnp.dot(t, wd_ref[...].astype(BF16), preferred_element_type=F32)

    @pl.when(j == pl.num_programs(1) - 1)
    def _():
        o_ref[...] = h_ref[...] + 0.5 * acc_ref[...]


def ffn_half_step(h, g, wg, wu, wd, layer, *, tm=2048, tf=256):
    t, d = h.shape
    f = wg.shape[2]
    return pl.pallas_call(
        _ffn_kernel,
        grid=(t // tm, f // tf),
        in_specs=[pl.BlockSpec((tm, d), lambda i, j: (i, 0)),
                  pl.BlockSpec((None, 1, d), lambda i, j: (layer, 0, 0)),
                  pl.BlockSpec((None, d, tf), lambda i, j: (layer, 0, j)),
                  pl.BlockSpec((None, d, tf), lambda i, j: (layer, 0, j)),
                  pl.BlockSpec((None, tf, d), lambda i, j: (layer, j, 0))],
        out_specs=pl.BlockSpec((tm, d), lambda i, j: (i, 0)),
        out_shape=jax.ShapeDtypeStruct((t, d), F32),
        scratch_shapes=[pltpu.VMEM((tm, d), BF16), pltpu.VMEM((tm, d), F32)],
        compiler_params=_params("parallel", "arbitrary"),
        name="ffn_half_step",
    )(h, g.reshape(g.shape[0], 1, d), wg, wu, wd)


def _rope_partner(x):
    half = MLA_ROPE // 2
    zeros = jnp.zeros_like(x[..., :MLA_NOPE])
    return jnp.concatenate([zeros, x[..., MLA_NOPE + half:MLA_QK], x[..., MLA_NOPE:MLA_NOPE + half],
                            jnp.zeros_like(x[..., MLA_QK:])], axis=-1)


def _mla_prep_kernel(c_ref, cos_ref, sin_ref, qa_ref, kva_ref, wuq_ref, wuk_ref, wuvt_ref, gains_ref,
                     q_ref, k_ref, vt_ref):
    c = c_ref[...].astype(F32)
    hw = MLA_HEADS * LANES
    zq = _rms(c[:, :MLA_Q_RANK], qa_ref[...]).astype(BF16)
    zkv = _rms(c[:, MLA_Q_RANK:MLA_Q_RANK + MLA_KV_RANK], kva_ref[...]).astype(BF16)
    kpe = c[:, MLA_Q_RANK + MLA_KV_RANK:]
    q = jnp.dot(zq, wuq_ref[...], preferred_element_type=F32)
    kn = jnp.dot(zkv, wuk_ref[...], preferred_element_type=F32)
    for j in range(vt_ref.shape[0]):
        rows = slice(j * ATTN_BLOCK, (j + 1) * ATTN_BLOCK)
        vt_ref[j] = _dot_nt(wuvt_ref[...], zkv[rows]).astype(BF16)
    cos_full = cos_ref[...]
    sin_signed = sin_ref[...]
    q_cos = gains_ref[0:1, :] * cos_full
    q_sin = gains_ref[1:2, :] * sin_signed
    k_cos = gains_ref[2:3, :] * cos_full
    lane = lax.broadcasted_iota(jnp.int32, kpe.shape, 1)
    kpe_partner = jnp.where(lane < MLA_NOPE + MLA_ROPE // 2,
                            pltpu.roll(kpe, LANES - MLA_ROPE // 2, axis=1),
                            pltpu.roll(kpe, MLA_ROPE // 2, axis=1))
    k_rot = kpe_partner * (gains_ref[3:4, :] * sin_signed)
    scale = MLA_QK ** -0.5 * LOG2E
    inv_n = 1.0 / MLA_QK
    for h in range(MLA_HEADS):
        sl = slice(h * LANES, (h + 1) * LANES)
        x = q[:, sl]
        r = lax.rsqrt(jnp.sum(x * x, axis=-1, keepdims=True) * inv_n + EPS) * scale
        q_ref[:, sl] = ((x * q_cos + q[:, hw + h * LANES:hw + (h + 1) * LANES] * q_sin) * r).astype(BF16)
        y = kn[:, sl] + kpe
        r = lax.rsqrt(jnp.sum(y * y, axis=-1, keepdims=True) * inv_n + EPS)
        k_ref[:, sl] = ((y * k_cos + k_rot) * r).astype(BF16)


def mla_prep(c, cos_full, sin_signed, qa, kva, wuq, wuk, wuvt, gains, *, tm=512):
    t = c.shape[0]
    hw = MLA_HEADS * LANES
    vw = MLA_HEADS * MLA_V
    row = lambda i: (i, 0)
    fix = lambda i: (0, 0)
    return pl.pallas_call(
        _mla_prep_kernel,
        grid=(t // tm,),
        in_specs=[pl.BlockSpec((tm, PROJ_SG - PROJ_MLA), lambda i: (i, PROJ_MLA // (PROJ_SG - PROJ_MLA))),
                  pl.BlockSpec((tm, LANES), row),
                  pl.BlockSpec((tm, LANES), row),
                  pl.BlockSpec((1, MLA_Q_RANK), fix),
                  pl.BlockSpec((1, MLA_KV_RANK), fix),
                  pl.BlockSpec(wuq.shape, fix),
                  pl.BlockSpec(wuk.shape, fix),
                  pl.BlockSpec(wuvt.shape, fix),
                  pl.BlockSpec(gains.shape, fix)],
        out_specs=[pl.BlockSpec((tm, hw), row),
                   pl.BlockSpec((tm, hw), row),
                   pl.BlockSpec((tm // ATTN_BLOCK, vw, ATTN_BLOCK), lambda i: (i, 0, 0))],
        out_shape=[jax.ShapeDtypeStruct((t, hw), BF16),
                   jax.ShapeDtypeStruct((t, hw), BF16),
                   jax.ShapeDtypeStruct((t // ATTN_BLOCK, vw, ATTN_BLOCK), BF16)],
        compiler_params=_params("parallel"),
        name="mla_prep",
    )(c, cos_full, sin_signed, qa, kva, wuq, wuk, wuvt, gains)


def _attn_kernel(q_ref, k_ref, vt_ref, o_ref, st_scr, p_scr, acc_scr):
    i = pl.program_id(2)
    blk = ATTN_BLOCK
    heads = range(q_ref.shape[1] // LANES)
    q = q_ref[...]
    krow = lax.broadcasted_iota(jnp.int32, (blk, blk), 0)
    qcol = lax.broadcasted_iota(jnp.int32, (blk, blk), 1)
    ones_rows = jnp.ones((SUM_ROWS, blk), BF16)

    def put_scores(j, slot):
        ks = k_ref[pl.ds(pl.multiple_of(j * blk, blk), blk), :]
        for hh in heads:
            st_scr[slot, hh] = _dot_nt(ks[:, hh * LANES:(hh + 1) * LANES], q[:, hh * LANES:(hh + 1) * LANES])

    def weighted_values(j, slot):
        vts = vt_ref[j]
        return [jnp.dot(jnp.concatenate([vts[hh * MLA_V:(hh + 1) * MLA_V], ones_rows], axis=0), p_scr[slot, hh],
                        preferred_element_type=F32) for hh in heads]

    def step(j, slot, alphas_prev, ms, masked, prefetch):
        if prefetch:
            put_scores(j + 1, 1 - slot)
        pv = weighted_values(jnp.maximum(j - 1, 0), 1 - slot)
        sts = [st_scr[slot, hh] for hh in heads]
        if masked:
            keep = krow + (j - i) * blk <= qcol
            sts = [jnp.where(keep, st, -jnp.inf) for st in sts]
        m_new = tuple(jnp.maximum(ms[hh], jnp.max(sts[hh], axis=0, keepdims=True)) for hh in heads)
        for hh in heads:
            p_scr[slot, hh] = jnp.exp2(sts[hh] - m_new[hh]).astype(BF16)
            acc_scr[hh] = alphas_prev[hh] * acc_scr[hh] + pv[hh]
        return tuple(jnp.exp2(ms[hh] - m_new[hh]) for hh in heads), m_new

    def pair(t, carry, masked):
        alphas, ms = step(2 * t, 0, *carry, masked, True)
        return step(2 * t + 1, 1, alphas, ms, masked, not masked)

    p_scr[1] = jnp.zeros_like(p_scr[1])
    acc_scr[...] = jnp.zeros_like(acc_scr)
    put_scores(0, 0)
    init = (tuple(jnp.ones((1, blk), F32) for _ in heads), tuple(jnp.full((1, blk), -jnp.inf, F32) for _ in heads))
    carry = lax.fori_loop(0, i // 2, lambda t, c: pair(t, c, False), init)
    alphas, _ = pair(i // 2, carry, True)
    pv = weighted_values(2 * (i // 2) + 1, 1)
    outs = []
    for hh in heads:
        acc = alphas[hh] * acc_scr[hh] + pv[hh]
        outs.append(acc[:MLA_V] / acc[MLA_V:MLA_V + 1])
    o_ref[...] = jnp.concatenate(outs, axis=0).T.astype(o_ref.dtype)


def mla_attention(q, k, vt, *, batch, seq):
    t = q.shape[0]
    nq = seq // ATTN_BLOCK
    return pl.pallas_call(
        _attn_kernel,
        grid=(batch, MLA_HEADS // ATTN_HEADS, nq),
        in_specs=[pl.BlockSpec((ATTN_BLOCK, ATTN_HEADS * LANES), lambda b, p, i: (b * nq + i, p)),
                  pl.BlockSpec((seq, ATTN_HEADS * LANES), lambda b, p, i: (b, p)),
                  pl.BlockSpec((nq, ATTN_HEADS * MLA_V, ATTN_BLOCK), lambda b, p, i: (b, p, 0))],
        out_specs=pl.BlockSpec((ATTN_BLOCK, ATTN_HEADS * MLA_V), lambda b, p, i: (b * nq + i, p)),
        out_shape=jax.ShapeDtypeStruct((t, MLA_HEADS * MLA_V), BF16),
        scratch_shapes=[pltpu.VMEM((2, ATTN_HEADS, ATTN_BLOCK, ATTN_BLOCK), F32),
                        pltpu.VMEM((2, ATTN_HEADS, ATTN_BLOCK, ATTN_BLOCK), BF16),
                        pltpu.VMEM((ATTN_HEADS, MLA_V + SUM_ROWS, ATTN_BLOCK), F32)],
        compiler_params=_params("parallel", "parallel", "arbitrary"),
        name="mla_attention",
    )(q, k, vt)


def _sgu_kernel(pu_ref, pv_ref, vn_ref, w_ref, b_ref, o_ref):
    u = jax.nn.gelu(pu_ref[...].astype(F32), approximate=True)
    v = _rms(jax.nn.gelu(pv_ref[...].astype(F32), approximate=True), vn_ref[...])
    wrow = lax.broadcasted_iota(jnp.int32, (SG_CHUNK, SG_GROUPS * SG_CHUNK), 0)
    wcol = lax.broadcasted_iota(jnp.int32, (SG_CHUNK, SG_GROUPS * SG_CHUNK), 1) % SG_CHUNK
    w = jnp.where(wcol <= wrow, w_ref[...], 0.0).astype(BF16)
    low = lax.broadcasted_iota(jnp.int32, (SG_CHUNK, LANES), 1) < SG_WIDTH // SG_GROUPS
    for c in range(u.shape[0] // SG_CHUNK):
        rows = slice(c * SG_CHUNK, (c + 1) * SG_CHUNK)
        for q in range(SG_WIDTH // LANES):
            cols = slice(q * LANES, (q + 1) * LANES)
            vq = v[rows, cols]
            halves = jnp.concatenate([jnp.where(low, vq, 0.0), jnp.where(low, 0.0, vq)], axis=0).astype(BF16)
            mixed = jnp.dot(w[:, 2 * q * SG_CHUNK:2 * (q + 1) * SG_CHUNK], halves,
                            preferred_element_type=F32) + b_ref[:, cols]
            o_ref[rows, cols] = (u[rows, cols] * mixed).astype(o_ref.dtype)


def sgu(proj, v_norm, w_cat, b_full, *, tm=512):
    t = proj.shape[0]
    fix = lambda i: (0, 0)
    return pl.pallas_call(
        _sgu_kernel,
        grid=(t // tm,),
        in_specs=[pl.BlockSpec((tm, SG_WIDTH), lambda i: (i, PROJ_SG // SG_WIDTH)),
                  pl.BlockSpec((tm, SG_WIDTH), lambda i: (i, PROJ_SG // SG_WIDTH + 1)),
                  pl.BlockSpec((1, SG_WIDTH), fix),
                  pl.BlockSpec(w_cat.shape, fix),
                  pl.BlockSpec(b_full.shape, fix)],
        out_specs=pl.BlockSpec((tm, SG_WIDTH), lambda i: (i, 0)),
        out_shape=jax.ShapeDtypeStruct((t, SG_WIDTH), BF16),
        compiler_params=_params("parallel"),
        name="sgu",
    )(proj, proj, v_norm, w_cat, b_full)


def _rw_prep_kernel(*refs, tiles_per_seq, has_vres):
    if has_vres:
        (p_ref, prev_ref, mu_ref, w0_ref, w2_ref, a0_ref, a2_ref, g2_ref, kk_ref, ka_ref, rk_ref, bd_ref,
         vf_ref, v0_ref, v1_ref, v2_ref,
         r_o, lw_o, k_o, v_o, a_o, b_o, g_o, bonus_o) = refs
    else:
        (p_ref, prev_ref, mu_ref, w0_ref, w2_ref, a0_ref, a2_ref, g2_ref, kk_ref, ka_ref, rk_ref, bd_ref,
         r_o, lw_o, k_o, v_o, a_o, b_o, g_o, bonus_o) = refs
    cur = p_ref[...].astype(F32)
    tm = cur.shape[0]
    first = (pl.program_id(0) % tiles_per_seq) == 0
    last_prev = jnp.where(first, 0.0, prev_ref[PREV_ROWS - 1:PREV_ROWS, :].astype(F32))
    rolled = pltpu.roll(cur, 1, axis=0)
    top = lax.broadcasted_iota(jnp.int32, (8, cur.shape[1]), 0) == 0
    prev = jnp.concatenate([jnp.where(top, last_prev, rolled[:8]), rolled[8:]], axis=0)
    x = cur + (prev - cur) * mu_ref[...]
    w3 = 3 * RW_WIDTH
    r = x[:, :RW_WIDTH]
    k = x[:, RW_WIDTH:2 * RW_WIDTH]
    v = x[:, 2 * RW_WIDTH:w3]
    xwa = x[:, w3:w3 + LANES]
    xg = x[:, w3 + LANES:]
    u = w0_ref[...] + _dot_x3(jnp.tanh(xwa), w2_ref[...])
    lw_o[...] = -DECAY_SCALE * jax.nn.sigmoid(u)
    a = jax.nn.sigmoid(a0_ref[...] + _dot_x3(xwa, a2_ref[...]))
    g_o[...] = _dot(jax.nn.sigmoid(xg), g2_ref[...]).astype(g_o.dtype)
    if has_vres:
        mix = jax.nn.sigmoid(v0_ref[...] + _dot_x3(_dot_x3(v, v1_ref[...]), v2_ref[...]))
        v = v + (vf_ref[...].astype(F32) - v) * mix
    bd = bd_ref[...]
    kk = k * kk_ref[...]
    kk = kk * lax.rsqrt(jnp.maximum(_segsum(kk * kk, bd), 1e-24))
    k = k * (1.0 + (a - 1.0) * ka_ref[...])
    r_o[...] = r.astype(r_o.dtype)
    k_o[...] = k.astype(k_o.dtype)
    v_o[...] = v.astype(v_o.dtype)
    a_o[...] = (-kk).astype(a_o.dtype)
    b_o[...] = (kk * a).astype(b_o.dtype)
    bonus_o[...] = (_segsum(r * k * rk_ref[...], bd) * v).astype(bonus_o.dtype)


def rw_prep(proj, lp, v_first, *, seq, tm=256):
    t = proj.shape[0]
    has_vres = v_first is not None
    row = lambda i: (i, 0)
    fix = lambda i: (0, 0)
    vec = pl.BlockSpec((1, RW_WIDTH), fix)
    in_specs = [pl.BlockSpec((tm, RW_COLS), row),
                pl.BlockSpec((PREV_ROWS, RW_COLS), lambda i: (jnp.maximum(i * (tm // PREV_ROWS) - 1, 0), 0)),
                pl.BlockSpec((1, RW_COLS), fix),
                vec, pl.BlockSpec((LANES, RW_WIDTH), fix),
                vec, pl.BlockSpec((LANES, RW_WIDTH), fix),
                pl.BlockSpec((RW_GATE_RANK, RW_WIDTH), fix),
                vec, vec, vec, pl.BlockSpec((RW_WIDTH, RW_WIDTH), fix)]
    args = [proj, proj, lp["rw_mu"], lp["rw_w0"], lp["rw_w2"], lp["rw_a0"], lp["rw_a2"], lp["rw_g2"],
            lp["rw_k_k"], lp["rw_k_a"], lp["rw_r_k"], lp["rw_bd"]]
    if has_vres:
        in_specs += [pl.BlockSpec((tm, RW_WIDTH), row), vec,
                     pl.BlockSpec(lp["rw_v1"].shape, fix), pl.BlockSpec(lp["rw_v2"].shape, fix)]
        args += [v_first, lp["rw_v0"], lp["rw_v1"], lp["rw_v2"]]
    dtypes = [BF16, F32, BF16, BF16, BF16, BF16, BF16, BF16]
    return pl.pallas_call(
        functools.partial(_rw_prep_kernel, tiles_per_seq=seq // tm, has_vres=has_vres),
        grid=(t // tm,),
        in_specs=in_specs,
        out_specs=[pl.BlockSpec((tm, RW_WIDTH), row)] * 8,
        out_shape=[jax.ShapeDtypeStruct((t, RW_WIDTH), dt) for dt in dtypes],
        compiler_params=_params("parallel"),
        name="rw_prep",
    )(*args)


def _cumsum_rows(tri, x):
    h1 = x.astype(BF16)
    r1 = x - h1.astype(F32)
    h2 = r1.astype(BF16)
    h3 = (r1 - h2.astype(F32)).astype(BF16)
    f = lambda y: jnp.dot(tri, y, preferred_element_type=F32)
    return f(h1) + f(h2) + f(h3)


def _rows_bcast(x, c, which):
    n = x.shape[0] // c
    return jnp.concatenate([jnp.broadcast_to(x[j * c + which:j * c + which + 1], (c, x.shape[1]))
                            for j in range(n)], axis=0)


def _rw_scan_kernel(r_ref, lw_ref, k_ref, v_ref, a_ref, b_ref, bonus_ref, g_ref, lng_ref, lnb_ref, y_ref, s_ref):
    c = SCAN_CHUNK
    n_tok = r_ref.shape[0]
    pairs = range(r_ref.shape[1] // LANES)
    heads = [(p, hh) for p in pairs for hh in range(2)]
    nch = n_tok // c

    @pl.when(pl.program_id(2) == 0)
    def _():
        s_ref[...] = jnp.zeros_like(s_ref)

    ri = lax.broadcasted_iota(jnp.int32, (n_tok, n_tok), 0)
    ci = lax.broadcasted_iota(jnp.int32, (n_tok, n_tok), 1)
    same = (ri // c) == (ci // c)
    incl = jnp.logical_and(same, ci <= ri)
    strict = jnp.logical_and(same, ci < ri)
    eye = jnp.where(ri == ci, 1.0, 0.0)
    tri = jnp.where(incl, 1.0, 0.0).astype(BF16)
    head0 = lax.broadcasted_iota(jnp.int32, (n_tok, LANES), 1) < RW_HEAD
    in_head = (head0, jnp.logical_not(head0))
    sr = lax.broadcasted_iota(jnp.int32, (LANES, LANES), 0)
    sc = lax.broadcasted_iota(jnp.int32, (LANES, LANES), 1)
    same_head = (sr // RW_HEAD) == (sc // RW_HEAD)
    eye_s = jnp.where(sr == sc, 1.0, 0.0).astype(BF16)

    cols = lambda ref, p: ref[:, p * LANES:(p + 1) * LANES]
    lw = [cols(lw_ref, p) for p in pairs]
    cum = [_cumsum_rows(tri, x) for x in lw]
    cum_mid = [_rows_bcast(x, c, c // 2 - 1) for x in cum]
    cum_end = [_rows_bcast(x, c, c - 1) for x in cum]
    r = [cols(r_ref, p).astype(F32) for p in pairs]
    a = [cols(a_ref, p).astype(F32) for p in pairs]
    k = [cols(k_ref, p).astype(F32) for p in pairs]
    b = [cols(b_ref, p).astype(F32) for p in pairs]
    v = [cols(v_ref, p).astype(F32) for p in pairs]
    r_abs = [r[p] * jnp.exp(cum[p]) for p in pairs]
    a_abs = [a[p] * jnp.exp(cum[p] - lw[p]) for p in pairs]
    g_end = [jnp.exp(cum_end[p] - cum[p]) for p in pairs]
    bk_end = [[jnp.concatenate([(b[p] * g_end[p])[j * c:(j + 1) * c], (k[p] * g_end[p])[j * c:(j + 1) * c]],
                               axis=0).astype(BF16) for j in range(n_tok // c)] for p in pairs]
    g_bwd = [jnp.exp(cum_mid[p] - cum[p]) for p in pairs]
    r_mid = [r[p] * jnp.exp(cum[p] - cum_mid[p]) for p in pairs]
    a_mid = [a[p] * jnp.exp(cum[p] - lw[p] - cum_mid[p]) for p in pairs]
    rhs = [jnp.concatenate([b[p] * g_bwd[p], k[p] * g_bwd[p]], axis=0).astype(BF16) for p in pairs]
    a_h = [jnp.where(in_head[hh], a_mid[p], 0.0).astype(BF16) for p, hh in heads]
    r_h = [jnp.where(in_head[hh], r_mid[p], 0.0).astype(BF16) for p, hh in heads]
    pm = [[_dot_nt(jnp.concatenate([a_h[n][j * c:(j + 1) * c], r_h[n][j * c:(j + 1) * c]], axis=0),
                   jnp.concatenate([rhs[p][j * c:(j + 1) * c], rhs[p][n_tok + j * c:n_tok + (j + 1) * c]], axis=0))
           for j in range(nch)] for n, (p, hh) in enumerate(heads)]
    strict_c = strict[:c, :c]
    incl_c = incl[:c, :c]
    zero_cc = jnp.zeros((c, c), F32)

    def block_diag(blocks):
        return jnp.concatenate([jnp.concatenate([blocks[j] if jj == j else zero_cc for jj in range(nch)], axis=1)
                                for j in range(nch)], axis=0)

    lab = [block_diag([jnp.where(strict_c, x[:c, :c], 0.0) for x in xs]) for xs in pm]
    lak = [block_diag([jnp.where(strict_c, x[:c, c:], 0.0) for x in xs]).astype(BF16) for xs in pm]
    mrb = [block_diag([jnp.where(incl_c, x[c:, :c], 0.0) for x in xs]).astype(BF16) for xs in pm]
    mrk = [block_diag([jnp.where(incl_c, x[c:, c:], 0.0) for x in xs]).astype(BF16) for xs in pm]
    vh = [jnp.where(in_head[hh], v[p], 0.0).astype(BF16) for p, hh in heads]
    inv = [eye + x for x in lab]
    lp = lab
    q = 2
    while q < c:
        lp = [_dot(x, x) for x in lp]
        inv = [t + _dot(t, x) for t, x in zip(inv, lp)]
        q *= 2
    xv = [jnp.dot(lak[n], vh[n], preferred_element_type=F32) for n in range(len(heads))]
    tw = [_dot(inv[n], jnp.concatenate([jnp.where(in_head[hh], a_abs[p], 0.0), xv[n]], axis=1))
          for n, (p, hh) in enumerate(heads)]
    y2h = [jnp.dot(mrk[n], vh[n], preferred_element_type=F32) for n in range(len(heads))]
    w1 = [tw[2 * p][:, :LANES] + tw[2 * p + 1][:, :LANES] for p in pairs]
    w2 = [tw[2 * p][:, LANES:] + tw[2 * p + 1][:, LANES:] for p in pairs]
    w12 = [jnp.concatenate([w1[p], w2[p]], axis=1).astype(BF16) for p in pairs]
    lhs_xy = [jnp.concatenate([jnp.concatenate([w1[p][j * c:(j + 1) * c], r_abs[p][j * c:(j + 1) * c]], axis=0)
                               for j in range(n_tok // c)], axis=0).astype(BF16) for p in pairs]
    vb = [x.astype(BF16) for x in v]
    zeros_c = jnp.zeros((c, LANES), BF16)
    u_rows = [[] for _ in pairs]
    y0_rows = [[] for _ in pairs]
    for j in range(n_tok // c):
        rows = slice(j * c, (j + 1) * c)
        s0 = [s_ref[p] for p in pairs]
        s0b = [x.astype(BF16) for x in s0]
        xy = [_dot_nt(lhs_xy[p][2 * j * c:2 * (j + 1) * c], s0b[p]) for p in pairs]
        uvt = [_dot_nt(jnp.concatenate([s0b[p], eye_s], axis=1),
                       jnp.concatenate([w12[p][rows], jnp.concatenate([zeros_c, vb[p][rows]], axis=1)], axis=0))
               for p in pairs]
        upd = [jnp.dot(uvt[p].astype(BF16), bk_end[p][j], preferred_element_type=F32) for p in pairs]
        for p in pairs:
            u_rows[p].append(xy[p][:c] + w2[p][rows])
            y0_rows[p].append(xy[p][c:])
            s_ref[p] = s0[p] * jnp.exp(cum_end[p][j * c:j * c + 1, :]) + jnp.where(same_head, upd[p], 0.0)
    ub = [jnp.concatenate(u_rows[p], axis=0).astype(BF16) for p in pairs]
    y1 = [jnp.dot(mrb[n], ub[p], preferred_element_type=F32) for n, (p, hh) in enumerate(heads)]
    half_sum = lambda x: jnp.where(head0, jnp.sum(jnp.where(head0, x, 0.0), axis=-1, keepdims=True),
                                   jnp.sum(jnp.where(head0, 0.0, x), axis=-1, keepdims=True))
    for p in pairs:
        sl = slice(p * LANES, (p + 1) * LANES)
        y = (jnp.concatenate(y0_rows[p], axis=0) + y2h[2 * p] + y2h[2 * p + 1]
             + jnp.where(head0, y1[2 * p], y1[2 * p + 1]))
        d = y - half_sum(y) * (1.0 / RW_HEAD)
        var = half_sum(d * d) * (1.0 / RW_HEAD)
        yn = d * lax.rsqrt(var + RW_GN_EPS) * lng_ref[:, sl] + lnb_ref[:, sl]
        y_ref[:, sl] = ((yn + bonus_ref[:, sl].astype(F32)) * g_ref[:, sl].astype(F32)).astype(y_ref.dtype)


def rw_scan(r, lw, k, v, a, b, bonus, g, ln_g, ln_b, *, batch, seq):
    t = r.shape[0]
    ns = seq // SCAN_STEP
    width = SCAN_PAIRS * LANES
    spec = pl.BlockSpec((SCAN_STEP, width), lambda bi, p, i: (bi * ns + i, p))
    vec = pl.BlockSpec((1, width), lambda bi, p, i: (0, p))
    return pl.pallas_call(
        _rw_scan_kernel,
        grid=(batch, RW_WIDTH // width, ns),
        in_specs=[spec] * 8 + [vec] * 2,
        out_specs=spec,
        out_shape=jax.ShapeDtypeStruct((t, RW_WIDTH), BF16),
        scratch_shapes=[pltpu.VMEM((SCAN_PAIRS, LANES, LANES), F32)],
        compiler_params=_params("parallel", "parallel", "arbitrary"),
        name="rw_scan",
    )(r, lw, k, v, a, b, bonus, g, ln_g, ln_b)


def _merge_kernel(h_ref, ya_ref, yb_ref, yc_ref, ga_ref, gb_ref, gc_ref, wa_ref, wb_ref, wc_ref, wo_ref, o_ref):
    gate = lambda g_ref: jax.nn.sigmoid(g_ref[...].astype(F32))
    merged = (gate(ga_ref) * jnp.dot(ya_ref[...], wa_ref[...], preferred_element_type=F32)
              + gate(gb_ref) * jnp.dot(yb_ref[...], wb_ref[...], preferred_element_type=F32)
              + gate(gc_ref) * jnp.dot(yc_ref[...], wc_ref[...], preferred_element_type=F32))
    o_ref[...] = h_ref[...] + jnp.dot(merged.astype(BF16), wo_ref[...], preferred_element_type=F32)


def merge(h, ya, yb, yc, proj, wa, wb, wc, wo, *, tm=512):
    t, d = h.shape
    row = lambda i: (i, 0)
    fix = lambda i: (0, 0)
    gate_specs = [pl.BlockSpec((tm, d), functools.partial(lambda i, n: (i, PROJ_GATE // d + n), n=n))
                  for n in range(3)]
    return pl.pallas_call(
        _merge_kernel,
        grid=(t // tm,),
        in_specs=[pl.BlockSpec((tm, d), row),
                  pl.BlockSpec((tm, ya.shape[1]), row),
                  pl.BlockSpec((tm, yb.shape[1]), row),
                  pl.BlockSpec((tm, yc.shape[1]), row),
                  *gate_specs,
                  pl.BlockSpec(wa.shape, fix), pl.BlockSpec(wb.shape, fix),
                  pl.BlockSpec(wc.shape, fix), pl.BlockSpec(wo.shape, fix)],
        out_specs=pl.BlockSpec((tm, d), row),
        out_shape=jax.ShapeDtypeStruct((t, d), F32),
        compiler_params=_params("parallel"),
        name="merge",
    )(h, ya, yb, yc, proj, proj, proj, wa, wb, wc, wo)


def _pad_cols(w, n):
    return jnp.pad(w, ((0, 0), (0, n - w.shape[1])))


def _layer_params(i, p):
    d = p["w_in"].shape[1]
    w_in = p["w_in"][i].astype(BF16)
    o_kv = MLA_Q_RANK + MLA_KV_RANK
    o_sg = o_kv + MLA_ROPE
    o_rw = o_sg + 2 * SG_WIDTH
    o_gate = o_rw + RW_COLS
    zeros = lambda n: jnp.zeros((d, n), BF16)
    w_mla = jnp.concatenate([w_in[:, :o_kv], zeros(MLA_NOPE), w_in[:, o_kv:o_sg],
                             zeros(LANES - MLA_QK)], axis=1)
    wuq = p["mla_w_uq"][i].reshape(MLA_Q_RANK, MLA_HEADS, MLA_QK)
    wuq = jnp.pad(wuq, ((0, 0), (0, 0), (0, LANES - MLA_QK)))
    wuq = jnp.concatenate([wuq.reshape(MLA_Q_RANK, -1), _rope_partner(wuq).reshape(MLA_Q_RANK, -1)], axis=1)
    qn = _pad_cols(p["mla_q_norm"][i].reshape(1, -1), LANES)
    kn = _pad_cols(p["mla_k_norm"][i].reshape(1, -1), LANES)
    gains = jnp.concatenate([qn, _rope_partner(qn), kn, _rope_partner(kn), jnp.zeros((4, LANES), F32)], axis=0)
    wukv = p["mla_w_ukv"][i].reshape(MLA_KV_RANK, MLA_HEADS, MLA_NOPE + MLA_V)
    wuk = jnp.pad(wukv[:, :, :MLA_NOPE], ((0, 0), (0, 0), (0, LANES - MLA_NOPE)))
    wuk = wuk.reshape(MLA_KV_RANK, MLA_HEADS * LANES)
    wuv = wukv[:, :, MLA_NOPE:].reshape(MLA_KV_RANK, MLA_HEADS * MLA_V)
    head_of = jnp.arange(RW_WIDTH) // RW_HEAD
    w_proj = jnp.concatenate([w_in[:, o_rw:o_gate], zeros(PROJ_GATE - RW_COLS), w_in[:, o_gate:],
                              w_mla, w_in[:, o_sg:o_rw]], axis=1)
    lp = {
        "w_proj": w_proj,
        "qa": p["mla_q_a_norm"][i].reshape(1, -1),
        "kva": p["mla_kv_a_norm"][i].reshape(1, -1),
        "wuq": wuq.astype(BF16), "wuk": wuk.astype(BF16), "wuvt": wuv.T.astype(BF16),
        "mla_gains": gains,
        "sg_vn": p["sg_v_norm"][i].reshape(1, -1),
        "sg_w": p["sg_w_s"][i].transpose(1, 0, 2).reshape(SG_CHUNK, SG_GROUPS * SG_CHUNK),
        "sg_b": jnp.repeat(p["sg_b_s"][i].T, SG_WIDTH // SG_GROUPS, axis=1),
        "rw_mu": p["rw_mu"][i].reshape(1, -1),
        "rw_w0": p["rw_w0"][i].reshape(1, -1),
        "rw_w2": jnp.pad(p["rw_w2"][i], ((0, RW_A_RANK), (0, 0))),
        "rw_a0": p["rw_a0"][i].reshape(1, -1),
        "rw_a2": jnp.pad(p["rw_a2"][i], ((RW_DECAY_RANK, 0), (0, 0))),
        "rw_g2": p["rw_g2"][i].astype(BF16),
        "rw_k_k": p["rw_k_k"][i].reshape(1, -1),
        "rw_k_a": p["rw_k_a"][i].reshape(1, -1),
        "rw_r_k": p["rw_r_k"][i].reshape(1, -1),
        "rw_ln_g": p["rw_ln_g"][i].reshape(1, -1),
        "rw_ln_b": p["rw_ln_b"][i].reshape(1, -1),
        "rw_bd": (head_of[:, None] == head_of[None, :]).astype(BF16),
        "w_out_mla": p["w_out_mla"][i].astype(BF16),
        "w_out_sg": p["w_out_sg"][i].astype(BF16),
        "w_out_rw": p["w_out_rw"][i].astype(BF16),
        "w_o": p["w_o"][i].astype(BF16),
    }
    if i > 0:
        lp["rw_v0"] = p["rw_v0"][i - 1].reshape(1, -1)
        lp["rw_v1"] = _pad_cols(p["rw_v1"][i - 1], LANES)
        lp["rw_v2"] = jnp.pad(p["rw_v2"][i - 1], ((0, LANES - p["rw_v2"].shape[1]), (0, 0)))
    return lp


def _rope_tables(positions):
    inv_freq = ROPE_BASE ** (-jnp.arange(0, MLA_ROPE, 2, dtype=F32) / MLA_ROPE)
    ang = positions.astype(F32).reshape(-1, 1) * inv_freq
    cos, sin = jnp.cos(ang), jnp.sin(ang)
    t = ang.shape[0]
    cos_full = jnp.concatenate([jnp.ones((t, MLA_NOPE), F32), cos, cos,
                                jnp.ones((t, LANES - MLA_QK), F32)], axis=1)
    sin_signed = jnp.concatenate([jnp.zeros((t, MLA_NOPE), F32), -sin, sin,
                                  jnp.zeros((t, LANES - MLA_QK), F32)], axis=1)
    return cos_full, sin_signed


def kernel(x, positions, ffn1_norm, ffn1_w_gate, ffn1_w_up, ffn1_w_down, mix_norm, w_in, mla_q_a_norm, mla_w_uq, mla_kv_a_norm, mla_w_ukv, mla_q_norm, mla_k_norm, sg_v_norm, sg_w_s, sg_b_s, rw_mu, rw_w0, rw_w2, rw_a0, rw_a2, rw_g2, rw_k_k, rw_k_a, rw_r_k, rw_ln_g, rw_ln_b, rw_v0, rw_v1, rw_v2, w_out_mla, w_out_sg, w_out_rw, w_o, ffn2_norm, ffn2_w_gate, ffn2_w_up, ffn2_w_down):
    params = dict(ffn1_norm=ffn1_norm, ffn1_w_gate=ffn1_w_gate, ffn1_w_up=ffn1_w_up, ffn1_w_down=ffn1_w_down,
                  mix_norm=mix_norm, w_in=w_in, mla_q_a_norm=mla_q_a_norm, mla_w_uq=mla_w_uq,
                  mla_kv_a_norm=mla_kv_a_norm, mla_w_ukv=mla_w_ukv, mla_q_norm=mla_q_norm, mla_k_norm=mla_k_norm,
                  sg_v_norm=sg_v_norm, sg_w_s=sg_w_s, sg_b_s=sg_b_s, rw_mu=rw_mu, rw_w0=rw_w0, rw_w2=rw_w2,
                  rw_a0=rw_a0, rw_a2=rw_a2, rw_g2=rw_g2, rw_k_k=rw_k_k, rw_k_a=rw_k_a, rw_r_k=rw_r_k,
                  rw_ln_g=rw_ln_g, rw_ln_b=rw_ln_b, rw_v0=rw_v0, rw_v1=rw_v1, rw_v2=rw_v2,
                  w_out_mla=w_out_mla, w_out_sg=w_out_sg, w_out_rw=w_out_rw, w_o=w_o,
                  ffn2_norm=ffn2_norm, ffn2_w_gate=ffn2_w_gate, ffn2_w_up=ffn2_w_up, ffn2_w_down=ffn2_w_down)
    batch, seq, d = x.shape
    depth = w_in.shape[0]
    cos_full, sin_signed = _rope_tables(positions)
    h = x.reshape(batch * seq, d)
    v_first = None
    for i in range(depth):
        lp = _layer_params(i, params)
        h = ffn_half_step(h, ffn1_norm, ffn1_w_gate, ffn1_w_up, ffn1_w_down, i)
        proj = rms_matmul(h, mix_norm[i], lp["w_proj"], tm=1024, tn=PROJ_TILE, out_dtype=BF16)
        q, k, vt = mla_prep(proj, cos_full, sin_signed, lp["qa"], lp["kva"], lp["wuq"], lp["wuk"], lp["wuvt"],
                           lp["mla_gains"])
        y_a = mla_attention(q, k, vt, batch=batch, seq=seq)
        y_b = sgu(proj, lp["sg_vn"], lp["sg_w"], lp["sg_b"])
        r, lw, kr, vr, a, b, g, bonus = rw_prep(proj, lp, v_first, seq=seq)
        if v_first is None:
            v_first = vr
        y_c = rw_scan(r, lw, kr, vr, a, b, bonus, g, lp["rw_ln_g"], lp["rw_ln_b"], batch=batch, seq=seq)
        h = merge(h, y_a, y_b, y_c, proj, lp["w_out_mla"], lp["w_out_sg"], lp["w_out_rw"], lp["w_o"])
        h = ffn_half_step(h, ffn2_norm, ffn2_w_gate, ffn2_w_up, ffn2_w_down, i)
    return h.reshape(batch, seq, d)
```

```python
import functools

import jax
import jax.numpy as jnp
from jax import lax
from jax.experimental import pallas as pl
from jax.experimental.pallas import tpu as pltpu

F32 = jnp.float32
BF16 = jnp.bfloat16

LANES = 128
EPS = 1e-6
RW_GN_EPS = 64e-5
ROPE_BASE = 10000.0

MLA_HEADS = 8
MLA_NOPE = 64
MLA_ROPE = 32
MLA_QK = MLA_NOPE + MLA_ROPE
MLA_V = 64
MLA_Q_RANK = 256
MLA_KV_RANK = 128
SG_WIDTH = 512
SG_GROUPS = 8
SG_CHUNK = 128
RW_HEADS = 8
RW_HEAD = 64
RW_WIDTH = RW_HEADS * RW_HEAD
RW_DECAY_RANK = 64
RW_A_RANK = 64
RW_GATE_RANK = 128
RW_COLS = 3 * RW_WIDTH + RW_DECAY_RANK + RW_A_RANK + RW_GATE_RANK

V7X_VMEM_BYTES = 64 * 1024 * 1024
VMEM_LIMIT = V7X_VMEM_BYTES * 7 // 8
BF16_SUBLANES = 16

FFN_ROWS = 2048
FFN_COLS = 256
PROJ_ROWS = 1024
PROJ_TILE = 3328
MLA_PREP_ROWS = 1024
SGU_ROWS = 1024
RW_PREP_ROWS = 512
MERGE_ROWS = 1024
SCAN_CHUNK = 128
SCAN_STEP = 256
SCAN_PAIRS = 4
ATTN_BLOCK = 256
ATTN_HEADS = 8
PROJ_GATE = 2048
PROJ_MLA = PROJ_GATE + 3 * 1024
PROJ_SG = PROJ_MLA + 512
PROJ_COLS = PROJ_SG + 2 * SG_WIDTH
PREV_ROWS = BF16_SUBLANES
SUM_ROWS = BF16_SUBLANES
LOG2E = 1.4426950408889634
DECAY_SCALE = 0.6065306597126334


def _params(*sem):
    return pltpu.CompilerParams(dimension_semantics=sem, vmem_limit_bytes=VMEM_LIMIT)


def _dot(a, b):
    return jnp.dot(a.astype(BF16), b.astype(BF16), preferred_element_type=F32)


def _dot_x3(a, b):
    ah = a.astype(BF16)
    al = (a - ah.astype(F32)).astype(BF16)
    bh = b.astype(BF16)
    bl = (b - bh.astype(F32)).astype(BF16)
    f = lambda x, y: jnp.dot(x, y, preferred_element_type=F32)
    return f(ah, bh) + f(al, bh) + f(ah, bl)


def _dot_nt(a, b):
    return lax.dot_general(a, b, (((1,), (1,)), ((), ())), preferred_element_type=F32)


def _rms(x, g, n=None):
    n = x.shape[-1] if n is None else n
    ms = jnp.sum(x * x, axis=-1, keepdims=True) * (1.0 / n)
    return x * lax.rsqrt(ms + EPS) * g


def _segsum(x, bd):
    hi = x.astype(BF16)
    lo = (x - hi.astype(F32)).astype(BF16)
    return (jnp.dot(hi, bd, preferred_element_type=F32) + jnp.dot(lo, bd, preferred_element_type=F32))


def _proj_weight_kernel(w_ref, o_ref):
    x = w_ref[...]
    zeros = lambda n: jnp.zeros((x.shape[0], n), x.dtype)
    o_kv = MLA_Q_RANK + MLA_KV_RANK
    o_sg = o_kv + MLA_ROPE
    o_rw = o_sg + 2 * SG_WIDTH
    o_gate = o_rw + RW_COLS
    o_ref[...] = jnp.concatenate(
        [x[:, o_rw:o_gate], zeros(PROJ_GATE - RW_COLS), x[:, o_gate:],
         x[:, :o_kv], zeros(MLA_NOPE), x[:, o_kv:o_sg], zeros(LANES - MLA_QK),
         x[:, o_sg:o_rw]], axis=1).astype(o_ref.dtype)


def proj_weight(w_in, layer, *, tr=LANES):
    _, d, n = w_in.shape
    return pl.pallas_call(
        _proj_weight_kernel,
        grid=(d // tr,),
        in_specs=[pl.BlockSpec((None, tr, n), lambda i: (layer, i, 0))],
        out_specs=pl.BlockSpec((tr, PROJ_COLS), lambda i: (i, 0)),
        out_shape=jax.ShapeDtypeStruct((d, PROJ_COLS), BF16),
        compiler_params=_params("parallel"),
        name="proj_weight",
    )(w_in)


def _rms_matmul_kernel(h_ref, g_ref, w_ref, o_ref, z_ref):
    @pl.when(pl.program_id(1) == 0)
    def _():
        z_ref[...] = _rms(h_ref[...], g_ref[...]).astype(BF16)

    o_ref[...] = jnp.dot(z_ref[...], w_ref[...], preferred_element_type=F32).astype(o_ref.dtype)


def rms_matmul(h, g, w, *, tm, tn, out_dtype=F32):
    t, d = h.shape
    n = w.shape[1]
    return pl.pallas_call(
        _rms_matmul_kernel,
        grid=(t // tm, n // tn),
        in_specs=[pl.BlockSpec((tm, d), lambda i, j: (i, 0)),
                  pl.BlockSpec((1, d), lambda i, j: (0, 0)),
                  pl.BlockSpec((d, tn), lambda i, j: (0, j))],
        out_specs=pl.BlockSpec((tm, tn), lambda i, j: (i, j)),
        out_shape=jax.ShapeDtypeStruct((t, n), out_dtype),
        scratch_shapes=[pltpu.VMEM((tm, d), BF16)],
        compiler_params=_params("parallel", "arbitrary"),
        name="rms_matmul",
    )(h, g.reshape(1, d), w)


def _ffn_kernel(h_ref, g_ref, wg_ref, wu_ref, wd_ref, o_ref, z_ref, acc_ref):
    j = pl.program_id(1)

    @pl.when(j == 0)
    def _():
        z_ref[...] = _rms(h_ref[...], g_ref[...]).astype(BF16)
        acc_ref[...] = jnp.zeros_like(acc_ref)

    z = z_ref[...]
    a = jnp.dot(z, wg_ref[...].astype(BF16), preferred_element_type=F32)
    b = jnp.dot(z, wu_ref[...].astype(BF16), preferred_element_type=F32)
    t = (a * jax.nn.sigmoid(a) * b).astype(BF16)
    acc_ref[...] += jnp.dot(t, wd_ref[...].astype(BF16), preferred_element_type=F32)

    @pl.when(j == pl.num_programs(1) - 1)
    def _():
        o_ref[...] = h_ref[...] + 0.5 * acc_ref[...]


def ffn_half_step(h, g, wg, wu, wd, layer, *, tm=FFN_ROWS, tf=FFN_COLS):
    t, d = h.shape
    f = wg.shape[2]
    return pl.pallas_call(
        _ffn_kernel,
        grid=(t // tm, f // tf),
        in_specs=[pl.BlockSpec((tm, d), lambda i, j: (i, 0)),
                  pl.BlockSpec((None, 1, d), lambda i, j: (layer, 0, 0)),
                  pl.BlockSpec((None, d, tf), lambda i, j: (layer, 0, j)),
                  pl.BlockSpec((None, d, tf), lambda i, j: (layer, 0, j)),
                  pl.BlockSpec((None, tf, d), lambda i, j: (layer, j, 0))],
        out_specs=pl.BlockSpec((tm, d), lambda i, j: (i, 0)),
        out_shape=jax.ShapeDtypeStruct((t, d), F32),
        scratch_shapes=[pltpu.VMEM((tm, d), BF16), pltpu.VMEM((tm, d), F32)],
        compiler_params=_params("parallel", "arbitrary"),
        name="ffn_half_step",
    )(h, g.reshape(g.shape[0], 1, d), wg, wu, wd)


def _rope_partner(x):
    half = MLA_ROPE // 2
    zeros = jnp.zeros_like(x[..., :MLA_NOPE])
    return jnp.concatenate([zeros, x[..., MLA_NOPE + half:MLA_QK], x[..., MLA_NOPE:MLA_NOPE + half],
                            jnp.zeros_like(x[..., MLA_QK:])], axis=-1)


def _mla_prep_kernel(c_ref, cos_ref, sin_ref, qa_ref, kva_ref, wuq_ref, wuk_ref, wuvt_ref, gains_ref,
                     q_ref, k_ref, vt_ref):
    c = c_ref[...].astype(F32)
    hw = MLA_HEADS * LANES
    zq = _rms(c[:, :MLA_Q_RANK], qa_ref[...]).astype(BF16)
    zkv = _rms(c[:, MLA_Q_RANK:MLA_Q_RANK + MLA_KV_RANK], kva_ref[...]).astype(BF16)
    kpe = c[:, MLA_Q_RANK + MLA_KV_RANK:]
    q = jnp.dot(zq, wuq_ref[...], preferred_element_type=F32)
    kn = jnp.dot(zkv, wuk_ref[...], preferred_element_type=F32)
    for j in range(vt_ref.shape[0]):
        rows = slice(j * ATTN_BLOCK, (j + 1) * ATTN_BLOCK)
        vt_ref[j] = _dot_nt(wuvt_ref[...], zkv[rows]).astype(BF16)
    cos_full = cos_ref[...]
    sin_signed = sin_ref[...]
    q_cos = gains_ref[0:1, :] * cos_full
    q_sin = gains_ref[1:2, :] * sin_signed
    k_cos = gains_ref[2:3, :] * cos_full
    lane = lax.broadcasted_iota(jnp.int32, kpe.shape, 1)
    kpe_partner = jnp.where(lane < MLA_NOPE + MLA_ROPE // 2,
                            pltpu.roll(kpe, LANES - MLA_ROPE // 2, axis=1),
                            pltpu.roll(kpe, MLA_ROPE // 2, axis=1))
    k_rot = kpe_partner * (gains_ref[3:4, :] * sin_signed)
    scale = MLA_QK ** -0.5 * LOG2E
    inv_n = 1.0 / MLA_QK
    for h in range(MLA_HEADS):
        sl = slice(h * LANES, (h + 1) * LANES)
        x = q[:, sl]
        r = lax.rsqrt(jnp.sum(x * x, axis=-1, keepdims=True) * inv_n + EPS) * scale
        q_ref[:, sl] = ((x * q_cos + q[:, hw + h * LANES:hw + (h + 1) * LANES] * q_sin) * r).astype(BF16)
        y = kn[:, sl] + kpe
        r = lax.rsqrt(jnp.sum(y * y, axis=-1, keepdims=True) * inv_n + EPS)
        k_ref[:, sl] = ((y * k_cos + k_rot) * r).astype(BF16)


def mla_prep(c, cos_full, sin_signed, qa, kva, wuq, wuk, wuvt, gains, *, tm=MLA_PREP_ROWS):
    t = c.shape[0]
    hw = MLA_HEADS * LANES
    vw = MLA_HEADS * MLA_V
    row = lambda i: (i, 0)
    fix = lambda i: (0, 0)
    return pl.pallas_call(
        _mla_prep_kernel,
        grid=(t // tm,),
        in_specs=[pl.BlockSpec((tm, PROJ_SG - PROJ_MLA), lambda i: (i, PROJ_MLA // (PROJ_SG - PROJ_MLA))),
                  pl.BlockSpec((tm, LANES), row),
                  pl.BlockSpec((tm, LANES), row),
                  pl.BlockSpec((1, MLA_Q_RANK), fix),
                  pl.BlockSpec((1, MLA_KV_RANK), fix),
                  pl.BlockSpec(wuq.shape, fix),
                  pl.BlockSpec(wuk.shape, fix),
                  pl.BlockSpec(wuvt.shape, fix),
                  pl.BlockSpec(gains.shape, fix)],
        out_specs=[pl.BlockSpec((tm, hw), row),
                   pl.BlockSpec((tm, hw), row),
                   pl.BlockSpec((tm // ATTN_BLOCK, vw, ATTN_BLOCK), lambda i: (i, 0, 0))],
        out_shape=[jax.ShapeDtypeStruct((t, hw), BF16),
                   jax.ShapeDtypeStruct((t, hw), BF16),
                   jax.ShapeDtypeStruct((t // ATTN_BLOCK, vw, ATTN_BLOCK), BF16)],
        compiler_params=_params("parallel"),
        name="mla_prep",
    )(c, cos_full, sin_signed, qa, kva, wuq, wuk, wuvt, gains)


def _attn_kernel(q_ref, k_ref, vt_ref, o_ref, st_scr, p_scr, acc_scr):
    i = pl.program_id(2)
    blk = ATTN_BLOCK
    heads = range(q_ref.shape[1] // LANES)
    q = q_ref[...]
    krow = lax.broadcasted_iota(jnp.int32, (blk, blk), 0)
    qcol = lax.broadcasted_iota(jnp.int32, (blk, blk), 1)
    ones_rows = jnp.ones((SUM_ROWS, blk), BF16)

    def put_scores(j, slot):
        ks = k_ref[pl.ds(pl.multiple_of(j * blk, blk), blk), :]
        for hh in heads:
            st_scr[slot, hh] = _dot_nt(ks[:, hh * LANES:(hh + 1) * LANES], q[:, hh * LANES:(hh + 1) * LANES])

    def weighted_values(j, slot):
        vts = vt_ref[j]
        return [jnp.dot(jnp.concatenate([vts[hh * MLA_V:(hh + 1) * MLA_V], ones_rows], axis=0), p_scr[slot, hh],
                        preferred_element_type=F32) for hh in heads]

    def step(j, slot, alphas_prev, ms, masked, prefetch):
        if prefetch:
            put_scores(j + 1, 1 - slot)
        pv = weighted_values(jnp.maximum(j - 1, 0), 1 - slot)
        sts = [st_scr[slot, hh] for hh in heads]
        if masked:
            keep = krow + (j - i) * blk <= qcol
            sts = [jnp.where(keep, st, -jnp.inf) for st in sts]
        m_new = tuple(jnp.maximum(ms[hh], jnp.max(sts[hh], axis=0, keepdims=True)) for hh in heads)
        for hh in heads:
            p_scr[slot, hh] = jnp.exp2(sts[hh] - m_new[hh]).astype(BF16)
            acc_scr[hh] = alphas_prev[hh] * acc_scr[hh] + pv[hh]
        return tuple(jnp.exp2(ms[hh] - m_new[hh]) for hh in heads), m_new

    def pair(t, carry, masked):
        alphas, ms = step(2 * t, 0, *carry, masked, True)
        return step(2 * t + 1, 1, alphas, ms, masked, not masked)

    p_scr[1] = jnp.zeros_like(p_scr[1])
    acc_scr[...] = jnp.zeros_like(acc_scr)
    put_scores(0, 0)
    init = (tuple(jnp.ones((1, blk), F32) for _ in heads), tuple(jnp.full((1, blk), -jnp.inf, F32) for _ in heads))
    carry = lax.fori_loop(0, i // 2, lambda t, c: pair(t, c, False), init)
    alphas, _ = pair(i // 2, carry, True)
    pv = weighted_values(2 * (i // 2) + 1, 1)
    outs = []
    for hh in heads:
        acc = alphas[hh] * acc_scr[hh] + pv[hh]
        outs.append(acc[:MLA_V] / acc[MLA_V:MLA_V + 1])
    o_ref[...] = jnp.concatenate(outs, axis=0).T.astype(o_ref.dtype)


def mla_attention(q, k, vt, *, batch, seq):
    t = q.shape[0]
    nq = seq // ATTN_BLOCK
    return pl.pallas_call(
        _attn_kernel,
        grid=(batch, MLA_HEADS // ATTN_HEADS, nq),
        in_specs=[pl.BlockSpec((ATTN_BLOCK, ATTN_HEADS * LANES), lambda b, p, i: (b * nq + i, p)),
                  pl.BlockSpec((seq, ATTN_HEADS * LANES), lambda b, p, i: (b, p)),
                  pl.BlockSpec((nq, ATTN_HEADS * MLA_V, ATTN_BLOCK), lambda b, p, i: (b, p, 0))],
        out_specs=pl.BlockSpec((ATTN_BLOCK, ATTN_HEADS * MLA_V), lambda b, p, i: (b * nq + i, p)),
        out_shape=jax.ShapeDtypeStruct((t, MLA_HEADS * MLA_V), BF16),
        scratch_shapes=[pltpu.VMEM((2, ATTN_HEADS, ATTN_BLOCK, ATTN_BLOCK), F32),
                        pltpu.VMEM((2, ATTN_HEADS, ATTN_BLOCK, ATTN_BLOCK), BF16),
                        pltpu.VMEM((ATTN_HEADS, MLA_V + SUM_ROWS, ATTN_BLOCK), F32)],
        compiler_params=_params("parallel", "parallel", "arbitrary"),
        name="mla_attention",
    )(q, k, vt)


def _sgu_kernel(pu_ref, pv_ref, vn_ref, w_ref, b_ref, o_ref):
    u = jax.nn.gelu(pu_ref[...].astype(F32), approximate=True)
    v = _rms(jax.nn.gelu(pv_ref[...].astype(F32), approximate=True), vn_ref[...])
    wrow = lax.broadcasted_iota(jnp.int32, (SG_CHUNK, SG_GROUPS * SG_CHUNK), 0)
    wcol = lax.broadcasted_iota(jnp.int32, (SG_CHUNK, SG_GROUPS * SG_CHUNK), 1) % SG_CHUNK
    w = jnp.where(wcol <= wrow, w_ref[...], 0.0).astype(BF16)
    low = lax.broadcasted_iota(jnp.int32, (SG_CHUNK, LANES), 1) < SG_WIDTH // SG_GROUPS
    for c in range(u.shape[0] // SG_CHUNK):
        rows = slice(c * SG_CHUNK, (c + 1) * SG_CHUNK)
        for q in range(SG_WIDTH // LANES):
            cols = slice(q * LANES, (q + 1) * LANES)
            vq = v[rows, cols]
            halves = jnp.concatenate([jnp.where(low, vq, 0.0), jnp.where(low, 0.0, vq)], axis=0).astype(BF16)
            mixed = jnp.dot(w[:, 2 * q * SG_CHUNK:2 * (q + 1) * SG_CHUNK], halves,
                            preferred_element_type=F32) + b_ref[:, cols]
            o_ref[rows, cols] = (u[rows, cols] * mixed).astype(o_ref.dtype)


def sgu(proj, v_norm, w_cat, b_full, *, tm=SGU_ROWS):
    t = proj.shape[0]
    fix = lambda i: (0, 0)
    return pl.pallas_call(
        _sgu_kernel,
        grid=(t // tm,),
        in_specs=[pl.BlockSpec((tm, SG_WIDTH), lambda i: (i, PROJ_SG // SG_WIDTH)),
                  pl.BlockSpec((tm, SG_WIDTH), lambda i: (i, PROJ_SG // SG_WIDTH + 1)),
                  pl.BlockSpec((1, SG_WIDTH), fix),
                  pl.BlockSpec(w_cat.shape, fix),
                  pl.BlockSpec(b_full.shape, fix)],
        out_specs=pl.BlockSpec((tm, SG_WIDTH), lambda i: (i, 0)),
        out_shape=jax.ShapeDtypeStruct((t, SG_WIDTH), BF16),
        compiler_params=_params("parallel"),
        name="sgu",
    )(proj, proj, v_norm, w_cat, b_full)


def _rw_prep_kernel(*refs, tiles_per_seq, has_vres):
    if has_vres:
        (p_ref, prev_ref, mu_ref, w0_ref, w2_ref, a0_ref, a2_ref, g2_ref, kk_ref, ka_ref, rk_ref, bd_ref,
         vf_ref, v0_ref, v1_ref, v2_ref,
         r_o, lw_o, k_o, v_o, a_o, b_o, g_o, bonus_o) = refs
    else:
        (p_ref, prev_ref, mu_ref, w0_ref, w2_ref, a0_ref, a2_ref, g2_ref, kk_ref, ka_ref, rk_ref, bd_ref,
         r_o, lw_o, k_o, v_o, a_o, b_o, g_o, bonus_o) = refs
    cur = p_ref[...].astype(F32)
    tm = cur.shape[0]
    first = (pl.program_id(0) % tiles_per_seq) == 0
    last_prev = jnp.where(first, 0.0, prev_ref[PREV_ROWS - 1:PREV_ROWS, :].astype(F32))
    rolled = pltpu.roll(cur, 1, axis=0)
    top = lax.broadcasted_iota(jnp.int32, (8, cur.shape[1]), 0) == 0
    prev = jnp.concatenate([jnp.where(top, last_prev, rolled[:8]), rolled[8:]], axis=0)
    x = cur + (prev - cur) * mu_ref[...]
    w3 = 3 * RW_WIDTH
    r = x[:, :RW_WIDTH]
    k = x[:, RW_WIDTH:2 * RW_WIDTH]
    v = x[:, 2 * RW_WIDTH:w3]
    xwa = x[:, w3:w3 + LANES]
    xg = x[:, w3 + LANES:]
    u = w0_ref[...] + _dot_x3(jnp.tanh(xwa), w2_ref[...])
    lw_o[...] = -DECAY_SCALE * jax.nn.sigmoid(u)
    a = jax.nn.sigmoid(a0_ref[...] + _dot_x3(xwa, a2_ref[...]))
    g_o[...] = _dot(jax.nn.sigmoid(xg), g2_ref[...]).astype(g_o.dtype)
    if has_vres:
        mix = jax.nn.sigmoid(v0_ref[...] + _dot_x3(_dot_x3(v, v1_ref[...]), v2_ref[...]))
        v = v + (vf_ref[...].astype(F32) - v) * mix
    bd = bd_ref[...]
    kk = k * kk_ref[...]
    kk = kk * lax.rsqrt(jnp.maximum(_segsum(kk * kk, bd), 1e-24))
    k = k * (1.0 + (a - 1.0) * ka_ref[...])
    r_o[...] = r.astype(r_o.dtype)
    k_o[...] = k.astype(k_o.dtype)
    v_o[...] = v.astype(v_o.dtype)
    a_o[...] = (-kk).astype(a_o.dtype)
    b_o[...] = (kk * a).astype(b_o.dtype)
    bonus_o[...] = (_segsum(r * k * rk_ref[...], bd) * v).astype(bonus_o.dtype)


def rw_prep(proj, lp, v_first, *, seq, tm=RW_PREP_ROWS):
    t = proj.shape[0]
    has_vres = v_first is not None
    row = lambda i: (i, 0)
    fix = lambda i: (0, 0)
    vec = pl.BlockSpec((1, RW_WIDTH), fix)
    in_specs = [pl.BlockSpec((tm, RW_COLS), row),
                pl.BlockSpec((PREV_ROWS, RW_COLS), lambda i: (jnp.maximum(i * (tm // PREV_ROWS) - 1, 0), 0)),
                pl.BlockSpec((1, RW_COLS), fix),
                vec, pl.BlockSpec((LANES, RW_WIDTH), fix),
                vec, pl.BlockSpec((LANES, RW_WIDTH), fix),
                pl.BlockSpec((RW_GATE_RANK, RW_WIDTH), fix),
                vec, vec, vec, pl.BlockSpec((RW_WIDTH, RW_WIDTH), fix)]
    args = [proj, proj, lp["rw_mu"], lp["rw_w0"], lp["rw_w2"], lp["rw_a0"], lp["rw_a2"], lp["rw_g2"],
            lp["rw_k_k"], lp["rw_k_a"], lp["rw_r_k"], lp["rw_bd"]]
    if has_vres:
        in_specs += [pl.BlockSpec((tm, RW_WIDTH), row), vec,
                     pl.BlockSpec(lp["rw_v1"].shape, fix), pl.BlockSpec(lp["rw_v2"].shape, fix)]
        args += [v_first, lp["rw_v0"], lp["rw_v1"], lp["rw_v2"]]
    dtypes = [BF16, F32, BF16, BF16, BF16, BF16, BF16, BF16]
    return pl.pallas_call(
        functools.partial(_rw_prep_kernel, tiles_per_seq=seq // tm, has_vres=has_vres),
        grid=(t // tm,),
        in_specs=in_specs,
        out_specs=[pl.BlockSpec((tm, RW_WIDTH), row)] * 8,
        out_shape=[jax.ShapeDtypeStruct((t, RW_WIDTH), dt) for dt in dtypes],
        compiler_params=_params("parallel"),
        name="rw_prep",
    )(*args)


def _cumsum_rows(tri, x):
    hi = x.astype(BF16)
    lo = (x - hi.astype(F32)).astype(BF16)
    return jnp.dot(tri, hi, preferred_element_type=F32) + jnp.dot(tri, lo, preferred_element_type=F32)


def _rows_bcast(x, c, which):
    n = x.shape[0] // c
    return jnp.concatenate([jnp.broadcast_to(x[j * c + which:j * c + which + 1], (c, x.shape[1]))
                            for j in range(n)], axis=0)


def _rw_scan_kernel(r_ref, lw_ref, k_ref, v_ref, a_ref, b_ref, bonus_ref, g_ref, lng_ref, lnb_ref, y_ref, s_ref):
    c = SCAN_CHUNK
    n_tok = r_ref.shape[0]
    pairs = range(r_ref.shape[1] // LANES)
    heads = [(p, hh) for p in pairs for hh in range(2)]
    nch = n_tok // c

    @pl.when(pl.program_id(2) == 0)
    def _():
        s_ref[...] = jnp.zeros_like(s_ref)

    ri = lax.broadcasted_iota(jnp.int32, (n_tok, n_tok), 0)
    ci = lax.broadcasted_iota(jnp.int32, (n_tok, n_tok), 1)
    same = (ri // c) == (ci // c)
    incl = jnp.logical_and(same, ci <= ri)
    strict = jnp.logical_and(same, ci < ri)
    eye = jnp.where(ri == ci, 1.0, 0.0)
    tri = jnp.where(incl, 1.0, 0.0).astype(BF16)
    head0 = lax.broadcasted_iota(jnp.int32, (n_tok, LANES), 1) < RW_HEAD
    in_head = (head0, jnp.logical_not(head0))
    sr = lax.broadcasted_iota(jnp.int32, (LANES, LANES), 0)
    sc = lax.broadcasted_iota(jnp.int32, (LANES, LANES), 1)
    same_head = (sr // RW_HEAD) == (sc // RW_HEAD)
    eye_s = jnp.where(sr == sc, 1.0, 0.0).astype(BF16)

    cols = lambda ref, p: ref[:, p * LANES:(p + 1) * LANES]
    lw = [cols(lw_ref, p) for p in pairs]
    cum = [_cumsum_rows(tri, x) for x in lw]
    cum_mid = [_rows_bcast(x, c, c // 2 - 1) for x in cum]
    cum_end = [_rows_bcast(x, c, c - 1) for x in cum]
    r = [cols(r_ref, p).astype(F32) for p in pairs]
    a = [cols(a_ref, p).astype(F32) for p in pairs]
    k = [cols(k_ref, p).astype(F32) for p in pairs]
    b = [cols(b_ref, p).astype(F32) for p in pairs]
    v = [cols(v_ref, p).astype(F32) for p in pairs]
    r_abs = [r[p] * jnp.exp(cum[p]) for p in pairs]
    a_abs = [a[p] * jnp.exp(cum[p] - lw[p]) for p in pairs]
    g_end = [jnp.exp(cum_end[p] - cum[p]) for p in pairs]
    bk_end = [[jnp.concatenate([(b[p] * g_end[p])[j * c:(j + 1) * c], (k[p] * g_end[p])[j * c:(j + 1) * c]],
                               axis=0).astype(BF16) for j in range(n_tok // c)] for p in pairs]
    g_bwd = [jnp.exp(cum_mid[p] - cum[p]) for p in pairs]
    r_mid = [r[p] * jnp.exp(cum[p] - cum_mid[p]) for p in pairs]
    a_mid = [a[p] * jnp.exp(cum[p] - lw[p] - cum_mid[p]) for p in pairs]
    rhs = [jnp.concatenate([b[p] * g_bwd[p], k[p] * g_bwd[p]], axis=0).astype(BF16) for p in pairs]
    a_h = [jnp.where(in_head[hh], a_mid[p], 0.0).astype(BF16) for p, hh in heads]
    r_h = [jnp.where(in_head[hh], r_mid[p], 0.0).astype(BF16) for p, hh in heads]
    pm = [[_dot_nt(jnp.concatenate([a_h[n][j * c:(j + 1) * c], r_h[n][j * c:(j + 1) * c]], axis=0),
                   jnp.concatenate([rhs[p][j * c:(j + 1) * c], rhs[p][n_tok + j * c:n_tok + (j + 1) * c]], axis=0))
           for j in range(nch)] for n, (p, hh) in enumerate(heads)]
    strict_c = strict[:c, :c]
    incl_c = incl[:c, :c]
    zero_cc = jnp.zeros((c, c), F32)

    def block_diag(blocks):
        return jnp.concatenate([jnp.concatenate([blocks[j] if jj == j else zero_cc for jj in range(nch)], axis=1)
                                for j in range(nch)], axis=0)

    lab = [block_diag([jnp.where(strict_c, x[:c, :c], 0.0) for x in xs]) for xs in pm]
    lak = [block_diag([jnp.where(strict_c, x[:c, c:], 0.0) for x in xs]).astype(BF16) for xs in pm]
    mrb = [block_diag([jnp.where(incl_c, x[c:, :c], 0.0) for x in xs]).astype(BF16) for xs in pm]
    mrk = [block_diag([jnp.where(incl_c, x[c:, c:], 0.0) for x in xs]).astype(BF16) for xs in pm]
    vh = [jnp.where(in_head[hh], v[p], 0.0).astype(BF16) for p, hh in heads]
    inv = [eye + x for x in lab]
    lp = lab
    q = 2
    while q < c:
        lp = [_dot(x, x) for x in lp]
        inv = [t + _dot(t, x) for t, x in zip(inv, lp)]
        q *= 2
    xv = [jnp.dot(lak[n], vh[n], preferred_element_type=F32) for n in range(len(heads))]
    tw = [_dot(inv[n], jnp.concatenate([jnp.where(in_head[hh], a_abs[p], 0.0), xv[n]], axis=1))
          for n, (p, hh) in enumerate(heads)]
    y2h = [jnp.dot(mrk[n], vh[n], preferred_element_type=F32) for n in range(len(heads))]
    w1 = [tw[2 * p][:, :LANES] + tw[2 * p + 1][:, :LANES] for p in pairs]
    w2 = [tw[2 * p][:, LANES:] + tw[2 * p + 1][:, LANES:] for p in pairs]
    w12 = [jnp.concatenate([w1[p], w2[p]], axis=1).astype(BF16) for p in pairs]
    lhs_xy = [jnp.concatenate([jnp.concatenate([w1[p][j * c:(j + 1) * c], r_abs[p][j * c:(j + 1) * c]], axis=0)
                               for j in range(n_tok // c)], axis=0).astype(BF16) for p in pairs]
    vb = [x.astype(BF16) for x in v]
    zeros_c = jnp.zeros((c, LANES), BF16)
    u_rows = [[] for _ in pairs]
    y0_rows = [[] for _ in pairs]
    for j in range(n_tok // c):
        rows = slice(j * c, (j + 1) * c)
        s0 = [s_ref[p] for p in pairs]
        s0b = [x.astype(BF16) for x in s0]
        xy = [_dot_nt(lhs_xy[p][2 * j * c:2 * (j + 1) * c], s0b[p]) for p in pairs]
        uvt = [_dot_nt(jnp.concatenate([s0b[p], eye_s], axis=1),
                       jnp.concatenate([w12[p][rows], jnp.concatenate([zeros_c, vb[p][rows]], axis=1)], axis=0))
               for p in pairs]
        upd = [jnp.dot(uvt[p].astype(BF16), bk_end[p][j], preferred_element_type=F32) for p in pairs]
        for p in pairs:
            u_rows[p].append(xy[p][:c] + w2[p][rows])
            y0_rows[p].append(xy[p][c:])
            s_ref[p] = s0[p] * jnp.exp(cum_end[p][j * c:j * c + 1, :]) + jnp.where(same_head, upd[p], 0.0)
    ub = [jnp.concatenate(u_rows[p], axis=0).astype(BF16) for p in pairs]
    y1 = [jnp.dot(mrb[n], ub[p], preferred_element_type=F32) for n, (p, hh) in enumerate(heads)]
    half_sum = lambda x: jnp.where(head0, jnp.sum(jnp.where(head0, x, 0.0), axis=-1, keepdims=True),
                                   jnp.sum(jnp.where(head0, 0.0, x), axis=-1, keepdims=True))
    for p in pairs:
        sl = slice(p * LANES, (p + 1) * LANES)
        y = (jnp.concatenate(y0_rows[p], axis=0) + y2h[2 * p] + y2h[2 * p + 1]
             + jnp.where(head0, y1[2 * p], y1[2 * p + 1]))
        d = y - half_sum(y) * (1.0 / RW_HEAD)
        var = half_sum(d * d) * (1.0 / RW_HEAD)
        yn = d * lax.rsqrt(var + RW_GN_EPS) * lng_ref[:, sl] + lnb_ref[:, sl]
        y_ref[:, sl] = ((yn + bonus_ref[:, sl].astype(F32)) * g_ref[:, sl].astype(F32)).astype(y_ref.dtype)


def rw_scan(r, lw, k, v, a, b, bonus, g, ln_g, ln_b, *, batch, seq):
    t = r.shape[0]
    ns = seq // SCAN_STEP
    width = SCAN_PAIRS * LANES
    spec = pl.BlockSpec((SCAN_STEP, width), lambda bi, p, i: (bi * ns + i, p))
    vec = pl.BlockSpec((1, width), lambda bi, p, i: (0, p))
    return pl.pallas_call(
        _rw_scan_kernel,
        grid=(batch, RW_WIDTH // width, ns),
        in_specs=[spec] * 8 + [vec] * 2,
        out_specs=spec,
        out_shape=jax.ShapeDtypeStruct((t, RW_WIDTH), BF16),
        scratch_shapes=[pltpu.VMEM((SCAN_PAIRS, LANES, LANES), F32)],
        compiler_params=_params("parallel", "parallel", "arbitrary"),
        name="rw_scan",
    )(r, lw, k, v, a, b, bonus, g, ln_g, ln_b)


def _merge_kernel(h_ref, ya_ref, yb_ref, yc_ref, ga_ref, gb_ref, gc_ref, wa_ref, wb_ref, wc_ref, wo_ref, o_ref):
    gate = lambda g_ref: jax.nn.sigmoid(g_ref[...].astype(F32))
    merged = (gate(ga_ref) * jnp.dot(ya_ref[...], wa_ref[...], preferred_element_type=F32)
              + gate(gb_ref) * jnp.dot(yb_ref[...], wb_ref[...], preferred_element_type=F32)
              + gate(gc_ref) * jnp.dot(yc_ref[...], wc_ref[...], preferred_element_type=F32))
    o_ref[...] = h_ref[...] + jnp.dot(merged.astype(BF16), wo_ref[...], preferred_element_type=F32)


def merge(h, ya, yb, yc, proj, wa, wb, wc, wo, *, tm=MERGE_ROWS):
    t, d = h.shape
    row = lambda i: (i, 0)
    fix = lambda i: (0, 0)
    gate_specs = [pl.BlockSpec((tm, d), functools.partial(lambda i, n: (i, PROJ_GATE // d + n), n=n))
                  for n in range(3)]
    return pl.pallas_call(
        _merge_kernel,
        grid=(t // tm,),
        in_specs=[pl.BlockSpec((tm, d), row),
                  pl.BlockSpec((tm, ya.shape[1]), row),
                  pl.BlockSpec((tm, yb.shape[1]), row),
                  pl.BlockSpec((tm, yc.shape[1]), row),
                  *gate_specs,
                  pl.BlockSpec(wa.shape, fix), pl.BlockSpec(wb.shape, fix),
                  pl.BlockSpec(wc.shape, fix), pl.BlockSpec(wo.shape, fix)],
        out_specs=pl.BlockSpec((tm, d), row),
        out_shape=jax.ShapeDtypeStruct((t, d), F32),
        compiler_params=_params("parallel"),
        name="merge",
    )(h, ya, yb, yc, proj, proj, proj, wa, wb, wc, wo)


def _pad_cols(w, n):
    return jnp.pad(w, ((0, 0), (0, n - w.shape[1])))


def _layer_params(i, p):
    wuq = p["mla_w_uq"][i].reshape(MLA_Q_RANK, MLA_HEADS, MLA_QK)
    wuq = jnp.pad(wuq, ((0, 0), (0, 0), (0, LANES - MLA_QK)))
    wuq = jnp.concatenate([wuq.reshape(MLA_Q_RANK, -1), _rope_partner(wuq).reshape(MLA_Q_RANK, -1)], axis=1)
    qn = _pad_cols(p["mla_q_norm"][i].reshape(1, -1), LANES)
    kn = _pad_cols(p["mla_k_norm"][i].reshape(1, -1), LANES)
    gains = jnp.concatenate([qn, _rope_partner(qn), kn, _rope_partner(kn), jnp.zeros((4, LANES), F32)], axis=0)
    wukv = p["mla_w_ukv"][i].reshape(MLA_KV_RANK, MLA_HEADS, MLA_NOPE + MLA_V)
    wuk = jnp.pad(wukv[:, :, :MLA_NOPE], ((0, 0), (0, 0), (0, LANES - MLA_NOPE)))
    wuk = wuk.reshape(MLA_KV_RANK, MLA_HEADS * LANES)
    wuv = wukv[:, :, MLA_NOPE:].reshape(MLA_KV_RANK, MLA_HEADS * MLA_V)
    head_of = jnp.arange(RW_WIDTH) // RW_HEAD
    lp = {
        "w_proj": proj_weight(p["w_in"], i),
        "qa": p["mla_q_a_norm"][i].reshape(1, -1),
        "kva": p["mla_kv_a_norm"][i].reshape(1, -1),
        "wuq": wuq.astype(BF16), "wuk": wuk.astype(BF16), "wuvt": wuv.T.astype(BF16),
        "mla_gains": gains,
        "sg_vn": p["sg_v_norm"][i].reshape(1, -1),
        "sg_w": p["sg_w_s"][i].transpose(1, 0, 2).reshape(SG_CHUNK, SG_GROUPS * SG_CHUNK),
        "sg_b": jnp.repeat(p["sg_b_s"][i].T, SG_WIDTH // SG_GROUPS, axis=1),
        "rw_mu": p["rw_mu"][i].reshape(1, -1),
        "rw_w0": p["rw_w0"][i].reshape(1, -1),
        "rw_w2": jnp.pad(p["rw_w2"][i], ((0, RW_A_RANK), (0, 0))),
        "rw_a0": p["rw_a0"][i].reshape(1, -1),
        "rw_a2": jnp.pad(p["rw_a2"][i], ((RW_DECAY_RANK, 0), (0, 0))),
        "rw_g2": p["rw_g2"][i].astype(BF16),
        "rw_k_k": p["rw_k_k"][i].reshape(1, -1),
        "rw_k_a": p["rw_k_a"][i].reshape(1, -1),
        "rw_r_k": p["rw_r_k"][i].reshape(1, -1),
        "rw_ln_g": p["rw_ln_g"][i].reshape(1, -1),
        "rw_ln_b": p["rw_ln_b"][i].reshape(1, -1),
        "rw_bd": (head_of[:, None] == head_of[None, :]).astype(BF16),
        "w_out_mla": p["w_out_mla"][i].astype(BF16),
        "w_out_sg": p["w_out_sg"][i].astype(BF16),
        "w_out_rw": p["w_out_rw"][i].astype(BF16),
        "w_o": p["w_o"][i].astype(BF16),
    }
    if i > 0:
        lp["rw_v0"] = p["rw_v0"][i - 1].reshape(1, -1)
        lp["rw_v1"] = _pad_cols(p["rw_v1"][i - 1], LANES)
        lp["rw_v2"] = jnp.pad(p["rw_v2"][i - 1], ((0, LANES - p["rw_v2"].shape[1]), (0, 0)))
    return lp


def _rope_tables(positions):
    inv_freq = ROPE_BASE ** (-jnp.arange(0, MLA_ROPE, 2, dtype=F32) / MLA_ROPE)
    ang = positions.astype(F32).reshape(-1, 1) * inv_freq
    cos, sin = jnp.cos(ang), jnp.sin(ang)
    t = ang.shape[0]
    cos_full = jnp.concatenate([jnp.ones((t, MLA_NOPE), F32), cos, cos,
                                jnp.ones((t, LANES - MLA_QK), F32)], axis=1)
    sin_signed = jnp.concatenate([jnp.zeros((t, MLA_NOPE), F32), -sin, sin,
                                  jnp.zeros((t, LANES - MLA_QK), F32)], axis=1)
    return cos_full, sin_signed


def kernel(x, positions, ffn1_norm, ffn1_w_gate, ffn1_w_up, ffn1_w_down, mix_norm, w_in, mla_q_a_norm, mla_w_uq, mla_kv_a_norm, mla_w_ukv, mla_q_norm, mla_k_norm, sg_v_norm, sg_w_s, sg_b_s, rw_mu, rw_w0, rw_w2, rw_a0, rw_a2, rw_g2, rw_k_k, rw_k_a, rw_r_k, rw_ln_g, rw_ln_b, rw_v0, rw_v1, rw_v2, w_out_mla, w_out_sg, w_out_rw, w_o, ffn2_norm, ffn2_w_gate, ffn2_w_up, ffn2_w_down):
    params = dict(ffn1_norm=ffn1_norm, ffn1_w_gate=ffn1_w_gate, ffn1_w_up=ffn1_w_up, ffn1_w_down=ffn1_w_down,
                  mix_norm=mix_norm, w_in=w_in, mla_q_a_norm=mla_q_a_norm, mla_w_uq=mla_w_uq,
                  mla_kv_a_norm=mla_kv_a_norm, mla_w_ukv=mla_w_ukv, mla_q_norm=mla_q_norm, mla_k_norm=mla_k_norm,
                  sg_v_norm=sg_v_norm, sg_w_s=sg_w_s, sg_b_s=sg_b_s, rw_mu=rw_mu, rw_w0=rw_w0, rw_w2=rw_w2,
                  rw_a0=rw_a0, rw_a2=rw_a2, rw_g2=rw_g2, rw_k_k=rw_k_k, rw_k_a=rw_k_a, rw_r_k=rw_r_k,
                  rw_ln_g=rw_ln_g, rw_ln_b=rw_ln_b, rw_v0=rw_v0, rw_v1=rw_v1, rw_v2=rw_v2,
                  w_out_mla=w_out_mla, w_out_sg=w_out_sg, w_out_rw=w_out_rw, w_o=w_o,
                  ffn2_norm=ffn2_norm, ffn2_w_gate=ffn2_w_gate, ffn2_w_up=ffn2_w_up, ffn2_w_down=ffn2_w_down)
    batch, seq, d = x.shape
    depth = w_in.shape[0]
    cos_full, sin_signed = _rope_tables(positions)
    h = x.reshape(batch * seq, d)
    v_first = None
    for i in range(depth):
        lp = _layer_params(i, params)
        h = ffn_half_step(h, ffn1_norm, ffn1_w_gate, ffn1_w_up, ffn1_w_down, i)
        proj = rms_matmul(h, mix_norm[i], lp["w_proj"], tm=PROJ_ROWS, tn=PROJ_TILE, out_dtype=BF16)
        q, k, vt = mla_prep(proj, cos_full, sin_signed, lp["qa"], lp["kva"], lp["wuq"], lp["wuk"], lp["wuvt"],
                           lp["mla_gains"])
        y_a = mla_attention(q, k, vt, batch=batch, seq=seq)
        y_b = sgu(proj, lp["sg_vn"], lp["sg_w"], lp["sg_b"])
        r, lw, kr, vr, a, b, g, bonus = rw_prep(proj, lp, v_first, seq=seq)
        if v_first is None:
            v_first = vr
        y_c = rw_scan(r, lw, kr, vr, a, b, bonus, g, lp["rw_ln_g"], lp["rw_ln_b"], batch=batch, seq=seq)
        h = merge(h, y_a, y_b, y_c, proj, lp["w_out_mla"], lp["w_out_sg"], lp["w_out_rw"], lp["w_o"])
        h = ffn_half_step(h, ffn2_norm, ffn2_w_gate, ffn2_w_up, ffn2_w_down, i)
    return h.reshape(batch, seq, d)
```

```python
import functools

import jax
import jax.numpy as jnp
from jax import lax
from jax.experimental import pallas as pl
from jax.experimental.pallas import tpu as pltpu

F32 = jnp.float32
BF16 = jnp.bfloat16

LANES = 128
EPS = 1e-6
RW_GN_EPS = 64e-5
ROPE_BASE = 10000.0

MLA_HEADS = 8
MLA_NOPE = 64
MLA_ROPE = 32
MLA_QK = MLA_NOPE + MLA_ROPE
MLA_V = 64
MLA_Q_RANK = 256
MLA_KV_RANK = 128
SG_WIDTH = 512
SG_GROUPS = 8
SG_CHUNK = 128
RW_HEADS = 8
RW_HEAD = 64
RW_WIDTH = RW_HEADS * RW_HEAD
RW_DECAY_RANK = 64
RW_A_RANK = 64
RW_GATE_RANK = 128
RW_COLS = 3 * RW_WIDTH + RW_DECAY_RANK + RW_A_RANK + RW_GATE_RANK

V7X_VMEM_BYTES = 64 * 1024 * 1024
VMEM_LIMIT = V7X_VMEM_BYTES * 7 // 8
BF16_SUBLANES = 16

FFN_ROWS = 2048
FFN_COLS = 256
PROJ_ROWS = 1024
PROJ_TILE = 3328
MLA_PREP_ROWS = 1024
SGU_ROWS = 1024
RW_PREP_ROWS = 512
MERGE_ROWS = 1024
SCAN_CHUNK = 128
SCAN_STEP = 256
SCAN_PAIRS = 4
ATTN_BLOCK = 256
ATTN_HEADS = 8
PROJ_GATE = 2048
PROJ_MLA = PROJ_GATE + 3 * 1024
PROJ_SG = PROJ_MLA + 512
PROJ_COLS = PROJ_SG + 2 * SG_WIDTH
PREV_ROWS = BF16_SUBLANES
SUM_ROWS = BF16_SUBLANES
LOG2E = 1.4426950408889634
DECAY_SCALE = 0.6065306597126334


def _params(*sem):
    return pltpu.CompilerParams(dimension_semantics=sem, vmem_limit_bytes=VMEM_LIMIT)


def _dot(a, b):
    return jnp.dot(a.astype(BF16), b.astype(BF16), preferred_element_type=F32)


def _dot_x3(a, b):
    ah = a.astype(BF16)
    al = (a - ah.astype(F32)).astype(BF16)
    bh = b.astype(BF16)
    bl = (b - bh.astype(F32)).astype(BF16)
    f = lambda x, y: jnp.dot(x, y, preferred_element_type=F32)
    return f(ah, bh) + f(al, bh) + f(ah, bl)


def _dot_nt(a, b):
    return lax.dot_general(a, b, (((1,), (1,)), ((), ())), preferred_element_type=F32)


def _rms(x, g, n=None):
    n = x.shape[-1] if n is None else n
    ms = jnp.sum(x * x, axis=-1, keepdims=True) * (1.0 / n)
    return x * lax.rsqrt(ms + EPS) * g


def _segsum(x, bd):
    hi = x.astype(BF16)
    lo = (x - hi.astype(F32)).astype(BF16)
    return (jnp.dot(hi, bd, preferred_element_type=F32) + jnp.dot(lo, bd, preferred_element_type=F32))


def _proj_weight_kernel(wt_ref, o_ref):
    x = wt_ref[...]
    zeros = lambda n: jnp.zeros((n, x.shape[1]), x.dtype)
    o_kv = MLA_Q_RANK + MLA_KV_RANK
    o_sg = o_kv + MLA_ROPE
    o_rw = o_sg + 2 * SG_WIDTH
    o_gate = o_rw + RW_COLS
    rows = jnp.concatenate(
        [x[o_rw:o_gate], zeros(PROJ_GATE - RW_COLS), x[o_gate:],
         x[:o_kv], zeros(MLA_NOPE), x[o_kv:o_sg], zeros(LANES - MLA_QK),
         x[o_sg:o_rw]], axis=0)
    o_ref[...] = rows.T.astype(o_ref.dtype)


def proj_weight(w_in, layer, *, tr=LANES):
    wt = jnp.swapaxes(w_in, 1, 2)
    _, n, d = wt.shape
    return pl.pallas_call(
        _proj_weight_kernel,
        grid=(d // tr,),
        in_specs=[pl.BlockSpec((None, n, tr), lambda i: (layer, 0, i))],
        out_specs=pl.BlockSpec((tr, PROJ_COLS), lambda i: (i, 0)),
        out_shape=jax.ShapeDtypeStruct((d, PROJ_COLS), BF16),
        compiler_params=_params("parallel"),
        name="proj_weight",
    )(wt)


def _rms_matmul_kernel(h_ref, g_ref, w_ref, o_ref, z_ref):
    @pl.when(pl.program_id(1) == 0)
    def _():
        z_ref[...] = _rms(h_ref[...], g_ref[...]).astype(BF16)

    o_ref[...] = jnp.dot(z_ref[...], w_ref[...], preferred_element_type=F32).astype(o_ref.dtype)


def rms_matmul(h, g, w, *, tm, tn, out_dtype=F32):
    t, d = h.shape
    n = w.shape[1]
    return pl.pallas_call(
        _rms_matmul_kernel,
        grid=(t // tm, n // tn),
        in_specs=[pl.BlockSpec((tm, d), lambda i, j: (i, 0)),
                  pl.BlockSpec((1, d), lambda i, j: (0, 0)),
                  pl.BlockSpec((d, tn), lambda i, j: (0, j))],
        out_specs=pl.BlockSpec((tm, tn), lambda i, j: (i, j)),
        out_shape=jax.ShapeDtypeStruct((t, n), out_dtype),
        scratch_shapes=[pltpu.VMEM((tm, d), BF16)],
        compiler_params=_params("parallel", "arbitrary"),
        name="rms_matmul",
    )(h, g.reshape(1, d), w)


def _ffn_kernel(h_ref, g_ref, wg_ref, wu_ref, wd_ref, o_ref, z_ref, acc_ref):
    j = pl.program_id(1)

    @pl.when(j == 0)
    def _():
        z_ref[...] = _rms(h_ref[...], g_ref[...]).astype(BF16)
        acc_ref[...] = jnp.zeros_like(acc_ref)

    z = z_ref[...]
    a = jnp.dot(z, wg_ref[...].astype(BF16), preferred_element_type=F32)
    b = jnp.dot(z, wu_ref[...].astype(BF16), preferred_element_type=F32)
    t = (a * jax.nn.sigmoid(a) * b).astype(BF16)
    acc_ref[...] += jnp.dot(t, wd_ref[...].astype(BF16), preferred_element_type=F32)

    @pl.when(j == pl.num_programs(1) - 1)
    def _():
        o_ref[...] = h_ref[...] + 0.5 * acc_ref[...]


def ffn_half_step(h, g, wg, wu, wd, layer, *, tm=FFN_ROWS, tf=FFN_COLS):
    t, d = h.shape
    f = wg.shape[2]
    return pl.pallas_call(
        _ffn_kernel,
        grid=(t // tm, f // tf),
        in_specs=[pl.BlockSpec((tm, d), lambda i, j: (i, 0)),
                  pl.BlockSpec((None, 1, d), lambda i, j: (layer, 0, 0)),
                  pl.BlockSpec((None, d, tf), lambda i, j: (layer, 0, j)),
                  pl.BlockSpec((None, d, tf), lambda i, j: (layer, 0, j)),
                  pl.BlockSpec((None, tf, d), lambda i, j: (layer, j, 0))],
        out_specs=pl.BlockSpec((tm, d), lambda i, j: (i, 0)),
        out_shape=jax.ShapeDtypeStruct((t, d), F32),
        scratch_shapes=[pltpu.VMEM((tm, d), BF16), pltpu.VMEM((tm, d), F32)],
        compiler_params=_params("parallel", "arbitrary"),
        name="ffn_half_step",
    )(h, g.reshape(g.shape[0], 1, d), wg, wu, wd)


def _rope_partner(x):
    half = MLA_ROPE // 2
    zeros = jnp.zeros_like(x[..., :MLA_NOPE])
    return jnp.concatenate([zeros, x[..., MLA_NOPE + half:MLA_QK], x[..., MLA_NOPE:MLA_NOPE + half],
                            jnp.zeros_like(x[..., MLA_QK:])], axis=-1)


def _mla_prep_kernel(c_ref, cos_ref, sin_ref, qa_ref, kva_ref, wuq_ref, wuk_ref, wuvt_ref, gains_ref,
                     q_ref, k_ref, vt_ref):
    c = c_ref[...].astype(F32)
    hw = MLA_HEADS * LANES
    zq = _rms(c[:, :MLA_Q_RANK], qa_ref[...]).astype(BF16)
    zkv = _rms(c[:, MLA_Q_RANK:MLA_Q_RANK + MLA_KV_RANK], kva_ref[...]).astype(BF16)
    kpe = c[:, MLA_Q_RANK + MLA_KV_RANK:]
    q = jnp.dot(zq, wuq_ref[...], preferred_element_type=F32)
    kn = jnp.dot(zkv, wuk_ref[...], preferred_element_type=F32)
    for j in range(vt_ref.shape[0]):
        rows = slice(j * ATTN_BLOCK, (j + 1) * ATTN_BLOCK)
        vt_ref[j] = _dot_nt(wuvt_ref[...], zkv[rows]).astype(BF16)
    cos, sin = cos_ref[...], sin_ref[...]
    ones, zeros = jnp.ones_like(c[:, :MLA_NOPE]), jnp.zeros_like(c[:, :MLA_NOPE])
    cos_full = jnp.concatenate([ones, cos, cos, ones[:, :LANES - MLA_QK]], axis=1)
    sin_signed = jnp.concatenate([zeros, -sin, sin, zeros[:, :LANES - MLA_QK]], axis=1)
    q_cos = gains_ref[0:1, :] * cos_full
    q_sin = gains_ref[1:2, :] * sin_signed
    k_cos = gains_ref[2:3, :] * cos_full
    lane = lax.broadcasted_iota(jnp.int32, kpe.shape, 1)
    kpe_partner = jnp.where(lane < MLA_NOPE + MLA_ROPE // 2,
                            pltpu.roll(kpe, LANES - MLA_ROPE // 2, axis=1),
                            pltpu.roll(kpe, MLA_ROPE // 2, axis=1))
    k_rot = kpe_partner * (gains_ref[3:4, :] * sin_signed)
    scale = MLA_QK ** -0.5 * LOG2E
    inv_n = 1.0 / MLA_QK
    for h in range(MLA_HEADS):
        sl = slice(h * LANES, (h + 1) * LANES)
        x = q[:, sl]
        r = lax.rsqrt(jnp.sum(x * x, axis=-1, keepdims=True) * inv_n + EPS) * scale
        q_ref[:, sl] = ((x * q_cos + q[:, hw + h * LANES:hw + (h + 1) * LANES] * q_sin) * r).astype(BF16)
        y = kn[:, sl] + kpe
        r = lax.rsqrt(jnp.sum(y * y, axis=-1, keepdims=True) * inv_n + EPS)
        k_ref[:, sl] = ((y * k_cos + k_rot) * r).astype(BF16)


def mla_prep(c, cos, sin, qa, kva, wuq, wuk, wuvt, gains, *, tm=MLA_PREP_ROWS):
    t = c.shape[0]
    hw = MLA_HEADS * LANES
    vw = MLA_HEADS * MLA_V
    row = lambda i: (i, 0)
    fix = lambda i: (0, 0)
    return pl.pallas_call(
        _mla_prep_kernel,
        grid=(t // tm,),
        in_specs=[pl.BlockSpec((tm, PROJ_SG - PROJ_MLA), lambda i: (i, PROJ_MLA // (PROJ_SG - PROJ_MLA))),
                  pl.BlockSpec((tm, MLA_ROPE // 2), row),
                  pl.BlockSpec((tm, MLA_ROPE // 2), row),
                  pl.BlockSpec((1, MLA_Q_RANK), fix),
                  pl.BlockSpec((1, MLA_KV_RANK), fix),
                  pl.BlockSpec(wuq.shape, fix),
                  pl.BlockSpec(wuk.shape, fix),
                  pl.BlockSpec(wuvt.shape, fix),
                  pl.BlockSpec(gains.shape, fix)],
        out_specs=[pl.BlockSpec((tm, hw), row),
                   pl.BlockSpec((tm, hw), row),
                   pl.BlockSpec((tm // ATTN_BLOCK, vw, ATTN_BLOCK), lambda i: (i, 0, 0))],
        out_shape=[jax.ShapeDtypeStruct((t, hw), BF16),
                   jax.ShapeDtypeStruct((t, hw), BF16),
                   jax.ShapeDtypeStruct((t // ATTN_BLOCK, vw, ATTN_BLOCK), BF16)],
        compiler_params=_params("parallel"),
        name="mla_prep",
    )(c, cos, sin, qa, kva, wuq, wuk, wuvt, gains)


def _attn_kernel(q_ref, k_ref, vt_ref, o_ref, st_scr, p_scr, acc_scr):
    i = pl.program_id(2)
    blk = ATTN_BLOCK
    heads = range(q_ref.shape[1] // LANES)
    q = q_ref[...]
    krow = lax.broadcasted_iota(jnp.int32, (blk, blk), 0)
    qcol = lax.broadcasted_iota(jnp.int32, (blk, blk), 1)
    ones_rows = jnp.ones((SUM_ROWS, blk), BF16)

    def put_scores(j, slot):
        ks = k_ref[pl.ds(pl.multiple_of(j * blk, blk), blk), :]
        for hh in heads:
            st_scr[slot, hh] = _dot_nt(ks[:, hh * LANES:(hh + 1) * LANES], q[:, hh * LANES:(hh + 1) * LANES])

    def weighted_values(j, slot):
        vts = vt_ref[j]
        return [jnp.dot(jnp.concatenate([vts[hh * MLA_V:(hh + 1) * MLA_V], ones_rows], axis=0), p_scr[slot, hh],
                        preferred_element_type=F32) for hh in heads]

    def step(j, slot, alphas_prev, ms, masked, prefetch):
        if prefetch:
            put_scores(j + 1, 1 - slot)
        pv = weighted_values(jnp.maximum(j - 1, 0), 1 - slot)
        sts = [st_scr[slot, hh] for hh in heads]
        if masked:
            keep = krow + (j - i) * blk <= qcol
            sts = [jnp.where(keep, st, -jnp.inf) for st in sts]
        m_new = tuple(jnp.maximum(ms[hh], jnp.max(sts[hh], axis=0, keepdims=True)) for hh in heads)
        for hh in heads:
            p_scr[slot, hh] = jnp.exp2(sts[hh] - m_new[hh]).astype(BF16)
            acc_scr[hh] = alphas_prev[hh] * acc_scr[hh] + pv[hh]
        return tuple(jnp.exp2(ms[hh] - m_new[hh]) for hh in heads), m_new

    def pair(t, carry, masked):
        alphas, ms = step(2 * t, 0, *carry, masked, True)
        return step(2 * t + 1, 1, alphas, ms, masked, not masked)

    p_scr[1] = jnp.zeros_like(p_scr[1])
    acc_scr[...] = jnp.zeros_like(acc_scr)
    put_scores(0, 0)
    init = (tuple(jnp.ones((1, blk), F32) for _ in heads), tuple(jnp.full((1, blk), -jnp.inf, F32) for _ in heads))
    carry = lax.fori_loop(0, i // 2, lambda t, c: pair(t, c, False), init)
    alphas, _ = pair(i // 2, carry, True)
    pv = weighted_values(2 * (i // 2) + 1, 1)
    outs = []
    for hh in heads:
        acc = alphas[hh] * acc_scr[hh] + pv[hh]
        outs.append(acc[:MLA_V] / acc[MLA_V:MLA_V + 1])
    o_ref[...] = jnp.concatenate(outs, axis=0).T.astype(o_ref.dtype)


def mla_attention(q, k, vt, *, batch, seq):
    t = q.shape[0]
    nq = seq // ATTN_BLOCK
    return pl.pallas_call(
        _attn_kernel,
        grid=(batch, MLA_HEADS // ATTN_HEADS, nq),
        in_specs=[pl.BlockSpec((ATTN_BLOCK, ATTN_HEADS * LANES), lambda b, p, i: (b * nq + i, p)),
                  pl.BlockSpec((seq, ATTN_HEADS * LANES), lambda b, p, i: (b, p)),
                  pl.BlockSpec((nq, ATTN_HEADS * MLA_V, ATTN_BLOCK), lambda b, p, i: (b, p, 0))],
        out_specs=pl.BlockSpec((ATTN_BLOCK, ATTN_HEADS * MLA_V), lambda b, p, i: (b * nq + i, p)),
        out_shape=jax.ShapeDtypeStruct((t, MLA_HEADS * MLA_V), BF16),
        scratch_shapes=[pltpu.VMEM((2, ATTN_HEADS, ATTN_BLOCK, ATTN_BLOCK), F32),
                        pltpu.VMEM((2, ATTN_HEADS, ATTN_BLOCK, ATTN_BLOCK), BF16),
                        pltpu.VMEM((ATTN_HEADS, MLA_V + SUM_ROWS, ATTN_BLOCK), F32)],
        compiler_params=_params("parallel", "parallel", "arbitrary"),
        name="mla_attention",
    )(q, k, vt)


def _sgu_kernel(pu_ref, pv_ref, vn_ref, w_ref, b_ref, o_ref):
    u = jax.nn.gelu(pu_ref[...].astype(F32), approximate=True)
    v = _rms(jax.nn.gelu(pv_ref[...].astype(F32), approximate=True), vn_ref[...])
    wrow = lax.broadcasted_iota(jnp.int32, (SG_CHUNK, SG_GROUPS * SG_CHUNK), 0)
    wcol = lax.broadcasted_iota(jnp.int32, (SG_CHUNK, SG_GROUPS * SG_CHUNK), 1) % SG_CHUNK
    w = jnp.where(wcol <= wrow, w_ref[...], 0.0).astype(BF16)
    low = lax.broadcasted_iota(jnp.int32, (SG_CHUNK, LANES), 1) < SG_WIDTH // SG_GROUPS
    for c in range(u.shape[0] // SG_CHUNK):
        rows = slice(c * SG_CHUNK, (c + 1) * SG_CHUNK)
        for q in range(SG_WIDTH // LANES):
            cols = slice(q * LANES, (q + 1) * LANES)
            vq = v[rows, cols]
            halves = jnp.concatenate([jnp.where(low, vq, 0.0), jnp.where(low, 0.0, vq)], axis=0).astype(BF16)
            mixed = jnp.dot(w[:, 2 * q * SG_CHUNK:2 * (q + 1) * SG_CHUNK], halves,
                            preferred_element_type=F32) + b_ref[:, cols]
            o_ref[rows, cols] = (u[rows, cols] * mixed).astype(o_ref.dtype)


def sgu(proj, v_norm, w_cat, b_full, *, tm=SGU_ROWS):
    t = proj.shape[0]
    fix = lambda i: (0, 0)
    return pl.pallas_call(
        _sgu_kernel,
        grid=(t // tm,),
        in_specs=[pl.BlockSpec((tm, SG_WIDTH), lambda i: (i, PROJ_SG // SG_WIDTH)),
                  pl.BlockSpec((tm, SG_WIDTH), lambda i: (i, PROJ_SG // SG_WIDTH + 1)),
                  pl.BlockSpec((1, SG_WIDTH), fix),
                  pl.BlockSpec(w_cat.shape, fix),
                  pl.BlockSpec(b_full.shape, fix)],
        out_specs=pl.BlockSpec((tm, SG_WIDTH), lambda i: (i, 0)),
        out_shape=jax.ShapeDtypeStruct((t, SG_WIDTH), BF16),
        compiler_params=_params("parallel"),
        name="sgu",
    )(proj, proj, v_norm, w_cat, b_full)


def _rw_prep_kernel(*refs, tiles_per_seq, has_vres):
    if has_vres:
        (p_ref, prev_ref, mu_ref, w0_ref, w2_ref, a0_ref, a2_ref, g2_ref, kk_ref, ka_ref, rk_ref, bd_ref,
         vf_ref, v0_ref, v1_ref, v2_ref,
         r_o, lw_o, k_o, v_o, a_o, b_o, g_o, bonus_o) = refs
    else:
        (p_ref, prev_ref, mu_ref, w0_ref, w2_ref, a0_ref, a2_ref, g2_ref, kk_ref, ka_ref, rk_ref, bd_ref,
         r_o, lw_o, k_o, v_o, a_o, b_o, g_o, bonus_o) = refs
    cur = p_ref[...].astype(F32)
    tm = cur.shape[0]
    first = (pl.program_id(0) % tiles_per_seq) == 0
    last_prev = jnp.where(first, 0.0, prev_ref[PREV_ROWS - 1:PREV_ROWS, :].astype(F32))
    rolled = pltpu.roll(cur, 1, axis=0)
    top = lax.broadcasted_iota(jnp.int32, (8, cur.shape[1]), 0) == 0
    prev = jnp.concatenate([jnp.where(top, last_prev, rolled[:8]), rolled[8:]], axis=0)
    x = cur + (prev - cur) * mu_ref[...]
    w3 = 3 * RW_WIDTH
    r = x[:, :RW_WIDTH]
    k = x[:, RW_WIDTH:2 * RW_WIDTH]
    v = x[:, 2 * RW_WIDTH:w3]
    xwa = x[:, w3:w3 + LANES]
    xg = x[:, w3 + LANES:]
    u = w0_ref[...] + _dot_x3(jnp.tanh(xwa), w2_ref[...])
    lw_o[...] = -DECAY_SCALE * jax.nn.sigmoid(u)
    a = jax.nn.sigmoid(a0_ref[...] + _dot_x3(xwa, a2_ref[...]))
    g_o[...] = _dot(jax.nn.sigmoid(xg), g2_ref[...]).astype(g_o.dtype)
    if has_vres:
        mix = jax.nn.sigmoid(v0_ref[...] + _dot_x3(_dot_x3(v, v1_ref[...]), v2_ref[...]))
        v = v + (vf_ref[...].astype(F32) - v) * mix
    bd = bd_ref[...]
    kk = k * kk_ref[...]
    kk = kk * lax.rsqrt(jnp.maximum(_segsum(kk * kk, bd), 1e-24))
    k = k * (1.0 + (a - 1.0) * ka_ref[...])
    r_o[...] = r.astype(r_o.dtype)
    k_o[...] = k.astype(k_o.dtype)
    v_o[...] = v.astype(v_o.dtype)
    a_o[...] = (-kk).astype(a_o.dtype)
    b_o[...] = (kk * a).astype(b_o.dtype)
    bonus_o[...] = (_segsum(r * k * rk_ref[...], bd) * v).astype(bonus_o.dtype)


def rw_prep(proj, lp, v_first, *, seq, tm=RW_PREP_ROWS):
    t = proj.shape[0]
    has_vres = v_first is not None
    row = lambda i: (i, 0)
    fix = lambda i: (0, 0)
    vec = pl.BlockSpec((1, RW_WIDTH), fix)
    in_specs = [pl.BlockSpec((tm, RW_COLS), row),
                pl.BlockSpec((PREV_ROWS, RW_COLS), lambda i: (jnp.maximum(i * (tm // PREV_ROWS) - 1, 0), 0)),
                pl.BlockSpec((1, RW_COLS), fix),
                vec, pl.BlockSpec((LANES, RW_WIDTH), fix),
                vec, pl.BlockSpec((LANES, RW_WIDTH), fix),
                pl.BlockSpec((RW_GATE_RANK, RW_WIDTH), fix),
                vec, vec, vec, pl.BlockSpec((RW_WIDTH, RW_WIDTH), fix)]
    args = [proj, proj, lp["rw_mu"], lp["rw_w0"], lp["rw_w2"], lp["rw_a0"], lp["rw_a2"], lp["rw_g2"],
            lp["rw_k_k"], lp["rw_k_a"], lp["rw_r_k"], lp["rw_bd"]]
    if has_vres:
        in_specs += [pl.BlockSpec((tm, RW_WIDTH), row), vec,
                     pl.BlockSpec(lp["rw_v1"].shape, fix), pl.BlockSpec(lp["rw_v2"].shape, fix)]
        args += [v_first, lp["rw_v0"], lp["rw_v1"], lp["rw_v2"]]
    dtypes = [BF16, F32, BF16, BF16, BF16, BF16, BF16, BF16]
    return pl.pallas_call(
        functools.partial(_rw_prep_kernel, tiles_per_seq=seq // tm, has_vres=has_vres),
        grid=(t // tm,),
        in_specs=in_specs,
        out_specs=[pl.BlockSpec((tm, RW_WIDTH), row)] * 8,
        out_shape=[jax.ShapeDtypeStruct((t, RW_WIDTH), dt) for dt in dtypes],
        compiler_params=_params("parallel"),
        name="rw_prep",
    )(*args)


def _cumsum_rows(tri, x):
    hi = x.astype(BF16)
    lo = (x - hi.astype(F32)).astype(BF16)
    return jnp.dot(tri, hi, preferred_element_type=F32) + jnp.dot(tri, lo, preferred_element_type=F32)


def _rows_bcast(x, c, which):
    n = x.shape[0] // c
    return jnp.concatenate([jnp.broadcast_to(x[j * c + which:j * c + which + 1], (c, x.shape[1]))
                            for j in range(n)], axis=0)


def _rw_scan_kernel(r_ref, lw_ref, k_ref, v_ref, a_ref, b_ref, bonus_ref, g_ref, lng_ref, lnb_ref, y_ref, s_ref):
    c = SCAN_CHUNK
    n_tok = r_ref.shape[0]
    pairs = range(r_ref.shape[1] // LANES)
    heads = [(p, hh) for p in pairs for hh in range(2)]
    nch = n_tok // c

    @pl.when(pl.program_id(2) == 0)
    def _():
        s_ref[...] = jnp.zeros_like(s_ref)

    ri = lax.broadcasted_iota(jnp.int32, (n_tok, n_tok), 0)
    ci = lax.broadcasted_iota(jnp.int32, (n_tok, n_tok), 1)
    same = (ri // c) == (ci // c)
    incl = jnp.logical_and(same, ci <= ri)
    strict = jnp.logical_and(same, ci < ri)
    eye = jnp.where(ri == ci, 1.0, 0.0)
    tri = jnp.where(incl, 1.0, 0.0).astype(BF16)
    head0 = lax.broadcasted_iota(jnp.int32, (n_tok, LANES), 1) < RW_HEAD
    in_head = (head0, jnp.logical_not(head0))
    sr = lax.broadcasted_iota(jnp.int32, (LANES, LANES), 0)
    sc = lax.broadcasted_iota(jnp.int32, (LANES, LANES), 1)
    same_head = (sr // RW_HEAD) == (sc // RW_HEAD)
    eye_s = jnp.where(sr == sc, 1.0, 0.0).astype(BF16)

    cols = lambda ref, p: ref[:, p * LANES:(p + 1) * LANES]
    lw = [cols(lw_ref, p) for p in pairs]
    cum = [_cumsum_rows(tri, x) for x in lw]
    cum_mid = [_rows_bcast(x, c, c // 2 - 1) for x in cum]
    cum_end = [_rows_bcast(x, c, c - 1) for x in cum]
    r = [cols(r_ref, p).astype(F32) for p in pairs]
    a = [cols(a_ref, p).astype(F32) for p in pairs]
    k = [cols(k_ref, p).astype(F32) for p in pairs]
    b = [cols(b_ref, p).astype(F32) for p in pairs]
    v = [cols(v_ref, p).astype(F32) for p in pairs]
    r_abs = [r[p] * jnp.exp(cum[p]) for p in pairs]
    a_abs = [a[p] * jnp.exp(cum[p] - lw[p]) for p in pairs]
    g_end = [jnp.exp(cum_end[p] - cum[p]) for p in pairs]
    bk_end = [[jnp.concatenate([(b[p] * g_end[p])[j * c:(j + 1) * c], (k[p] * g_end[p])[j * c:(j + 1) * c]],
                               axis=0).astype(BF16) for j in range(n_tok // c)] for p in pairs]
    g_bwd = [jnp.exp(cum_mid[p] - cum[p]) for p in pairs]
    r_mid = [r[p] * jnp.exp(cum[p] - cum_mid[p]) for p in pairs]
    a_mid = [a[p] * jnp.exp(cum[p] - lw[p] - cum_mid[p]) for p in pairs]
    rhs = [jnp.concatenate([b[p] * g_bwd[p], k[p] * g_bwd[p]], axis=0).astype(BF16) for p in pairs]
    a_h = [jnp.where(in_head[hh], a_mid[p], 0.0).astype(BF16) for p, hh in heads]
    r_h = [jnp.where(in_head[hh], r_mid[p], 0.0).astype(BF16) for p, hh in heads]
    pm = [[_dot_nt(jnp.concatenate([a_h[n][j * c:(j + 1) * c], r_h[n][j * c:(j + 1) * c]], axis=0),
                   jnp.concatenate([rhs[p][j * c:(j + 1) * c], rhs[p][n_tok + j * c:n_tok + (j + 1) * c]], axis=0))
           for j in range(nch)] for n, (p, hh) in enumerate(heads)]
    strict_c = strict[:c, :c]
    incl_c = incl[:c, :c]
    zero_cc = jnp.zeros((c, c), F32)

    def block_diag(blocks):
        return jnp.concatenate([jnp.concatenate([blocks[j] if jj == j else zero_cc for jj in range(nch)], axis=1)
                                for j in range(nch)], axis=0)

    lab = [block_diag([jnp.where(strict_c, x[:c, :c], 0.0) for x in xs]) for xs in pm]
    lak = [block_diag([jnp.where(strict_c, x[:c, c:], 0.0) for x in xs]).astype(BF16) for xs in pm]
    mrb = [block_diag([jnp.where(incl_c, x[c:, :c], 0.0) for x in xs]).astype(BF16) for xs in pm]
    mrk = [block_diag([jnp.where(incl_c, x[c:, c:], 0.0) for x in xs]).astype(BF16) for xs in pm]
    vh = [jnp.where(in_head[hh], v[p], 0.0).astype(BF16) for p, hh in heads]
    inv = [eye + x for x in lab]
    lp = lab
    q = 2
    while q < c:
        lp = [_dot(x, x) for x in lp]
        inv = [t + _dot(t, x) for t, x in zip(inv, lp)]
        q *= 2
    xv = [jnp.dot(lak[n], vh[n], preferred_element_type=F32) for n in range(len(heads))]
    tw = [_dot(inv[n], jnp.concatenate([jnp.where(in_head[hh], a_abs[p], 0.0), xv[n]], axis=1))
          for n, (p, hh) in enumerate(heads)]
    y2h = [jnp.dot(mrk[n], vh[n], preferred_element_type=F32) for n in range(len(heads))]
    w1 = [tw[2 * p][:, :LANES] + tw[2 * p + 1][:, :LANES] for p in pairs]
    w2 = [tw[2 * p][:, LANES:] + tw[2 * p + 1][:, LANES:] for p in pairs]
    w12 = [jnp.concatenate([w1[p], w2[p]], axis=1).astype(BF16) for p in pairs]
    lhs_xy = [jnp.concatenate([jnp.concatenate([w1[p][j * c:(j + 1) * c], r_abs[p][j * c:(j + 1) * c]], axis=0)
                               for j in range(n_tok // c)], axis=0).astype(BF16) for p in pairs]
    vb = [x.astype(BF16) for x in v]
    zeros_c = jnp.zeros((c, LANES), BF16)
    u_rows = [[] for _ in pairs]
    y0_rows = [[] for _ in pairs]
    for j in range(n_tok // c):
        rows = slice(j * c, (j + 1) * c)
        s0 = [s_ref[p] for p in pairs]
        s0b = [x.astype(BF16) for x in s0]
        xy = [_dot_nt(lhs_xy[p][2 * j * c:2 * (j + 1) * c], s0b[p]) for p in pairs]
        uvt = [_dot_nt(jnp.concatenate([s0b[p], eye_s], axis=1),
                       jnp.concatenate([w12[p][rows], jnp.concatenate([zeros_c, vb[p][rows]], axis=1)], axis=0))
               for p in pairs]
        upd = [jnp.dot(uvt[p].astype(BF16), bk_end[p][j], preferred_element_type=F32) for p in pairs]
        for p in pairs:
            u_rows[p].append(xy[p][:c] + w2[p][rows])
            y0_rows[p].append(xy[p][c:])
            s_ref[p] = s0[p] * jnp.exp(cum_end[p][j * c:j * c + 1, :]) + jnp.where(same_head, upd[p], 0.0)
    ub = [jnp.concatenate(u_rows[p], axis=0).astype(BF16) for p in pairs]
    y1 = [jnp.dot(mrb[n], ub[p], preferred_element_type=F32) for n, (p, hh) in enumerate(heads)]
    half_sum = lambda x: jnp.where(head0, jnp.sum(jnp.where(head0, x, 0.0), axis=-1, keepdims=True),
                                   jnp.sum(jnp.where(head0, 0.0, x), axis=-1, keepdims=True))
    for p in pairs:
        sl = slice(p * LANES, (p + 1) * LANES)
        y = (jnp.concatenate(y0_rows[p], axis=0) + y2h[2 * p] + y2h[2 * p + 1]
             + jnp.where(head0, y1[2 * p], y1[2 * p + 1]))
        d = y - half_sum(y) * (1.0 / RW_HEAD)
        var = half_sum(d * d) * (1.0 / RW_HEAD)
        yn = d * lax.rsqrt(var + RW_GN_EPS) * lng_ref[:, sl] + lnb_ref[:, sl]
        y_ref[:, sl] = ((yn + bonus_ref[:, sl].astype(F32)) * g_ref[:, sl].astype(F32)).astype(y_ref.dtype)


def rw_scan(r, lw, k, v, a, b, bonus, g, ln_g, ln_b, *, batch, seq):
    t = r.shape[0]
    ns = seq // SCAN_STEP
    width = SCAN_PAIRS * LANES
    spec = pl.BlockSpec((SCAN_STEP, width), lambda bi, p, i: (bi * ns + i, p))
    vec = pl.BlockSpec((1, width), lambda bi, p, i: (0, p))
    return pl.pallas_call(
        _rw_scan_kernel,
        grid=(batch, RW_WIDTH // width, ns),
        in_specs=[spec] * 8 + [vec] * 2,
        out_specs=spec,
        out_shape=jax.ShapeDtypeStruct((t, RW_WIDTH), BF16),
        scratch_shapes=[pltpu.VMEM((SCAN_PAIRS, LANES, LANES), F32)],
        compiler_params=_params("parallel", "parallel", "arbitrary"),
        name="rw_scan",
    )(r, lw, k, v, a, b, bonus, g, ln_g, ln_b)


def _merge_kernel(h_ref, ya_ref, yb_ref, yc_ref, ga_ref, gb_ref, gc_ref, wa_ref, wb_ref, wc_ref, wo_ref, o_ref):
    gate = lambda g_ref: jax.nn.sigmoid(g_ref[...].astype(F32))
    merged = (gate(ga_ref) * jnp.dot(ya_ref[...], wa_ref[...], preferred_element_type=F32)
              + gate(gb_ref) * jnp.dot(yb_ref[...], wb_ref[...], preferred_element_type=F32)
              + gate(gc_ref) * jnp.dot(yc_ref[...], wc_ref[...], preferred_element_type=F32))
    o_ref[...] = h_ref[...] + jnp.dot(merged.astype(BF16), wo_ref[...], preferred_element_type=F32)


def merge(h, ya, yb, yc, proj, wa, wb, wc, wo, *, tm=MERGE_ROWS):
    t, d = h.shape
    row = lambda i: (i, 0)
    fix = lambda i: (0, 0)
    gate_specs = [pl.BlockSpec((tm, d), functools.partial(lambda i, n: (i, PROJ_GATE // d + n), n=n))
                  for n in range(3)]
    return pl.pallas_call(
        _merge_kernel,
        grid=(t // tm,),
        in_specs=[pl.BlockSpec((tm, d), row),
                  pl.BlockSpec((tm, ya.shape[1]), row),
                  pl.BlockSpec((tm, yb.shape[1]), row),
                  pl.BlockSpec((tm, yc.shape[1]), row),
                  *gate_specs,
                  pl.BlockSpec(wa.shape, fix), pl.BlockSpec(wb.shape, fix),
                  pl.BlockSpec(wc.shape, fix), pl.BlockSpec(wo.shape, fix)],
        out_specs=pl.BlockSpec((tm, d), row),
        out_shape=jax.ShapeDtypeStruct((t, d), F32),
        compiler_params=_params("parallel"),
        name="merge",
    )(h, ya, yb, yc, proj, proj, proj, wa, wb, wc, wo)


def _pad_cols(w, n):
    return jnp.pad(w, ((0, 0), (0, n - w.shape[1])))


def _layer_params(i, p):
    wuq = p["mla_w_uq"][i].reshape(MLA_Q_RANK, MLA_HEADS, MLA_QK)
    wuq = jnp.pad(wuq, ((0, 0), (0, 0), (0, LANES - MLA_QK)))
    wuq = jnp.concatenate([wuq.reshape(MLA_Q_RANK, -1), _rope_partner(wuq).reshape(MLA_Q_RANK, -1)], axis=1)
    qn = _pad_cols(p["mla_q_norm"][i].reshape(1, -1), LANES)
    kn = _pad_cols(p["mla_k_norm"][i].reshape(1, -1), LANES)
    gains = jnp.concatenate([qn, _rope_partner(qn), kn, _rope_partner(kn), jnp.zeros((4, LANES), F32)], axis=0)
    wukv = p["mla_w_ukv"][i].reshape(MLA_KV_RANK, MLA_HEADS, MLA_NOPE + MLA_V)
    wuk = jnp.pad(wukv[:, :, :MLA_NOPE], ((0, 0), (0, 0), (0, LANES - MLA_NOPE)))
    wuk = wuk.reshape(MLA_KV_RANK, MLA_HEADS * LANES)
    wuv = wukv[:, :, MLA_NOPE:].reshape(MLA_KV_RANK, MLA_HEADS * MLA_V)
    head_of = jnp.arange(RW_WIDTH) // RW_HEAD
    lp = {
        "w_proj": proj_weight(p["w_in"], i),
        "qa": p["mla_q_a_norm"][i].reshape(1, -1),
        "kva": p["mla_kv_a_norm"][i].reshape(1, -1),
        "wuq": wuq.astype(BF16), "wuk": wuk.astype(BF16), "wuvt": wuv.T.astype(BF16),
        "mla_gains": gains,
        "sg_vn": p["sg_v_norm"][i].reshape(1, -1),
        "sg_w": p["sg_w_s"][i].transpose(1, 0, 2).reshape(SG_CHUNK, SG_GROUPS * SG_CHUNK),
        "sg_b": jnp.repeat(p["sg_b_s"][i].T, SG_WIDTH // SG_GROUPS, axis=1),
        "rw_mu": p["rw_mu"][i].reshape(1, -1),
        "rw_w0": p["rw_w0"][i].reshape(1, -1),
        "rw_w2": jnp.pad(p["rw_w2"][i], ((0, RW_A_RANK), (0, 0))),
        "rw_a0": p["rw_a0"][i].reshape(1, -1),
        "rw_a2": jnp.pad(p["rw_a2"][i], ((RW_DECAY_RANK, 0), (0, 0))),
        "rw_g2": p["rw_g2"][i].astype(BF16),
        "rw_k_k": p["rw_k_k"][i].reshape(1, -1),
        "rw_k_a": p["rw_k_a"][i].reshape(1, -1),
        "rw_r_k": p["rw_r_k"][i].reshape(1, -1),
        "rw_ln_g": p["rw_ln_g"][i].reshape(1, -1),
        "rw_ln_b": p["rw_ln_b"][i].reshape(1, -1),
        "rw_bd": (head_of[:, None] == head_of[None, :]).astype(BF16),
        "w_out_mla": p["w_out_mla"][i].astype(BF16),
        "w_out_sg": p["w_out_sg"][i].astype(BF16),
        "w_out_rw": p["w_out_rw"][i].astype(BF16),
        "w_o": p["w_o"][i].astype(BF16),
    }
    if i > 0:
        lp["rw_v0"] = p["rw_v0"][i - 1].reshape(1, -1)
        lp["rw_v1"] = _pad_cols(p["rw_v1"][i - 1], LANES)
        lp["rw_v2"] = jnp.pad(p["rw_v2"][i - 1], ((0, LANES - p["rw_v2"].shape[1]), (0, 0)))
    return lp


def _rope_tables(positions):
    half = MLA_ROPE // 2
    per_row = LANES // half
    inv_freq = ROPE_BASE ** (-jnp.arange(0, MLA_ROPE, 2, dtype=F32) / MLA_ROPE)
    pos = jnp.repeat(positions.astype(F32).reshape(-1, per_row), half, axis=1)
    ang = pos * jnp.tile(inv_freq, per_row)
    return jnp.cos(ang).reshape(-1, half), jnp.sin(ang).reshape(-1, half)


def kernel(x, positions, ffn1_norm, ffn1_w_gate, ffn1_w_up, ffn1_w_down, mix_norm, w_in, mla_q_a_norm, mla_w_uq, mla_kv_a_norm, mla_w_ukv, mla_q_norm, mla_k_norm, sg_v_norm, sg_w_s, sg_b_s, rw_mu, rw_w0, rw_w2, rw_a0, rw_a2, rw_g2, rw_k_k, rw_k_a, rw_r_k, rw_ln_g, rw_ln_b, rw_v0, rw_v1, rw_v2, w_out_mla, w_out_sg, w_out_rw, w_o, ffn2_norm, ffn2_w_gate, ffn2_w_up, ffn2_w_down):
    params = dict(ffn1_norm=ffn1_norm, ffn1_w_gate=ffn1_w_gate, ffn1_w_up=ffn1_w_up, ffn1_w_down=ffn1_w_down,
                  mix_norm=mix_norm, w_in=w_in, mla_q_a_norm=mla_q_a_norm, mla_w_uq=mla_w_uq,
                  mla_kv_a_norm=mla_kv_a_norm, mla_w_ukv=mla_w_ukv, mla_q_norm=mla_q_norm, mla_k_norm=mla_k_norm,
                  sg_v_norm=sg_v_norm, sg_w_s=sg_w_s, sg_b_s=sg_b_s, rw_mu=rw_mu, rw_w0=rw_w0, rw_w2=rw_w2,
                  rw_a0=rw_a0, rw_a2=rw_a2, rw_g2=rw_g2, rw_k_k=rw_k_k, rw_k_a=rw_k_a, rw_r_k=rw_r_k,
                  rw_ln_g=rw_ln_g, rw_ln_b=rw_ln_b, rw_v0=rw_v0, rw_v1=rw_v1, rw_v2=rw_v2,
                  w_out_mla=w_out_mla, w_out_sg=w_out_sg, w_out_rw=w_out_rw, w_o=w_o,
                  ffn2_norm=ffn2_norm, ffn2_w_gate=ffn2_w_gate, ffn2_w_up=ffn2_w_up, ffn2_w_down=ffn2_w_down)
    batch, seq, d = x.shape
    depth = w_in.shape[0]
    cos, sin = _rope_tables(positions)
    h = x.reshape(batch * seq, d)
    v_first = None
    for i in range(depth):
        lp = _layer_params(i, params)
        h = ffn_half_step(h, ffn1_norm, ffn1_w_gate, ffn1_w_up, ffn1_w_down, i)
        proj = rms_matmul(h, mix_norm[i], lp["w_proj"], tm=PROJ_ROWS, tn=PROJ_TILE, out_dtype=BF16)
        q, k, vt = mla_prep(proj, cos, sin, lp["qa"], lp["kva"], lp["wuq"], lp["wuk"], lp["wuvt"],
                           lp["mla_gains"])
        y_a = mla_attention(q, k, vt, batch=batch, seq=seq)
        y_b = sgu(proj, lp["sg_vn"], lp["sg_w"], lp["sg_b"])
        r, lw, kr, vr, a, b, g, bonus = rw_prep(proj, lp, v_first, seq=seq)
        if v_first is None:
            v_first = vr
        y_c = rw_scan(r, lw, kr, vr, a, b, bonus, g, lp["rw_ln_g"], lp["rw_ln_b"], batch=batch, seq=seq)
        h = merge(h, y_a, y_b, y_c, proj, lp["w_out_mla"], lp["w_out_sg"], lp["w_out_rw"], lp["w_o"])
        h = ffn_half_step(h, ffn2_norm, ffn2_w_gate, ffn2_w_up, ffn2_w_down, i)
    return h.reshape(batch, seq, d)
```

```python
import functools

import jax
import jax.numpy as jnp
from jax import lax
from jax.experimental import pallas as pl
from jax.experimental.pallas import tpu as pltpu

F32 = jnp.float32
BF16 = jnp.bfloat16

LANES = 128
EPS = 1e-6
RW_GN_EPS = 64e-5
ROPE_BASE = 10000.0

MLA_HEADS = 8
MLA_NOPE = 64
MLA_ROPE = 32
MLA_QK = MLA_NOPE + MLA_ROPE
MLA_V = 64
MLA_Q_RANK = 256
MLA_KV_RANK = 128
SG_WIDTH = 512
SG_GROUPS = 8
SG_CHUNK = 128
RW_HEADS = 8
RW_HEAD = 64
RW_WIDTH = RW_HEADS * RW_HEAD
RW_DECAY_RANK = 64
RW_A_RANK = 64
RW_GATE_RANK = 128
RW_COLS = 3 * RW_WIDTH + RW_DECAY_RANK + RW_A_RANK + RW_GATE_RANK

V7X_VMEM_BYTES = 64 * 1024 * 1024
VMEM_LIMIT = V7X_VMEM_BYTES * 7 // 8
BF16_SUBLANES = 16

FFN_ROWS = 2048
FFN_COLS = 256
PROJ_ROWS = 1024
PROJ_TILE = 3328
MLA_PREP_ROWS = 1024
SGU_ROWS = 1024
RW_PREP_ROWS = 512
MERGE_ROWS = 1024
SCAN_CHUNK = 128
SCAN_STEP = 256
SCAN_PAIRS = 4
ATTN_BLOCK = 256
ATTN_HEADS = 8
PROJ_GATE = 2048
PROJ_MLA = PROJ_GATE + 3 * 1024
PROJ_SG = PROJ_MLA + 512
PROJ_COLS = PROJ_SG + 2 * SG_WIDTH
PREV_ROWS = BF16_SUBLANES
SUM_ROWS = BF16_SUBLANES
LOG2E = 1.4426950408889634
DECAY_SCALE = 0.6065306597126334


def _params(*sem):
    return pltpu.CompilerParams(dimension_semantics=sem, vmem_limit_bytes=VMEM_LIMIT)


def _dot(a, b):
    return jnp.dot(a.astype(BF16), b.astype(BF16), preferred_element_type=F32)


def _dot_x3(a, b):
    ah = a.astype(BF16)
    al = (a - ah.astype(F32)).astype(BF16)
    bh = b.astype(BF16)
    bl = (b - bh.astype(F32)).astype(BF16)
    f = lambda x, y: jnp.dot(x, y, preferred_element_type=F32)
    return f(ah, bh) + f(al, bh) + f(ah, bl)


def _dot_nt(a, b):
    return lax.dot_general(a, b, (((1,), (1,)), ((), ())), preferred_element_type=F32)


def _rms(x, g, n=None):
    n = x.shape[-1] if n is None else n
    ms = jnp.sum(x * x, axis=-1, keepdims=True) * (1.0 / n)
    return x * lax.rsqrt(ms + EPS) * g


def _segsum(x, bd):
    hi = x.astype(BF16)
    lo = (x - hi.astype(F32)).astype(BF16)
    return (jnp.dot(hi, bd, preferred_element_type=F32) + jnp.dot(lo, bd, preferred_element_type=F32))


def _proj_weight_kernel(wt_ref, o_ref):
    x = wt_ref[...]
    zeros = lambda n: jnp.zeros((n, x.shape[1]), x.dtype)
    o_kv = MLA_Q_RANK + MLA_KV_RANK
    o_sg = o_kv + MLA_ROPE
    o_rw = o_sg + 2 * SG_WIDTH
    o_gate = o_rw + RW_COLS
    rows = jnp.concatenate(
        [x[o_rw:o_gate], zeros(PROJ_GATE - RW_COLS), x[o_gate:],
         x[:o_kv], zeros(MLA_NOPE), x[o_kv:o_sg], zeros(LANES - MLA_QK),
         x[o_sg:o_rw]], axis=0)
    o_ref[...] = rows.T.astype(o_ref.dtype)


def proj_weight(w_in, layer, *, tr=LANES):
    wt = jnp.swapaxes(w_in, 1, 2)
    _, n, d = wt.shape
    return pl.pallas_call(
        _proj_weight_kernel,
        grid=(d // tr,),
        in_specs=[pl.BlockSpec((None, n, tr), lambda i: (layer, 0, i))],
        out_specs=pl.BlockSpec((tr, PROJ_COLS), lambda i: (i, 0)),
        out_shape=jax.ShapeDtypeStruct((d, PROJ_COLS), BF16),
        compiler_params=_params("parallel"),
        name="proj_weight",
    )(wt)


def _rms_matmul_kernel(h_ref, g_ref, w_ref, o_ref, z_ref):
    @pl.when(pl.program_id(1) == 0)
    def _():
        z_ref[...] = _rms(h_ref[...], g_ref[...]).astype(BF16)

    o_ref[...] = jnp.dot(z_ref[...], w_ref[...], preferred_element_type=F32).astype(o_ref.dtype)


def rms_matmul(h, g, w, *, tm, tn, out_dtype=F32):
    t, d = h.shape
    n = w.shape[1]
    return pl.pallas_call(
        _rms_matmul_kernel,
        grid=(t // tm, n // tn),
        in_specs=[pl.BlockSpec((tm, d), lambda i, j: (i, 0)),
                  pl.BlockSpec((1, d), lambda i, j: (0, 0)),
                  pl.BlockSpec((d, tn), lambda i, j: (0, j))],
        out_specs=pl.BlockSpec((tm, tn), lambda i, j: (i, j)),
        out_shape=jax.ShapeDtypeStruct((t, n), out_dtype),
        scratch_shapes=[pltpu.VMEM((tm, d), BF16)],
        compiler_params=_params("parallel", "arbitrary"),
        name="rms_matmul",
    )(h, g.reshape(1, d), w)


def _ffn_kernel(h_ref, g_ref, wg_ref, wu_ref, wd_ref, o_ref, z_ref, acc_ref):
    j = pl.program_id(1)

    @pl.when(j == 0)
    def _():
        z_ref[...] = _rms(h_ref[...], g_ref[...]).astype(BF16)
        acc_ref[...] = jnp.zeros_like(acc_ref)

    z = z_ref[...]
    a = jnp.dot(z, wg_ref[...].astype(BF16), preferred_element_type=F32)
    b = jnp.dot(z, wu_ref[...].astype(BF16), preferred_element_type=F32)
    t = (a * jax.nn.sigmoid(a) * b).astype(BF16)
    acc_ref[...] += jnp.dot(t, wd_ref[...].astype(BF16), preferred_element_type=F32)

    @pl.when(j == pl.num_programs(1) - 1)
    def _():
        o_ref[...] = h_ref[...] + 0.5 * acc_ref[...]


def ffn_half_step(h, g, wg, wu, wd, layer, *, tm=FFN_ROWS, tf=FFN_COLS):
    t, d = h.shape
    f = wg.shape[2]
    return pl.pallas_call(
        _ffn_kernel,
        grid=(t // tm, f // tf),
        in_specs=[pl.BlockSpec((tm, d), lambda i, j: (i, 0)),
                  pl.BlockSpec((None, 1, d), lambda i, j: (layer, 0, 0)),
                  pl.BlockSpec((None, d, tf), lambda i, j: (layer, 0, j)),
                  pl.BlockSpec((None, d, tf), lambda i, j: (layer, 0, j)),
                  pl.BlockSpec((None, tf, d), lambda i, j: (layer, j, 0))],
        out_specs=pl.BlockSpec((tm, d), lambda i, j: (i, 0)),
        out_shape=jax.ShapeDtypeStruct((t, d), F32),
        scratch_shapes=[pltpu.VMEM((tm, d), BF16), pltpu.VMEM((tm, d), F32)],
        compiler_params=_params("parallel", "arbitrary"),
        name="ffn_half_step",
    )(h, g.reshape(g.shape[0], 1, d), wg, wu, wd)


def _rope_partner(x):
    half = MLA_ROPE // 2
    zeros = jnp.zeros_like(x[..., :MLA_NOPE])
    return jnp.concatenate([zeros, x[..., MLA_NOPE + half:MLA_QK], x[..., MLA_NOPE:MLA_NOPE + half],
                            jnp.zeros_like(x[..., MLA_QK:])], axis=-1)


def _mla_prep_kernel(c_ref, cos_ref, sin_ref, qa_ref, kva_ref, wuq_ref, wuk_ref, wuvt_ref, gains_ref,
                     q_ref, k_ref, vt_ref):
    c = c_ref[...].astype(F32)
    hw = MLA_HEADS * LANES
    zq = _rms(c[:, :MLA_Q_RANK], qa_ref[...]).astype(BF16)
    zkv = _rms(c[:, MLA_Q_RANK:MLA_Q_RANK + MLA_KV_RANK], kva_ref[...]).astype(BF16)
    kpe = c[:, MLA_Q_RANK + MLA_KV_RANK:]
    q = jnp.dot(zq, wuq_ref[...], preferred_element_type=F32)
    kn = jnp.dot(zkv, wuk_ref[...], preferred_element_type=F32)
    for j in range(vt_ref.shape[0]):
        rows = slice(j * ATTN_BLOCK, (j + 1) * ATTN_BLOCK)
        vt_ref[j] = _dot_nt(wuvt_ref[...], zkv[rows]).astype(BF16)
    cos, sin = cos_ref[...], sin_ref[...]
    ones, zeros = jnp.ones_like(c[:, :MLA_NOPE]), jnp.zeros_like(c[:, :MLA_NOPE])
    cos_full = jnp.concatenate([ones, cos, cos, ones[:, :LANES - MLA_QK]], axis=1)
    sin_signed = jnp.concatenate([zeros, -sin, sin, zeros[:, :LANES - MLA_QK]], axis=1)
    q_cos = gains_ref[0:1, :] * cos_full
    q_sin = gains_ref[1:2, :] * sin_signed
    k_cos = gains_ref[2:3, :] * cos_full
    lane = lax.broadcasted_iota(jnp.int32, kpe.shape, 1)
    kpe_partner = jnp.where(lane < MLA_NOPE + MLA_ROPE // 2,
                            pltpu.roll(kpe, LANES - MLA_ROPE // 2, axis=1),
                            pltpu.roll(kpe, MLA_ROPE // 2, axis=1))
    k_rot = kpe_partner * (gains_ref[3:4, :] * sin_signed)
    scale = MLA_QK ** -0.5 * LOG2E
    inv_n = 1.0 / MLA_QK
    for h in range(MLA_HEADS):
        sl = slice(h * LANES, (h + 1) * LANES)
        x = q[:, sl]
        r = lax.rsqrt(jnp.sum(x * x, axis=-1, keepdims=True) * inv_n + EPS) * scale
        q_ref[:, sl] = ((x * q_cos + q[:, hw + h * LANES:hw + (h + 1) * LANES] * q_sin) * r).astype(BF16)
        y = kn[:, sl] + kpe
        r = lax.rsqrt(jnp.sum(y * y, axis=-1, keepdims=True) * inv_n + EPS)
        k_ref[:, sl] = ((y * k_cos + k_rot) * r).astype(BF16)


def mla_prep(c, cos, sin, qa, kva, wuq, wuk, wuvt, gains, *, tm=MLA_PREP_ROWS):
    t = c.shape[0]
    hw = MLA_HEADS * LANES
    vw = MLA_HEADS * MLA_V
    row = lambda i: (i, 0)
    fix = lambda i: (0, 0)
    return pl.pallas_call(
        _mla_prep_kernel,
        grid=(t // tm,),
        in_specs=[pl.BlockSpec((tm, PROJ_SG - PROJ_MLA), lambda i: (i, PROJ_MLA // (PROJ_SG - PROJ_MLA))),
                  pl.BlockSpec((tm, MLA_ROPE // 2), row),
                  pl.BlockSpec((tm, MLA_ROPE // 2), row),
                  pl.BlockSpec((1, MLA_Q_RANK), fix),
                  pl.BlockSpec((1, MLA_KV_RANK), fix),
                  pl.BlockSpec(wuq.shape, fix),
                  pl.BlockSpec(wuk.shape, fix),
                  pl.BlockSpec(wuvt.shape, fix),
                  pl.BlockSpec(gains.shape, fix)],
        out_specs=[pl.BlockSpec((tm, hw), row),
                   pl.BlockSpec((tm, hw), row),
                   pl.BlockSpec((tm // ATTN_BLOCK, vw, ATTN_BLOCK), lambda i: (i, 0, 0))],
        out_shape=[jax.ShapeDtypeStruct((t, hw), BF16),
                   jax.ShapeDtypeStruct((t, hw), BF16),
                   jax.ShapeDtypeStruct((t // ATTN_BLOCK, vw, ATTN_BLOCK), BF16)],
        compiler_params=_params("parallel"),
        name="mla_prep",
    )(c, cos, sin, qa, kva, wuq, wuk, wuvt, gains)


def _attn_kernel(q_ref, k_ref, vt_ref, o_ref, st_scr, p_scr, acc_scr, mx_scr):
    i = pl.program_id(2)
    blk = ATTN_BLOCK
    heads = range(q_ref.shape[1] // LANES)
    q = q_ref[...]
    krow = lax.broadcasted_iota(jnp.int32, (blk, blk), 0)
    qcol = lax.broadcasted_iota(jnp.int32, (blk, blk), 1)
    ones_rows = jnp.ones((SUM_ROWS, blk), BF16)

    def put_scores(j, slot):
        ks = k_ref[pl.ds(pl.multiple_of(j * blk, blk), blk), :]
        for hh in heads:
            st = _dot_nt(ks[:, hh * LANES:(hh + 1) * LANES], q[:, hh * LANES:(hh + 1) * LANES])
            st_scr[slot, hh] = st
            mx_scr[slot, hh] = jnp.max(st, axis=0, keepdims=True)

    def weighted_values(j, slot):
        vts = vt_ref[j]
        return [jnp.dot(jnp.concatenate([vts[hh * MLA_V:(hh + 1) * MLA_V], ones_rows], axis=0), p_scr[slot, hh],
                        preferred_element_type=F32) for hh in heads]

    def step(j, slot, alphas_prev, ms, masked, prefetch):
        if prefetch:
            put_scores(j + 1, 1 - slot)
        pv = weighted_values(jnp.maximum(j - 1, 0), 1 - slot)
        sts = [st_scr[slot, hh] for hh in heads]
        if masked:
            keep = krow + (j - i) * blk <= qcol
            sts = [jnp.where(keep, st, -jnp.inf) for st in sts]
            m_new = tuple(jnp.maximum(ms[hh], jnp.max(sts[hh], axis=0, keepdims=True)) for hh in heads)
        else:
            m_new = tuple(jnp.maximum(ms[hh], mx_scr[slot, hh]) for hh in heads)
        for hh in heads:
            p_scr[slot, hh] = jnp.exp2(sts[hh] - m_new[hh]).astype(BF16)
            acc_scr[hh] = alphas_prev[hh] * acc_scr[hh] + pv[hh]
        return tuple(jnp.exp2(ms[hh] - m_new[hh]) for hh in heads), m_new

    def pair(t, carry, masked):
        alphas, ms = step(2 * t, 0, *carry, masked, True)
        return step(2 * t + 1, 1, alphas, ms, masked, not masked)

    p_scr[1] = jnp.zeros_like(p_scr[1])
    acc_scr[...] = jnp.zeros_like(acc_scr)
    put_scores(0, 0)
    init = (tuple(jnp.ones((1, blk), F32) for _ in heads), tuple(jnp.full((1, blk), -jnp.inf, F32) for _ in heads))
    carry = lax.fori_loop(0, i // 2, lambda t, c: pair(t, c, False), init)
    alphas, _ = pair(i // 2, carry, True)
    pv = weighted_values(2 * (i // 2) + 1, 1)
    outs = []
    for hh in heads:
        acc = alphas[hh] * acc_scr[hh] + pv[hh]
        outs.append(acc[:MLA_V] / acc[MLA_V:MLA_V + 1])
    o_ref[...] = jnp.concatenate(outs, axis=0).T.astype(o_ref.dtype)


def mla_attention(q, k, vt, *, batch, seq):
    t = q.shape[0]
    nq = seq // ATTN_BLOCK
    return pl.pallas_call(
        _attn_kernel,
        grid=(batch, MLA_HEADS // ATTN_HEADS, nq),
        in_specs=[pl.BlockSpec((ATTN_BLOCK, ATTN_HEADS * LANES), lambda b, p, i: (b * nq + i, p)),
                  pl.BlockSpec((seq, ATTN_HEADS * LANES), lambda b, p, i: (b, p)),
                  pl.BlockSpec((nq, ATTN_HEADS * MLA_V, ATTN_BLOCK), lambda b, p, i: (b, p, 0))],
        out_specs=pl.BlockSpec((ATTN_BLOCK, ATTN_HEADS * MLA_V), lambda b, p, i: (b * nq + i, p)),
        out_shape=jax.ShapeDtypeStruct((t, MLA_HEADS * MLA_V), BF16),
        scratch_shapes=[pltpu.VMEM((2, ATTN_HEADS, ATTN_BLOCK, ATTN_BLOCK), F32),
                        pltpu.VMEM((2, ATTN_HEADS, ATTN_BLOCK, ATTN_BLOCK), BF16),
                        pltpu.VMEM((ATTN_HEADS, MLA_V + SUM_ROWS, ATTN_BLOCK), F32),
                        pltpu.VMEM((2, ATTN_HEADS, 1, ATTN_BLOCK), F32)],
        compiler_params=_params("parallel", "parallel", "arbitrary"),
        name="mla_attention",
    )(q, k, vt)


def _sgu_kernel(pu_ref, pv_ref, vn_ref, w_ref, b_ref, o_ref):
    u = jax.nn.gelu(pu_ref[...].astype(F32), approximate=True)
    v = _rms(jax.nn.gelu(pv_ref[...].astype(F32), approximate=True), vn_ref[...])
    wrow = lax.broadcasted_iota(jnp.int32, (SG_CHUNK, SG_GROUPS * SG_CHUNK), 0)
    wcol = lax.broadcasted_iota(jnp.int32, (SG_CHUNK, SG_GROUPS * SG_CHUNK), 1) % SG_CHUNK
    w = jnp.where(wcol <= wrow, w_ref[...], 0.0).astype(BF16)
    low = lax.broadcasted_iota(jnp.int32, (SG_CHUNK, LANES), 1) < SG_WIDTH // SG_GROUPS
    for c in range(u.shape[0] // SG_CHUNK):
        rows = slice(c * SG_CHUNK, (c + 1) * SG_CHUNK)
        for q in range(SG_WIDTH // LANES):
            cols = slice(q * LANES, (q + 1) * LANES)
            vq = v[rows, cols]
            halves = jnp.concatenate([jnp.where(low, vq, 0.0), jnp.where(low, 0.0, vq)], axis=0).astype(BF16)
            mixed = jnp.dot(w[:, 2 * q * SG_CHUNK:2 * (q + 1) * SG_CHUNK], halves,
                            preferred_element_type=F32) + b_ref[:, cols]
            o_ref[rows, cols] = (u[rows, cols] * mixed).astype(o_ref.dtype)


def sgu(proj, v_norm, w_cat, b_full, *, tm=SGU_ROWS):
    t = proj.shape[0]
    fix = lambda i: (0, 0)
    return pl.pallas_call(
        _sgu_kernel,
        grid=(t // tm,),
        in_specs=[pl.BlockSpec((tm, SG_WIDTH), lambda i: (i, PROJ_SG // SG_WIDTH)),
                  pl.BlockSpec((tm, SG_WIDTH), lambda i: (i, PROJ_SG // SG_WIDTH + 1)),
                  pl.BlockSpec((1, SG_WIDTH), fix),
                  pl.BlockSpec(w_cat.shape, fix),
                  pl.BlockSpec(b_full.shape, fix)],
        out_specs=pl.BlockSpec((tm, SG_WIDTH), lambda i: (i, 0)),
        out_shape=jax.ShapeDtypeStruct((t, SG_WIDTH), BF16),
        compiler_params=_params("parallel"),
        name="sgu",
    )(proj, proj, v_norm, w_cat, b_full)


def _rw_prep_kernel(*refs, tiles_per_seq, has_vres):
    if has_vres:
        (p_ref, prev_ref, mu_ref, w0_ref, w2_ref, a0_ref, a2_ref, g2_ref, kk_ref, ka_ref, rk_ref, bd_ref,
         vf_ref, v0_ref, v1_ref, v2_ref,
         r_o, lw_o, k_o, v_o, a_o, b_o, g_o, bonus_o) = refs
    else:
        (p_ref, prev_ref, mu_ref, w0_ref, w2_ref, a0_ref, a2_ref, g2_ref, kk_ref, ka_ref, rk_ref, bd_ref,
         r_o, lw_o, k_o, v_o, a_o, b_o, g_o, bonus_o) = refs
    cur = p_ref[...].astype(F32)
    tm = cur.shape[0]
    first = (pl.program_id(0) % tiles_per_seq) == 0
    last_prev = jnp.where(first, 0.0, prev_ref[PREV_ROWS - 1:PREV_ROWS, :].astype(F32))
    rolled = pltpu.roll(cur, 1, axis=0)
    top = lax.broadcasted_iota(jnp.int32, (8, cur.shape[1]), 0) == 0
    prev = jnp.concatenate([jnp.where(top, last_prev, rolled[:8]), rolled[8:]], axis=0)
    x = cur + (prev - cur) * mu_ref[...]
    w3 = 3 * RW_WIDTH
    r = x[:, :RW_WIDTH]
    k = x[:, RW_WIDTH:2 * RW_WIDTH]
    v = x[:, 2 * RW_WIDTH:w3]
    xwa = x[:, w3:w3 + LANES]
    xg = x[:, w3 + LANES:]
    u = w0_ref[...] + _dot_x3(jnp.tanh(xwa), w2_ref[...])
    lw_o[...] = -DECAY_SCALE * jax.nn.sigmoid(u)
    a = jax.nn.sigmoid(a0_ref[...] + _dot_x3(xwa, a2_ref[...]))
    g_o[...] = _dot(jax.nn.sigmoid(xg), g2_ref[...]).astype(g_o.dtype)
    if has_vres:
        mix = jax.nn.sigmoid(v0_ref[...] + _dot(_dot(v, v1_ref[...]), v2_ref[...]))
        v = v + (vf_ref[...].astype(F32) - v) * mix
    bd = bd_ref[...]
    kk = k * kk_ref[...]
    kk = kk * lax.rsqrt(jnp.maximum(_segsum(kk * kk, bd), 1e-24))
    k = k * (1.0 + (a - 1.0) * ka_ref[...])
    r_o[...] = r.astype(r_o.dtype)
    k_o[...] = k.astype(k_o.dtype)
    v_o[...] = v.astype(v_o.dtype)
    a_o[...] = (-kk).astype(a_o.dtype)
    b_o[...] = (kk * a).astype(b_o.dtype)
    bonus_o[...] = (_segsum(r * k * rk_ref[...], bd) * v).astype(bonus_o.dtype)


def rw_prep(proj, lp, v_first, *, seq, tm=RW_PREP_ROWS):
    t = proj.shape[0]
    has_vres = v_first is not None
    row = lambda i: (i, 0)
    fix = lambda i: (0, 0)
    vec = pl.BlockSpec((1, RW_WIDTH), fix)
    in_specs = [pl.BlockSpec((tm, RW_COLS), row),
                pl.BlockSpec((PREV_ROWS, RW_COLS), lambda i: (jnp.maximum(i * (tm // PREV_ROWS) - 1, 0), 0)),
                pl.BlockSpec((1, RW_COLS), fix),
                vec, pl.BlockSpec((LANES, RW_WIDTH), fix),
                vec, pl.BlockSpec((LANES, RW_WIDTH), fix),
                pl.BlockSpec((RW_GATE_RANK, RW_WIDTH), fix),
                vec, vec, vec, pl.BlockSpec((RW_WIDTH, RW_WIDTH), fix)]
    args = [proj, proj, lp["rw_mu"], lp["rw_w0"], lp["rw_w2"], lp["rw_a0"], lp["rw_a2"], lp["rw_g2"],
            lp["rw_k_k"], lp["rw_k_a"], lp["rw_r_k"], lp["rw_bd"]]
    if has_vres:
        in_specs += [pl.BlockSpec((tm, RW_WIDTH), row), vec,
                     pl.BlockSpec(lp["rw_v1"].shape, fix), pl.BlockSpec(lp["rw_v2"].shape, fix)]
        args += [v_first, lp["rw_v0"], lp["rw_v1"], lp["rw_v2"]]
    dtypes = [BF16, F32, BF16, BF16, BF16, BF16, BF16, BF16]
    return pl.pallas_call(
        functools.partial(_rw_prep_kernel, tiles_per_seq=seq // tm, has_vres=has_vres),
        grid=(t // tm,),
        in_specs=in_specs,
        out_specs=[pl.BlockSpec((tm, RW_WIDTH), row)] * 8,
        out_shape=[jax.ShapeDtypeStruct((t, RW_WIDTH), dt) for dt in dtypes],
        compiler_params=_params("parallel"),
        name="rw_prep",
    )(*args)


def _cumsum_rows(tri, x):
    hi = x.astype(BF16)
    lo = (x - hi.astype(F32)).astype(BF16)
    return jnp.dot(tri, hi, preferred_element_type=F32) + jnp.dot(tri, lo, preferred_element_type=F32)


def _rows_bcast(x, c, which):
    n = x.shape[0] // c
    return jnp.concatenate([jnp.broadcast_to(x[j * c + which:j * c + which + 1], (c, x.shape[1]))
                            for j in range(n)], axis=0)


def _rw_scan_kernel(r_ref, lw_ref, k_ref, v_ref, a_ref, b_ref, bonus_ref, g_ref, lng_ref, lnb_ref, y_ref, s_ref):
    c = SCAN_CHUNK
    n_tok = r_ref.shape[0]
    pairs = range(r_ref.shape[1] // LANES)
    heads = [(p, hh) for p in pairs for hh in range(2)]
    nch = n_tok // c

    @pl.when(pl.program_id(2) == 0)
    def _():
        s_ref[...] = jnp.zeros_like(s_ref)

    ri = lax.broadcasted_iota(jnp.int32, (n_tok, n_tok), 0)
    ci = lax.broadcasted_iota(jnp.int32, (n_tok, n_tok), 1)
    same = (ri // c) == (ci // c)
    incl = jnp.logical_and(same, ci <= ri)
    strict = jnp.logical_and(same, ci < ri)
    eye = jnp.where(ri == ci, 1.0, 0.0)
    tri = jnp.where(incl, 1.0, 0.0).astype(BF16)
    head0 = lax.broadcasted_iota(jnp.int32, (n_tok, LANES), 1) < RW_HEAD
    in_head = (head0, jnp.logical_not(head0))
    sr = lax.broadcasted_iota(jnp.int32, (LANES, LANES), 0)
    sc = lax.broadcasted_iota(jnp.int32, (LANES, LANES), 1)
    same_head = (sr // RW_HEAD) == (sc // RW_HEAD)
    eye_s = jnp.where(sr == sc, 1.0, 0.0).astype(BF16)

    cols = lambda ref, p: ref[:, p * LANES:(p + 1) * LANES]
    lw = [cols(lw_ref, p) for p in pairs]
    cum = [_cumsum_rows(tri, x) for x in lw]
    cum_mid = [_rows_bcast(x, c, c // 2 - 1) for x in cum]
    cum_end = [_rows_bcast(x, c, c - 1) for x in cum]
    r = [cols(r_ref, p).astype(F32) for p in pairs]
    a = [cols(a_ref, p).astype(F32) for p in pairs]
    k = [cols(k_ref, p).astype(F32) for p in pairs]
    b = [cols(b_ref, p).astype(F32) for p in pairs]
    v = [cols(v_ref, p).astype(F32) for p in pairs]
    r_abs = [r[p] * jnp.exp(cum[p]) for p in pairs]
    a_abs = [a[p] * jnp.exp(cum[p] - lw[p]) for p in pairs]
    g_end = [jnp.exp(cum_end[p] - cum[p]) for p in pairs]
    bk_end = [[jnp.concatenate([(b[p] * g_end[p])[j * c:(j + 1) * c], (k[p] * g_end[p])[j * c:(j + 1) * c]],
                               axis=0).astype(BF16) for j in range(n_tok // c)] for p in pairs]
    g_bwd = [jnp.exp(cum_mid[p] - cum[p]) for p in pairs]
    r_mid = [r[p] * jnp.exp(cum[p] - cum_mid[p]) for p in pairs]
    a_mid = [a[p] * jnp.exp(cum[p] - lw[p] - cum_mid[p]) for p in pairs]
    rhs = [jnp.concatenate([b[p] * g_bwd[p], k[p] * g_bwd[p]], axis=0).astype(BF16) for p in pairs]
    a_h = [jnp.where(in_head[hh], a_mid[p], 0.0).astype(BF16) for p, hh in heads]
    r_h = [jnp.where(in_head[hh], r_mid[p], 0.0).astype(BF16) for p, hh in heads]
    pm = [[_dot_nt(jnp.concatenate([a_h[n][j * c:(j + 1) * c], r_h[n][j * c:(j + 1) * c]], axis=0),
                   jnp.concatenate([rhs[p][j * c:(j + 1) * c], rhs[p][n_tok + j * c:n_tok + (j + 1) * c]], axis=0))
           for j in range(nch)] for n, (p, hh) in enumerate(heads)]
    strict_c = strict[:c, :c]
    incl_c = incl[:c, :c]
    zero_cc = jnp.zeros((c, c), F32)

    def block_diag(blocks):
        return jnp.concatenate([jnp.concatenate([blocks[j] if jj == j else zero_cc for jj in range(nch)], axis=1)
                                for j in range(nch)], axis=0)

    lab = [block_diag([jnp.where(strict_c, x[:c, :c], 0.0) for x in xs]) for xs in pm]
    lak = [block_diag([jnp.where(strict_c, x[:c, c:], 0.0) for x in xs]).astype(BF16) for xs in pm]
    mrb = [block_diag([jnp.where(incl_c, x[c:, :c], 0.0) for x in xs]).astype(BF16) for xs in pm]
    mrk = [block_diag([jnp.where(incl_c, x[c:, c:], 0.0) for x in xs]).astype(BF16) for xs in pm]
    vh = [jnp.where(in_head[hh], v[p], 0.0).astype(BF16) for p, hh in heads]
    inv = [eye + x for x in lab]
    lp = lab
    q = 2
    while q < c:
        lp = [_dot(x, x) for x in lp]
        inv = [t + _dot(t, x) for t, x in zip(inv, lp)]
        q *= 2
    xv = [jnp.dot(lak[n], vh[n], preferred_element_type=F32) for n in range(len(heads))]
    tw = [_dot(inv[n], jnp.concatenate([jnp.where(in_head[hh], a_abs[p], 0.0), xv[n]], axis=1))
          for n, (p, hh) in enumerate(heads)]
    y2h = [jnp.dot(mrk[n], vh[n], preferred_element_type=F32) for n in range(len(heads))]
    w1 = [tw[2 * p][:, :LANES] + tw[2 * p + 1][:, :LANES] for p in pairs]
    w2 = [tw[2 * p][:, LANES:] + tw[2 * p + 1][:, LANES:] for p in pairs]
    w12 = [jnp.concatenate([w1[p], w2[p]], axis=1).astype(BF16) for p in pairs]
    lhs_xy = [jnp.concatenate([jnp.concatenate([w1[p][j * c:(j + 1) * c], r_abs[p][j * c:(j + 1) * c]], axis=0)
                               for j in range(n_tok // c)], axis=0).astype(BF16) for p in pairs]
    vb = [x.astype(BF16) for x in v]
    zeros_c = jnp.zeros((c, LANES), BF16)
    u_rows = [[] for _ in pairs]
    y0_rows = [[] for _ in pairs]
    for j in range(n_tok // c):
        rows = slice(j * c, (j + 1) * c)
        s0 = [s_ref[p] for p in pairs]
        s0b = [x.astype(BF16) for x in s0]
        xy = [_dot_nt(lhs_xy[p][2 * j * c:2 * (j + 1) * c], s0b[p]) for p in pairs]
        uvt = [_dot_nt(jnp.concatenate([s0b[p], eye_s], axis=1),
                       jnp.concatenate([w12[p][rows], jnp.concatenate([zeros_c, vb[p][rows]], axis=1)], axis=0))
               for p in pairs]
        upd = [jnp.dot(uvt[p].astype(BF16), bk_end[p][j], preferred_element_type=F32) for p in pairs]
        for p in pairs:
            u_rows[p].append(xy[p][:c] + w2[p][rows])
            y0_rows[p].append(xy[p][c:])
            s_ref[p] = s0[p] * jnp.exp(cum_end[p][j * c:j * c + 1, :]) + jnp.where(same_head, upd[p], 0.0)
    ub = [jnp.concatenate(u_rows[p], axis=0).astype(BF16) for p in pairs]
    y1 = [jnp.dot(mrb[n], ub[p], preferred_element_type=F32) for n, (p, hh) in enumerate(heads)]
    half_sum = lambda x: jnp.where(head0, jnp.sum(jnp.where(head0, x, 0.0), axis=-1, keepdims=True),
                                   jnp.sum(jnp.where(head0, 0.0, x), axis=-1, keepdims=True))
    for p in pairs:
        sl = slice(p * LANES, (p + 1) * LANES)
        y = (jnp.concatenate(y0_rows[p], axis=0) + y2h[2 * p] + y2h[2 * p + 1]
             + jnp.where(head0, y1[2 * p], y1[2 * p + 1]))
        d = y - half_sum(y) * (1.0 / RW_HEAD)
        var = half_sum(d * d) * (1.0 / RW_HEAD)
        yn = d * lax.rsqrt(var + RW_GN_EPS) * lng_ref[:, sl] + lnb_ref[:, sl]
        y_ref[:, sl] = ((yn + bonus_ref[:, sl].astype(F32)) * g_ref[:, sl].astype(F32)).astype(y_ref.dtype)


def rw_scan(r, lw, k, v, a, b, bonus, g, ln_g, ln_b, *, batch, seq):
    t = r.shape[0]
    ns = seq // SCAN_STEP
    width = SCAN_PAIRS * LANES
    spec = pl.BlockSpec((SCAN_STEP, width), lambda bi, p, i: (bi * ns + i, p))
    vec = pl.BlockSpec((1, width), lambda bi, p, i: (0, p))
    return pl.pallas_call(
        _rw_scan_kernel,
        grid=(batch, RW_WIDTH // width, ns),
        in_specs=[spec] * 8 + [vec] * 2,
        out_specs=spec,
        out_shape=jax.ShapeDtypeStruct((t, RW_WIDTH), BF16),
        scratch_shapes=[pltpu.VMEM((SCAN_PAIRS, LANES, LANES), F32)],
        compiler_params=_params("parallel", "parallel", "arbitrary"),
        name="rw_scan",
    )(r, lw, k, v, a, b, bonus, g, ln_g, ln_b)


def _merge_kernel(h_ref, ya_ref, yb_ref, yc_ref, ga_ref, gb_ref, gc_ref, wa_ref, wb_ref, wc_ref, wo_ref, o_ref):
    gate = lambda g_ref: jax.nn.sigmoid(g_ref[...].astype(F32))
    merged = (gate(ga_ref) * jnp.dot(ya_ref[...], wa_ref[...], preferred_element_type=F32)
              + gate(gb_ref) * jnp.dot(yb_ref[...], wb_ref[...], preferred_element_type=F32)
              + gate(gc_ref) * jnp.dot(yc_ref[...], wc_ref[...], preferred_element_type=F32))
    o_ref[...] = h_ref[...] + jnp.dot(merged.astype(BF16), wo_ref[...], preferred_element_type=F32)


def merge(h, ya, yb, yc, proj, wa, wb, wc, wo, *, tm=MERGE_ROWS):
    t, d = h.shape
    row = lambda i: (i, 0)
    fix = lambda i: (0, 0)
    gate_specs = [pl.BlockSpec((tm, d), functools.partial(lambda i, n: (i, PROJ_GATE // d + n), n=n))
                  for n in range(3)]
    return pl.pallas_call(
        _merge_kernel,
        grid=(t // tm,),
        in_specs=[pl.BlockSpec((tm, d), row),
                  pl.BlockSpec((tm, ya.shape[1]), row),
                  pl.BlockSpec((tm, yb.shape[1]), row),
                  pl.BlockSpec((tm, yc.shape[1]), row),
                  *gate_specs,
                  pl.BlockSpec(wa.shape, fix), pl.BlockSpec(wb.shape, fix),
                  pl.BlockSpec(wc.shape, fix), pl.BlockSpec(wo.shape, fix)],
        out_specs=pl.BlockSpec((tm, d), row),
        out_shape=jax.ShapeDtypeStruct((t, d), F32),
        compiler_params=_params("parallel"),
        name="merge",
    )(h, ya, yb, yc, proj, proj, proj, wa, wb, wc, wo)


def _pad_cols(w, n):
    return jnp.pad(w, ((0, 0), (0, n - w.shape[1])))


def _layer_params(i, p):
    wuq = p["mla_w_uq"][i].reshape(MLA_Q_RANK, MLA_HEADS, MLA_QK)
    wuq = jnp.pad(wuq, ((0, 0), (0, 0), (0, LANES - MLA_QK)))
    wuq = jnp.concatenate([wuq.reshape(MLA_Q_RANK, -1), _rope_partner(wuq).reshape(MLA_Q_RANK, -1)], axis=1)
    qn = _pad_cols(p["mla_q_norm"][i].reshape(1, -1), LANES)
    kn = _pad_cols(p["mla_k_norm"][i].reshape(1, -1), LANES)
    gains = jnp.concatenate([qn, _rope_partner(qn), kn, _rope_partner(kn), jnp.zeros((4, LANES), F32)], axis=0)
    wukv = p["mla_w_ukv"][i].reshape(MLA_KV_RANK, MLA_HEADS, MLA_NOPE + MLA_V)
    wuk = jnp.pad(wukv[:, :, :MLA_NOPE], ((0, 0), (0, 0), (0, LANES - MLA_NOPE)))
    wuk = wuk.reshape(MLA_KV_RANK, MLA_HEADS * LANES)
    wuv = wukv[:, :, MLA_NOPE:].reshape(MLA_KV_RANK, MLA_HEADS * MLA_V)
    head_of = jnp.arange(RW_WIDTH) // RW_HEAD
    lp = {
        "w_proj": proj_weight(p["w_in"], i),
        "qa": p["mla_q_a_norm"][i].reshape(1, -1),
        "kva": p["mla_kv_a_norm"][i].reshape(1, -1),
        "wuq": wuq.astype(BF16), "wuk": wuk.astype(BF16), "wuvt": wuv.T.astype(BF16),
        "mla_gains": gains,
        "sg_vn": p["sg_v_norm"][i].reshape(1, -1),
        "sg_w": p["sg_w_s"][i].transpose(1, 0, 2).reshape(SG_CHUNK, SG_GROUPS * SG_CHUNK),
        "sg_b": jnp.repeat(p["sg_b_s"][i].T, SG_WIDTH // SG_GROUPS, axis=1),
        "rw_mu": p["rw_mu"][i].reshape(1, -1),
        "rw_w0": p["rw_w0"][i].reshape(1, -1),
        "rw_w2": jnp.pad(p["rw_w2"][i], ((0, RW_A_RANK), (0, 0))),
        "rw_a0": p["rw_a0"][i].reshape(1, -1),
        "rw_a2": jnp.pad(p["rw_a2"][i], ((RW_DECAY_RANK, 0), (0, 0))),
        "rw_g2": p["rw_g2"][i].astype(BF16),
        "rw_k_k": p["rw_k_k"][i].reshape(1, -1),
        "rw_k_a": p["rw_k_a"][i].reshape(1, -1),
        "rw_r_k": p["rw_r_k"][i].reshape(1, -1),
        "rw_ln_g": p["rw_ln_g"][i].reshape(1, -1),
        "rw_ln_b": p["rw_ln_b"][i].reshape(1, -1),
        "rw_bd": (head_of[:, None] == head_of[None, :]).astype(BF16),
        "w_out_mla": p["w_out_mla"][i].astype(BF16),
        "w_out_sg": p["w_out_sg"][i].astype(BF16),
        "w_out_rw": p["w_out_rw"][i].astype(BF16),
        "w_o": p["w_o"][i].astype(BF16),
    }
    if i > 0:
        lp["rw_v0"] = p["rw_v0"][i - 1].reshape(1, -1)
        lp["rw_v1"] = _pad_cols(p["rw_v1"][i - 1], LANES)
        lp["rw_v2"] = jnp.pad(p["rw_v2"][i - 1], ((0, LANES - p["rw_v2"].shape[1]), (0, 0)))
    return lp


def _rope_tables(positions):
    half = MLA_ROPE // 2
    per_row = LANES // half
    inv_freq = ROPE_BASE ** (-jnp.arange(0, MLA_ROPE, 2, dtype=F32) / MLA_ROPE)
    pos = jnp.repeat(positions.astype(F32).reshape(-1, per_row), half, axis=1)
    ang = pos * jnp.tile(inv_freq, per_row)
    return jnp.cos(ang).reshape(-1, half), jnp.sin(ang).reshape(-1, half)


def kernel(x, positions, ffn1_norm, ffn1_w_gate, ffn1_w_up, ffn1_w_down, mix_norm, w_in, mla_q_a_norm, mla_w_uq, mla_kv_a_norm, mla_w_ukv, mla_q_norm, mla_k_norm, sg_v_norm, sg_w_s, sg_b_s, rw_mu, rw_w0, rw_w2, rw_a0, rw_a2, rw_g2, rw_k_k, rw_k_a, rw_r_k, rw_ln_g, rw_ln_b, rw_v0, rw_v1, rw_v2, w_out_mla, w_out_sg, w_out_rw, w_o, ffn2_norm, ffn2_w_gate, ffn2_w_up, ffn2_w_down):
    params = dict(ffn1_norm=ffn1_norm, ffn1_w_gate=ffn1_w_gate, ffn1_w_up=ffn1_w_up, ffn1_w_down=ffn1_w_down,
                  mix_norm=mix_norm, w_in=w_in, mla_q_a_norm=mla_q_a_norm, mla_w_uq=mla_w_uq,
                  mla_kv_a_norm=mla_kv_a_norm, mla_w_ukv=mla_w_ukv, mla_q_norm=mla_q_norm, mla_k_norm=mla_k_norm,
                  sg_v_norm=sg_v_norm, sg_w_s=sg_w_s, sg_b_s=sg_b_s, rw_mu=rw_mu, rw_w0=rw_w0, rw_w2=rw_w2,
                  rw_a0=rw_a0, rw_a2=rw_a2, rw_g2=rw_g2, rw_k_k=rw_k_k, rw_k_a=rw_k_a, rw_r_k=rw_r_k,
                  rw_ln_g=rw_ln_g, rw_ln_b=rw_ln_b, rw_v0=rw_v0, rw_v1=rw_v1, rw_v2=rw_v2,
                  w_out_mla=w_out_mla, w_out_sg=w_out_sg, w_out_rw=w_out_rw, w_o=w_o,
                  ffn2_norm=ffn2_norm, ffn2_w_gate=ffn2_w_gate, ffn2_w_up=ffn2_w_up, ffn2_w_down=ffn2_w_down)
    batch, seq, d = x.shape
    depth = w_in.shape[0]
    cos, sin = _rope_tables(positions)
    h = x.reshape(batch * seq, d)
    v_first = None
    for i in range(depth):
        lp = _layer_params(i, params)
        h = ffn_half_step(h, ffn1_norm, ffn1_w_gate, ffn1_w_up, ffn1_w_down, i)
        proj = rms_matmul(h, mix_norm[i], lp["w_proj"], tm=PROJ_ROWS, tn=PROJ_TILE, out_dtype=BF16)
        q, k, vt = mla_prep(proj, cos, sin, lp["qa"], lp["kva"], lp["wuq"], lp["wuk"], lp["wuvt"],
                           lp["mla_gains"])
        y_a = mla_attention(q, k, vt, batch=batch, seq=seq)
        y_b = sgu(proj, lp["sg_vn"], lp["sg_w"], lp["sg_b"])
        r, lw, kr, vr, a, b, g, bonus = rw_prep(proj, lp, v_first, seq=seq)
        if v_first is None:
            v_first = vr
        y_c = rw_scan(r, lw, kr, vr, a, b, bonus, g, lp["rw_ln_g"], lp["rw_ln_b"], batch=batch, seq=seq)
        h = merge(h, y_a, y_b, y_c, proj, lp["w_out_mla"], lp["w_out_sg"], lp["w_out_rw"], lp["w_o"])
        h = ffn_half_step(h, ffn2_norm, ffn2_w_gate, ffn2_w_up, ffn2_w_down, i)
    return h.reshape(batch, seq, d)
```

```python
import functools

import jax
import jax.numpy as jnp
from jax import lax
from jax.experimental import pallas as pl
from jax.experimental.pallas import tpu as pltpu

F32 = jnp.float32
BF16 = jnp.bfloat16

LANES = 128
EPS = 1e-6
RW_GN_EPS = 64e-5
ROPE_BASE = 10000.0

MLA_HEADS = 8
MLA_NOPE = 64
MLA_ROPE = 32
MLA_QK = MLA_NOPE + MLA_ROPE
MLA_V = 64
MLA_Q_RANK = 256
MLA_KV_RANK = 128
SG_WIDTH = 512
SG_GROUPS = 8
SG_CHUNK = 128
RW_HEADS = 8
RW_HEAD = 64
RW_WIDTH = RW_HEADS * RW_HEAD
RW_DECAY_RANK = 64
RW_A_RANK = 64
RW_GATE_RANK = 128
RW_COLS = 3 * RW_WIDTH + RW_DECAY_RANK + RW_A_RANK + RW_GATE_RANK

V7X_VMEM_BYTES = 64 * 1024 * 1024
VMEM_LIMIT = V7X_VMEM_BYTES * 7 // 8
BF16_SUBLANES = 16

FFN_ROWS = 1024
FFN_COLS = 256
PROJ_ROWS = 1024
PROJ_TILE = 3328
MLA_PREP_ROWS = 1024
SGU_ROWS = 1024
RW_PREP_ROWS = 512
MERGE_ROWS = 1024
SCAN_CHUNK = 128
SCAN_STEP = 256
SCAN_PAIRS = 4
ATTN_BLOCK = 256
ATTN_HEADS = 8
PROJ_GATE = 2048
PROJ_MLA = PROJ_GATE + 3 * 1024
PROJ_SG = PROJ_MLA + 512
PROJ_COLS = PROJ_SG + 2 * SG_WIDTH
PREV_ROWS = BF16_SUBLANES
SUM_ROWS = BF16_SUBLANES
LOG2E = 1.4426950408889634
DECAY_SCALE = 0.6065306597126334


def _params(*sem):
    return pltpu.CompilerParams(dimension_semantics=sem, vmem_limit_bytes=VMEM_LIMIT)


def _dot(a, b):
    return jnp.dot(a.astype(BF16), b.astype(BF16), preferred_element_type=F32)


def _dot_x3(a, b):
    ah = a.astype(BF16)
    al = (a - ah.astype(F32)).astype(BF16)
    bh = b.astype(BF16)
    bl = (b - bh.astype(F32)).astype(BF16)
    f = lambda x, y: jnp.dot(x, y, preferred_element_type=F32)
    return f(ah, bh) + f(al, bh) + f(ah, bl)


def _dot_nt(a, b):
    return lax.dot_general(a, b, (((1,), (1,)), ((), ())), preferred_element_type=F32)


def _rms(x, g, n=None):
    n = x.shape[-1] if n is None else n
    ms = jnp.sum(x * x, axis=-1, keepdims=True) * (1.0 / n)
    return x * lax.rsqrt(ms + EPS) * g


def _segsum(x, bd):
    hi = x.astype(BF16)
    lo = (x - hi.astype(F32)).astype(BF16)
    return (jnp.dot(hi, bd, preferred_element_type=F32) + jnp.dot(lo, bd, preferred_element_type=F32))


def _proj_weight_kernel(wt_ref, o_ref):
    x = wt_ref[...]
    zeros = lambda n: jnp.zeros((n, x.shape[1]), x.dtype)
    o_kv = MLA_Q_RANK + MLA_KV_RANK
    o_sg = o_kv + MLA_ROPE
    o_rw = o_sg + 2 * SG_WIDTH
    o_gate = o_rw + RW_COLS
    rows = jnp.concatenate(
        [x[o_rw:o_gate], zeros(PROJ_GATE - RW_COLS), x[o_gate:],
         x[:o_kv], zeros(MLA_NOPE), x[o_kv:o_sg], zeros(LANES - MLA_QK),
         x[o_sg:o_rw]], axis=0)
    o_ref[...] = rows.T.astype(o_ref.dtype)


def proj_weight(w_in, layer, *, tr=LANES):
    wt = jnp.swapaxes(w_in, 1, 2)
    _, n, d = wt.shape
    return pl.pallas_call(
        _proj_weight_kernel,
        grid=(d // tr,),
        in_specs=[pl.BlockSpec((None, n, tr), lambda i: (layer, 0, i))],
        out_specs=pl.BlockSpec((tr, PROJ_COLS), lambda i: (i, 0)),
        out_shape=jax.ShapeDtypeStruct((d, PROJ_COLS), BF16),
        compiler_params=_params("parallel"),
        name="proj_weight",
    )(wt)


def _rms_matmul_kernel(h_ref, g_ref, w_ref, o_ref, z_ref):
    @pl.when(pl.program_id(1) == 0)
    def _():
        z_ref[...] = _rms(h_ref[...], g_ref[...]).astype(BF16)

    o_ref[...] = jnp.dot(z_ref[...], w_ref[...], preferred_element_type=F32).astype(o_ref.dtype)


def rms_matmul(h, g, w, *, tm, tn, out_dtype=F32):
    t, d = h.shape
    n = w.shape[1]
    return pl.pallas_call(
        _rms_matmul_kernel,
        grid=(t // tm, n // tn),
        in_specs=[pl.BlockSpec((tm, d), lambda i, j: (i, 0)),
                  pl.BlockSpec((1, d), lambda i, j: (0, 0)),
                  pl.BlockSpec((d, tn), lambda i, j: (0, j))],
        out_specs=pl.BlockSpec((tm, tn), lambda i, j: (i, j)),
        out_shape=jax.ShapeDtypeStruct((t, n), out_dtype),
        scratch_shapes=[pltpu.VMEM((tm, d), BF16)],
        compiler_params=_params("parallel", "arbitrary"),
        name="rms_matmul",
    )(h, g.reshape(1, d), w)


def _ffn_cast_kernel(wg_ref, wu_ref, wd_ref, og_ref, ou_ref, od_ref):
    og_ref[...] = wg_ref[...].astype(BF16)
    ou_ref[...] = wu_ref[...].astype(BF16)
    od_ref[...] = wd_ref[...].astype(BF16)


def ffn_weights(wg, wu, wd, layer, *, parts=2):
    _, d, f = wg.shape
    up = pl.BlockSpec((None, d // parts, f), lambda i: (layer, i, 0))
    down = pl.BlockSpec((None, f // parts, d), lambda i: (layer, i, 0))
    return pl.pallas_call(
        _ffn_cast_kernel,
        grid=(parts,),
        in_specs=[up, up, down],
        out_specs=[pl.BlockSpec((d // parts, f), lambda i: (i, 0)), pl.BlockSpec((d // parts, f), lambda i: (i, 0)),
                   pl.BlockSpec((f // parts, d), lambda i: (i, 0))],
        out_shape=[jax.ShapeDtypeStruct((d, f), BF16), jax.ShapeDtypeStruct((d, f), BF16),
                   jax.ShapeDtypeStruct((f, d), BF16)],
        compiler_params=_params("parallel"),
        name="ffn_weights",
    )(wg, wu, wd)


def _ffn_kernel(h_ref, g_ref, wg_ref, wu_ref, wd_ref, o_ref):
    z = _rms(h_ref[...], g_ref[...]).astype(BF16)
    acc = None
    for j in range(wg_ref.shape[1] // FFN_COLS):
        cols = slice(j * FFN_COLS, (j + 1) * FFN_COLS)
        a = jnp.dot(z, wg_ref[:, cols], preferred_element_type=F32)
        b = jnp.dot(z, wu_ref[:, cols], preferred_element_type=F32)
        t = (a * jax.nn.sigmoid(a) * b).astype(BF16)
        part = jnp.dot(t, wd_ref[cols, :], preferred_element_type=F32)
        acc = part if acc is None else acc + part
    o_ref[...] = h_ref[...] + 0.5 * acc


def ffn_half_step(h, g, wg, wu, wd, layer, *, tm=FFN_ROWS):
    t, d = h.shape
    f = wg.shape[1]
    fix = lambda i: (0, 0)
    return pl.pallas_call(
        _ffn_kernel,
        grid=(t // tm,),
        in_specs=[pl.BlockSpec((tm, d), lambda i: (i, 0)),
                  pl.BlockSpec((None, 1, d), lambda i: (layer, 0, 0)),
                  pl.BlockSpec((d, f), fix, pipeline_mode=pl.Buffered(1)),
                  pl.BlockSpec((d, f), fix, pipeline_mode=pl.Buffered(1)),
                  pl.BlockSpec((f, d), fix, pipeline_mode=pl.Buffered(1))],
        out_specs=pl.BlockSpec((tm, d), lambda i: (i, 0)),
        out_shape=jax.ShapeDtypeStruct((t, d), F32),
        compiler_params=_params("parallel"),
        name="ffn_half_step",
    )(h, g.reshape(g.shape[0], 1, d), wg, wu, wd)


def _rope_partner(x):
    half = MLA_ROPE // 2
    zeros = jnp.zeros_like(x[..., :MLA_NOPE])
    return jnp.concatenate([zeros, x[..., MLA_NOPE + half:MLA_QK], x[..., MLA_NOPE:MLA_NOPE + half],
                            jnp.zeros_like(x[..., MLA_QK:])], axis=-1)


def _mla_prep_kernel(c_ref, cos_ref, sin_ref, qa_ref, kva_ref, wuq_ref, wuk_ref, wuvt_ref, gains_ref,
                     q_ref, k_ref, vt_ref):
    c = c_ref[...].astype(F32)
    hw = MLA_HEADS * LANES
    zq = _rms(c[:, :MLA_Q_RANK], qa_ref[...]).astype(BF16)
    zkv = _rms(c[:, MLA_Q_RANK:MLA_Q_RANK + MLA_KV_RANK], kva_ref[...]).astype(BF16)
    kpe = c[:, MLA_Q_RANK + MLA_KV_RANK:]
    q = jnp.dot(zq, wuq_ref[...], preferred_element_type=F32)
    kn = jnp.dot(zkv, wuk_ref[...], preferred_element_type=F32)
    for j in range(vt_ref.shape[0]):
        rows = slice(j * ATTN_BLOCK, (j + 1) * ATTN_BLOCK)
        vt_ref[j] = _dot_nt(wuvt_ref[...], zkv[rows]).astype(BF16)
    cos, sin = cos_ref[...], sin_ref[...]
    ones, zeros = jnp.ones_like(c[:, :MLA_NOPE]), jnp.zeros_like(c[:, :MLA_NOPE])
    cos_full = jnp.concatenate([ones, cos, cos, ones[:, :LANES - MLA_QK]], axis=1)
    sin_signed = jnp.concatenate([zeros, -sin, sin, zeros[:, :LANES - MLA_QK]], axis=1)
    q_cos = gains_ref[0:1, :] * cos_full
    q_sin = gains_ref[1:2, :] * sin_signed
    k_cos = gains_ref[2:3, :] * cos_full
    lane = lax.broadcasted_iota(jnp.int32, kpe.shape, 1)
    kpe_partner = jnp.where(lane < MLA_NOPE + MLA_ROPE // 2,
                            pltpu.roll(kpe, LANES - MLA_ROPE // 2, axis=1),
                            pltpu.roll(kpe, MLA_ROPE // 2, axis=1))
    k_rot = kpe_partner * (gains_ref[3:4, :] * sin_signed)
    scale = MLA_QK ** -0.5 * LOG2E
    inv_n = 1.0 / MLA_QK
    for h in range(MLA_HEADS):
        sl = slice(h * LANES, (h + 1) * LANES)
        x = q[:, sl]
        r = lax.rsqrt(jnp.sum(x * x, axis=-1, keepdims=True) * inv_n + EPS) * scale
        q_ref[:, sl] = ((x * q_cos + q[:, hw + h * LANES:hw + (h + 1) * LANES] * q_sin) * r).astype(BF16)
        y = kn[:, sl] + kpe
        r = lax.rsqrt(jnp.sum(y * y, axis=-1, keepdims=True) * inv_n + EPS)
        k_ref[:, sl] = ((y * k_cos + k_rot) * r).astype(BF16)


def mla_prep(c, cos, sin, qa, kva, wuq, wuk, wuvt, gains, *, tm=MLA_PREP_ROWS):
    t = c.shape[0]
    hw = MLA_HEADS * LANES
    vw = MLA_HEADS * MLA_V
    row = lambda i: (i, 0)
    fix = lambda i: (0, 0)
    return pl.pallas_call(
        _mla_prep_kernel,
        grid=(t // tm,),
        in_specs=[pl.BlockSpec((tm, PROJ_SG - PROJ_MLA), lambda i: (i, PROJ_MLA // (PROJ_SG - PROJ_MLA))),
                  pl.BlockSpec((tm, MLA_ROPE // 2), row),
                  pl.BlockSpec((tm, MLA_ROPE // 2), row),
                  pl.BlockSpec((1, MLA_Q_RANK), fix),
                  pl.BlockSpec((1, MLA_KV_RANK), fix),
                  pl.BlockSpec(wuq.shape, fix),
                  pl.BlockSpec(wuk.shape, fix),
                  pl.BlockSpec(wuvt.shape, fix),
                  pl.BlockSpec(gains.shape, fix)],
        out_specs=[pl.BlockSpec((tm, hw), row),
                   pl.BlockSpec((tm, hw), row),
                   pl.BlockSpec((tm // ATTN_BLOCK, vw, ATTN_BLOCK), lambda i: (i, 0, 0))],
        out_shape=[jax.ShapeDtypeStruct((t, hw), BF16),
                   jax.ShapeDtypeStruct((t, hw), BF16),
                   jax.ShapeDtypeStruct((t // ATTN_BLOCK, vw, ATTN_BLOCK), BF16)],
        compiler_params=_params("parallel"),
        name="mla_prep",
    )(c, cos, sin, qa, kva, wuq, wuk, wuvt, gains)


def _attn_kernel(q_ref, k_ref, vt_ref, o_ref, st_scr, p_scr, acc_scr, mx_scr):
    i = pl.program_id(2)
    blk = ATTN_BLOCK
    heads = range(q_ref.shape[1] // LANES)
    q = q_ref[...]
    krow = lax.broadcasted_iota(jnp.int32, (blk, blk), 0)
    qcol = lax.broadcasted_iota(jnp.int32, (blk, blk), 1)
    ones_rows = jnp.ones((SUM_ROWS, blk), BF16)

    def put_scores(j, slot):
        ks = k_ref[pl.ds(pl.multiple_of(j * blk, blk), blk), :]
        for hh in heads:
            st = _dot_nt(ks[:, hh * LANES:(hh + 1) * LANES], q[:, hh * LANES:(hh + 1) * LANES])
            st_scr[slot, hh] = st
            mx_scr[slot, hh] = jnp.max(st, axis=0, keepdims=True)

    def weighted_values(j, slot):
        vts = vt_ref[j]
        return [jnp.dot(jnp.concatenate([vts[hh * MLA_V:(hh + 1) * MLA_V], ones_rows], axis=0), p_scr[slot, hh],
                        preferred_element_type=F32) for hh in heads]

    def step(j, slot, alphas_prev, ms, masked, prefetch):
        if prefetch:
            put_scores(j + 1, 1 - slot)
        pv = weighted_values(jnp.maximum(j - 1, 0), 1 - slot)
        sts = [st_scr[slot, hh] for hh in heads]
        if masked:
            keep = krow + (j - i) * blk <= qcol
            sts = [jnp.where(keep, st, -jnp.inf) for st in sts]
            m_new = tuple(jnp.maximum(ms[hh], jnp.max(sts[hh], axis=0, keepdims=True)) for hh in heads)
        else:
            m_new = tuple(jnp.maximum(ms[hh], mx_scr[slot, hh]) for hh in heads)
        for hh in heads:
            p_scr[slot, hh] = jnp.exp2(sts[hh] - m_new[hh]).astype(BF16)
            acc_scr[hh] = alphas_prev[hh] * acc_scr[hh] + pv[hh]
        return tuple(jnp.exp2(ms[hh] - m_new[hh]) for hh in heads), m_new

    def pair(t, carry, masked):
        alphas, ms = step(2 * t, 0, *carry, masked, True)
        return step(2 * t + 1, 1, alphas, ms, masked, not masked)

    p_scr[1] = jnp.zeros_like(p_scr[1])
    acc_scr[...] = jnp.zeros_like(acc_scr)
    put_scores(0, 0)
    init = (tuple(jnp.ones((1, blk), F32) for _ in heads), tuple(jnp.full((1, blk), -jnp.inf, F32) for _ in heads))
    carry = lax.fori_loop(0, i // 2, lambda t, c: pair(t, c, False), init)
    alphas, _ = pair(i // 2, carry, True)
    pv = weighted_values(2 * (i // 2) + 1, 1)
    outs = []
    for hh in heads:
        acc = alphas[hh] * acc_scr[hh] + pv[hh]
        outs.append(acc[:MLA_V] / acc[MLA_V:MLA_V + 1])
    o_ref[...] = jnp.concatenate(outs, axis=0).T.astype(o_ref.dtype)


def mla_attention(q, k, vt, *, batch, seq):
    t = q.shape[0]
    nq = seq // ATTN_BLOCK
    return pl.pallas_call(
        _attn_kernel,
        grid=(batch, MLA_HEADS // ATTN_HEADS, nq),
        in_specs=[pl.BlockSpec((ATTN_BLOCK, ATTN_HEADS * LANES), lambda b, p, i: (b * nq + i, p)),
                  pl.BlockSpec((seq, ATTN_HEADS * LANES), lambda b, p, i: (b, p)),
                  pl.BlockSpec((nq, ATTN_HEADS * MLA_V, ATTN_BLOCK), lambda b, p, i: (b, p, 0))],
        out_specs=pl.BlockSpec((ATTN_BLOCK, ATTN_HEADS * MLA_V), lambda b, p, i: (b * nq + i, p)),
        out_shape=jax.ShapeDtypeStruct((t, MLA_HEADS * MLA_V), BF16),
        scratch_shapes=[pltpu.VMEM((2, ATTN_HEADS, ATTN_BLOCK, ATTN_BLOCK), F32),
                        pltpu.VMEM((2, ATTN_HEADS, ATTN_BLOCK, ATTN_BLOCK), BF16),
                        pltpu.VMEM((ATTN_HEADS, MLA_V + SUM_ROWS, ATTN_BLOCK), F32),
                        pltpu.VMEM((2, ATTN_HEADS, 1, ATTN_BLOCK), F32)],
        compiler_params=_params("parallel", "parallel", "arbitrary"),
        name="mla_attention",
    )(q, k, vt)


def _sgu_kernel(pu_ref, pv_ref, vn_ref, w_ref, b_ref, o_ref):
    u = jax.nn.gelu(pu_ref[...].astype(F32), approximate=True)
    v = _rms(jax.nn.gelu(pv_ref[...].astype(F32), approximate=True), vn_ref[...])
    wrow = lax.broadcasted_iota(jnp.int32, (SG_CHUNK, SG_GROUPS * SG_CHUNK), 0)
    wcol = lax.broadcasted_iota(jnp.int32, (SG_CHUNK, SG_GROUPS * SG_CHUNK), 1) % SG_CHUNK
    w = jnp.where(wcol <= wrow, w_ref[...], 0.0).astype(BF16)
    low = lax.broadcasted_iota(jnp.int32, (SG_CHUNK, LANES), 1) < SG_WIDTH // SG_GROUPS
    for c in range(u.shape[0] // SG_CHUNK):
        rows = slice(c * SG_CHUNK, (c + 1) * SG_CHUNK)
        for q in range(SG_WIDTH // LANES):
            cols = slice(q * LANES, (q + 1) * LANES)
            vq = v[rows, cols]
            halves = jnp.concatenate([jnp.where(low, vq, 0.0), jnp.where(low, 0.0, vq)], axis=0).astype(BF16)
            mixed = jnp.dot(w[:, 2 * q * SG_CHUNK:2 * (q + 1) * SG_CHUNK], halves,
                            preferred_element_type=F32) + b_ref[:, cols]
            o_ref[rows, cols] = (u[rows, cols] * mixed).astype(o_ref.dtype)


def sgu(proj, v_norm, w_cat, b_full, *, tm=SGU_ROWS):
    t = proj.shape[0]
    fix = lambda i: (0, 0)
    return pl.pallas_call(
        _sgu_kernel,
        grid=(t // tm,),
        in_specs=[pl.BlockSpec((tm, SG_WIDTH), lambda i: (i, PROJ_SG // SG_WIDTH)),
                  pl.BlockSpec((tm, SG_WIDTH), lambda i: (i, PROJ_SG // SG_WIDTH + 1)),
                  pl.BlockSpec((1, SG_WIDTH), fix),
                  pl.BlockSpec(w_cat.shape, fix),
                  pl.BlockSpec(b_full.shape, fix)],
        out_specs=pl.BlockSpec((tm, SG_WIDTH), lambda i: (i, 0)),
        out_shape=jax.ShapeDtypeStruct((t, SG_WIDTH), BF16),
        compiler_params=_params("parallel"),
        name="sgu",
    )(proj, proj, v_norm, w_cat, b_full)


def _rw_prep_kernel(*refs, tiles_per_seq, has_vres):
    if has_vres:
        (p_ref, prev_ref, mu_ref, w0_ref, w2_ref, a0_ref, a2_ref, g2_ref, kk_ref, ka_ref, rk_ref, bd_ref,
         vf_ref, v0_ref, v1_ref, v2_ref,
         r_o, lw_o, k_o, v_o, a_o, b_o, g_o, bonus_o) = refs
    else:
        (p_ref, prev_ref, mu_ref, w0_ref, w2_ref, a0_ref, a2_ref, g2_ref, kk_ref, ka_ref, rk_ref, bd_ref,
         r_o, lw_o, k_o, v_o, a_o, b_o, g_o, bonus_o) = refs
    cur = p_ref[...].astype(F32)
    tm = cur.shape[0]
    first = (pl.program_id(0) % tiles_per_seq) == 0
    last_prev = jnp.where(first, 0.0, prev_ref[PREV_ROWS - 1:PREV_ROWS, :].astype(F32))
    rolled = pltpu.roll(cur, 1, axis=0)
    top = lax.broadcasted_iota(jnp.int32, (8, cur.shape[1]), 0) == 0
    prev = jnp.concatenate([jnp.where(top, last_prev, rolled[:8]), rolled[8:]], axis=0)
    x = cur + (prev - cur) * mu_ref[...]
    w3 = 3 * RW_WIDTH
    r = x[:, :RW_WIDTH]
    k = x[:, RW_WIDTH:2 * RW_WIDTH]
    v = x[:, 2 * RW_WIDTH:w3]
    xwa = x[:, w3:w3 + LANES]
    xg = x[:, w3 + LANES:]
    u = w0_ref[...] + _dot_x3(jnp.tanh(xwa), w2_ref[...])
    lw_o[...] = -DECAY_SCALE * jax.nn.sigmoid(u)
    a = jax.nn.sigmoid(a0_ref[...] + _dot_x3(xwa, a2_ref[...]))
    g_o[...] = _dot(jax.nn.sigmoid(xg), g2_ref[...]).astype(g_o.dtype)
    if has_vres:
        mix = jax.nn.sigmoid(v0_ref[...] + _dot(_dot(v, v1_ref[...]), v2_ref[...]))
        v = v + (vf_ref[...].astype(F32) - v) * mix
    bd = bd_ref[...]
    kk = k * kk_ref[...]
    kk = kk * lax.rsqrt(jnp.maximum(_segsum(kk * kk, bd), 1e-24))
    k = k * (1.0 + (a - 1.0) * ka_ref[...])
    r_o[...] = r.astype(r_o.dtype)
    k_o[...] = k.astype(k_o.dtype)
    v_o[...] = v.astype(v_o.dtype)
    a_o[...] = (-kk).astype(a_o.dtype)
    b_o[...] = (kk * a).astype(b_o.dtype)
    bonus_o[...] = (_segsum(r * k * rk_ref[...], bd) * v).astype(bonus_o.dtype)


def rw_prep(proj, lp, v_first, *, seq, tm=RW_PREP_ROWS):
    t = proj.shape[0]
    has_vres = v_first is not None
    row = lambda i: (i, 0)
    fix = lambda i: (0, 0)
    vec = pl.BlockSpec((1, RW_WIDTH), fix)
    in_specs = [pl.BlockSpec((tm, RW_COLS), row),
                pl.BlockSpec((PREV_ROWS, RW_COLS), lambda i: (jnp.maximum(i * (tm // PREV_ROWS) - 1, 0), 0)),
                pl.BlockSpec((1, RW_COLS), fix),
                vec, pl.BlockSpec((LANES, RW_WIDTH), fix),
                vec, pl.BlockSpec((LANES, RW_WIDTH), fix),
                pl.BlockSpec((RW_GATE_RANK, RW_WIDTH), fix),
                vec, vec, vec, pl.BlockSpec((RW_WIDTH, RW_WIDTH), fix)]
    args = [proj, proj, lp["rw_mu"], lp["rw_w0"], lp["rw_w2"], lp["rw_a0"], lp["rw_a2"], lp["rw_g2"],
            lp["rw_k_k"], lp["rw_k_a"], lp["rw_r_k"], lp["rw_bd"]]
    if has_vres:
        in_specs += [pl.BlockSpec((tm, RW_WIDTH), row), vec,
                     pl.BlockSpec(lp["rw_v1"].shape, fix), pl.BlockSpec(lp["rw_v2"].shape, fix)]
        args += [v_first, lp["rw_v0"], lp["rw_v1"], lp["rw_v2"]]
    dtypes = [BF16, F32, BF16, BF16, BF16, BF16, BF16, BF16]
    return pl.pallas_call(
        functools.partial(_rw_prep_kernel, tiles_per_seq=seq // tm, has_vres=has_vres),
        grid=(t // tm,),
        in_specs=in_specs,
        out_specs=[pl.BlockSpec((tm, RW_WIDTH), row)] * 8,
        out_shape=[jax.ShapeDtypeStruct((t, RW_WIDTH), dt) for dt in dtypes],
        compiler_params=_params("parallel"),
        name="rw_prep",
    )(*args)


def _cumsum_rows(tri, x):
    hi = x.astype(BF16)
    lo = (x - hi.astype(F32)).astype(BF16)
    return jnp.dot(tri, hi, preferred_element_type=F32) + jnp.dot(tri, lo, preferred_element_type=F32)


def _rows_bcast(x, c, which):
    n = x.shape[0] // c
    return jnp.concatenate([jnp.broadcast_to(x[j * c + which:j * c + which + 1], (c, x.shape[1]))
                            for j in range(n)], axis=0)


def _rw_scan_kernel(r_ref, lw_ref, k_ref, v_ref, a_ref, b_ref, bonus_ref, g_ref, lng_ref, lnb_ref, y_ref, s_ref):
    c = SCAN_CHUNK
    n_tok = r_ref.shape[0]
    pairs = range(r_ref.shape[1] // LANES)
    heads = [(p, hh) for p in pairs for hh in range(2)]
    nch = n_tok // c

    @pl.when(pl.program_id(2) == 0)
    def _():
        s_ref[...] = jnp.zeros_like(s_ref)

    ri = lax.broadcasted_iota(jnp.int32, (n_tok, n_tok), 0)
    ci = lax.broadcasted_iota(jnp.int32, (n_tok, n_tok), 1)
    same = (ri // c) == (ci // c)
    incl = jnp.logical_and(same, ci <= ri)
    strict = jnp.logical_and(same, ci < ri)
    eye = jnp.where(ri == ci, 1.0, 0.0)
    tri = jnp.where(incl, 1.0, 0.0).astype(BF16)
    head0 = lax.broadcasted_iota(jnp.int32, (n_tok, LANES), 1) < RW_HEAD
    in_head = (head0, jnp.logical_not(head0))
    sr = lax.broadcasted_iota(jnp.int32, (LANES, LANES), 0)
    sc = lax.broadcasted_iota(jnp.int32, (LANES, LANES), 1)
    same_head = (sr // RW_HEAD) == (sc // RW_HEAD)
    eye_s = jnp.where(sr == sc, 1.0, 0.0).astype(BF16)

    cols = lambda ref, p: ref[:, p * LANES:(p + 1) * LANES]
    lw = [cols(lw_ref, p) for p in pairs]
    cum = [_cumsum_rows(tri, x) for x in lw]
    cum_mid = [_rows_bcast(x, c, c // 2 - 1) for x in cum]
    cum_end = [_rows_bcast(x, c, c - 1) for x in cum]
    r = [cols(r_ref, p).astype(F32) for p in pairs]
    a = [cols(a_ref, p).astype(F32) for p in pairs]
    k = [cols(k_ref, p).astype(F32) for p in pairs]
    b = [cols(b_ref, p).astype(F32) for p in pairs]
    v = [cols(v_ref, p).astype(F32) for p in pairs]
    r_abs = [r[p] * jnp.exp(cum[p]) for p in pairs]
    a_abs = [a[p] * jnp.exp(cum[p] - lw[p]) for p in pairs]
    g_end = [jnp.exp(cum_end[p] - cum[p]) for p in pairs]
    bk_end = [[jnp.concatenate([(b[p] * g_end[p])[j * c:(j + 1) * c], (k[p] * g_end[p])[j * c:(j + 1) * c]],
                               axis=0).astype(BF16) for j in range(n_tok // c)] for p in pairs]
    g_bwd = [jnp.exp(cum_mid[p] - cum[p]) for p in pairs]
    r_mid = [r[p] * jnp.exp(cum[p] - cum_mid[p]) for p in pairs]
    a_mid = [a[p] * jnp.exp(cum[p] - lw[p] - cum_mid[p]) for p in pairs]
    rhs = [jnp.concatenate([b[p] * g_bwd[p], k[p] * g_bwd[p]], axis=0).astype(BF16) for p in pairs]
    a_h = [jnp.where(in_head[hh], a_mid[p], 0.0).astype(BF16) for p, hh in heads]
    r_h = [jnp.where(in_head[hh], r_mid[p], 0.0).astype(BF16) for p, hh in heads]
    pm = [[_dot_nt(jnp.concatenate([a_h[n][j * c:(j + 1) * c], r_h[n][j * c:(j + 1) * c]], axis=0),
                   jnp.concatenate([rhs[p][j * c:(j + 1) * c], rhs[p][n_tok + j * c:n_tok + (j + 1) * c]], axis=0))
           for j in range(nch)] for n, (p, hh) in enumerate(heads)]
    strict_c = strict[:c, :c]
    incl_c = incl[:c, :c]
    zero_cc = jnp.zeros((c, c), F32)

    def block_diag(blocks):
        return jnp.concatenate([jnp.concatenate([blocks[j] if jj == j else zero_cc for jj in range(nch)], axis=1)
                                for j in range(nch)], axis=0)

    lab = [block_diag([jnp.where(strict_c, x[:c, :c], 0.0) for x in xs]) for xs in pm]
    lak = [block_diag([jnp.where(strict_c, x[:c, c:], 0.0) for x in xs]).astype(BF16) for xs in pm]
    mrb = [block_diag([jnp.where(incl_c, x[c:, :c], 0.0) for x in xs]).astype(BF16) for xs in pm]
    mrk = [block_diag([jnp.where(incl_c, x[c:, c:], 0.0) for x in xs]).astype(BF16) for xs in pm]
    vh = [jnp.where(in_head[hh], v[p], 0.0).astype(BF16) for p, hh in heads]
    inv = [eye + x for x in lab]
    lp = lab
    q = 2
    while q < c:
        lp = [_dot(x, x) for x in lp]
        inv = [t + _dot(t, x) for t, x in zip(inv, lp)]
        q *= 2
    xv = [jnp.dot(lak[n], vh[n], preferred_element_type=F32) for n in range(len(heads))]
    tw = [_dot(inv[n], jnp.concatenate([jnp.where(in_head[hh], a_abs[p], 0.0), xv[n]], axis=1))
          for n, (p, hh) in enumerate(heads)]
    y2h = [jnp.dot(mrk[n], vh[n], preferred_element_type=F32) for n in range(len(heads))]
    w1 = [tw[2 * p][:, :LANES] + tw[2 * p + 1][:, :LANES] for p in pairs]
    w2 = [tw[2 * p][:, LANES:] + tw[2 * p + 1][:, LANES:] for p in pairs]
    w12 = [jnp.concatenate([w1[p], w2[p]], axis=1).astype(BF16) for p in pairs]
    lhs_xy = [jnp.concatenate([jnp.concatenate([w1[p][j * c:(j + 1) * c], r_abs[p][j * c:(j + 1) * c]], axis=0)
                               for j in range(n_tok // c)], axis=0).astype(BF16) for p in pairs]
    vb = [x.astype(BF16) for x in v]
    zeros_c = jnp.zeros((c, LANES), BF16)
    u_rows = [[] for _ in pairs]
    y0_rows = [[] for _ in pairs]
    for j in range(n_tok // c):
        rows = slice(j * c, (j + 1) * c)
        s0 = [s_ref[p] for p in pairs]
        s0b = [x.astype(BF16) for x in s0]
        xy = [_dot_nt(lhs_xy[p][2 * j * c:2 * (j + 1) * c], s0b[p]) for p in pairs]
        uvt = [_dot_nt(jnp.concatenate([s0b[p], eye_s], axis=1),
                       jnp.concatenate([w12[p][rows], jnp.concatenate([zeros_c, vb[p][rows]], axis=1)], axis=0))
               for p in pairs]
        upd = [jnp.dot(uvt[p].astype(BF16), bk_end[p][j], preferred_element_type=F32) for p in pairs]
        for p in pairs:
            u_rows[p].append(xy[p][:c] + w2[p][rows])
            y0_rows[p].append(xy[p][c:])
            s_ref[p] = s0[p] * jnp.exp(cum_end[p][j * c:j * c + 1, :]) + jnp.where(same_head, upd[p], 0.0)
    ub = [jnp.concatenate(u_rows[p], axis=0).astype(BF16) for p in pairs]
    y1 = [jnp.dot(mrb[n], ub[p], preferred_element_type=F32) for n, (p, hh) in enumerate(heads)]
    half_sum = lambda x: jnp.where(head0, jnp.sum(jnp.where(head0, x, 0.0), axis=-1, keepdims=True),
                                   jnp.sum(jnp.where(head0, 0.0, x), axis=-1, keepdims=True))
    for p in pairs:
        sl = slice(p * LANES, (p + 1) * LANES)
        y = (jnp.concatenate(y0_rows[p], axis=0) + y2h[2 * p] + y2h[2 * p + 1]
             + jnp.where(head0, y1[2 * p], y1[2 * p + 1]))
        d = y - half_sum(y) * (1.0 / RW_HEAD)
        var = half_sum(d * d) * (1.0 / RW_HEAD)
        yn = d * lax.rsqrt(var + RW_GN_EPS) * lng_ref[:, sl] + lnb_ref[:, sl]
        y_ref[:, sl] = ((yn + bonus_ref[:, sl].astype(F32)) * g_ref[:, sl].astype(F32)).astype(y_ref.dtype)


def rw_scan(r, lw, k, v, a, b, bonus, g, ln_g, ln_b, *, batch, seq):
    t = r.shape[0]
    ns = seq // SCAN_STEP
    width = SCAN_PAIRS * LANES
    spec = pl.BlockSpec((SCAN_STEP, width), lambda bi, p, i: (bi * ns + i, p))
    vec = pl.BlockSpec((1, width), lambda bi, p, i: (0, p))
    return pl.pallas_call(
        _rw_scan_kernel,
        grid=(batch, RW_WIDTH // width, ns),
        in_specs=[spec] * 8 + [vec] * 2,
        out_specs=spec,
        out_shape=jax.ShapeDtypeStruct((t, RW_WIDTH), BF16),
        scratch_shapes=[pltpu.VMEM((SCAN_PAIRS, LANES, LANES), F32)],
        compiler_params=_params("parallel", "parallel", "arbitrary"),
        name="rw_scan",
    )(r, lw, k, v, a, b, bonus, g, ln_g, ln_b)


def _merge_kernel(h_ref, ya_ref, yb_ref, yc_ref, ga_ref, gb_ref, gc_ref, wa_ref, wb_ref, wc_ref, wo_ref, o_ref):
    gate = lambda g_ref: jax.nn.sigmoid(g_ref[...].astype(F32))
    merged = (gate(ga_ref) * jnp.dot(ya_ref[...], wa_ref[...], preferred_element_type=F32)
              + gate(gb_ref) * jnp.dot(yb_ref[...], wb_ref[...], preferred_element_type=F32)
              + gate(gc_ref) * jnp.dot(yc_ref[...], wc_ref[...], preferred_element_type=F32))
    o_ref[...] = h_ref[...] + jnp.dot(merged.astype(BF16), wo_ref[...], preferred_element_type=F32)


def merge(h, ya, yb, yc, proj, wa, wb, wc, wo, *, tm=MERGE_ROWS):
    t, d = h.shape
    row = lambda i: (i, 0)
    fix = lambda i: (0, 0)
    gate_specs = [pl.BlockSpec((tm, d), functools.partial(lambda i, n: (i, PROJ_GATE // d + n), n=n))
                  for n in range(3)]
    return pl.pallas_call(
        _merge_kernel,
        grid=(t // tm,),
        in_specs=[pl.BlockSpec((tm, d), row),
                  pl.BlockSpec((tm, ya.shape[1]), row),
                  pl.BlockSpec((tm, yb.shape[1]), row),
                  pl.BlockSpec((tm, yc.shape[1]), row),
                  *gate_specs,
                  pl.BlockSpec(wa.shape, fix), pl.BlockSpec(wb.shape, fix),
                  pl.BlockSpec(wc.shape, fix), pl.BlockSpec(wo.shape, fix)],
        out_specs=pl.BlockSpec((tm, d), row),
        out_shape=jax.ShapeDtypeStruct((t, d), F32),
        compiler_params=_params("parallel"),
        name="merge",
    )(h, ya, yb, yc, proj, proj, proj, wa, wb, wc, wo)


def _pad_cols(w, n):
    return jnp.pad(w, ((0, 0), (0, n - w.shape[1])))


def _layer_params(i, p):
    wuq = p["mla_w_uq"][i].reshape(MLA_Q_RANK, MLA_HEADS, MLA_QK)
    wuq = jnp.pad(wuq, ((0, 0), (0, 0), (0, LANES - MLA_QK)))
    wuq = jnp.concatenate([wuq.reshape(MLA_Q_RANK, -1), _rope_partner(wuq).reshape(MLA_Q_RANK, -1)], axis=1)
    qn = _pad_cols(p["mla_q_norm"][i].reshape(1, -1), LANES)
    kn = _pad_cols(p["mla_k_norm"][i].reshape(1, -1), LANES)
    gains = jnp.concatenate([qn, _rope_partner(qn), kn, _rope_partner(kn), jnp.zeros((4, LANES), F32)], axis=0)
    wukv = p["mla_w_ukv"][i].reshape(MLA_KV_RANK, MLA_HEADS, MLA_NOPE + MLA_V)
    wuk = jnp.pad(wukv[:, :, :MLA_NOPE], ((0, 0), (0, 0), (0, LANES - MLA_NOPE)))
    wuk = wuk.reshape(MLA_KV_RANK, MLA_HEADS * LANES)
    wuv = wukv[:, :, MLA_NOPE:].reshape(MLA_KV_RANK, MLA_HEADS * MLA_V)
    head_of = jnp.arange(RW_WIDTH) // RW_HEAD
    lp = {
        "w_proj": proj_weight(p["w_in"], i),
        "qa": p["mla_q_a_norm"][i].reshape(1, -1),
        "kva": p["mla_kv_a_norm"][i].reshape(1, -1),
        "wuq": wuq.astype(BF16), "wuk": wuk.astype(BF16), "wuvt": wuv.T.astype(BF16),
        "mla_gains": gains,
        "sg_vn": p["sg_v_norm"][i].reshape(1, -1),
        "sg_w": p["sg_w_s"][i].transpose(1, 0, 2).reshape(SG_CHUNK, SG_GROUPS * SG_CHUNK),
        "sg_b": jnp.repeat(p["sg_b_s"][i].T, SG_WIDTH // SG_GROUPS, axis=1),
        "rw_mu": p["rw_mu"][i].reshape(1, -1),
        "rw_w0": p["rw_w0"][i].reshape(1, -1),
        "rw_w2": jnp.pad(p["rw_w2"][i], ((0, RW_A_RANK), (0, 0))),
        "rw_a0": p["rw_a0"][i].reshape(1, -1),
        "rw_a2": jnp.pad(p["rw_a2"][i], ((RW_DECAY_RANK, 0), (0, 0))),
        "rw_g2": p["rw_g2"][i].astype(BF16),
        "rw_k_k": p["rw_k_k"][i].reshape(1, -1),
        "rw_k_a": p["rw_k_a"][i].reshape(1, -1),
        "rw_r_k": p["rw_r_k"][i].reshape(1, -1),
        "rw_ln_g": p["rw_ln_g"][i].reshape(1, -1),
        "rw_ln_b": p["rw_ln_b"][i].reshape(1, -1),
        "rw_bd": (head_of[:, None] == head_of[None, :]).astype(BF16),
        "w_out_mla": p["w_out_mla"][i].astype(BF16),
        "w_out_sg": p["w_out_sg"][i].astype(BF16),
        "w_out_rw": p["w_out_rw"][i].astype(BF16),
        "w_o": p["w_o"][i].astype(BF16),
    }
    if i > 0:
        lp["rw_v0"] = p["rw_v0"][i - 1].reshape(1, -1)
        lp["rw_v1"] = _pad_cols(p["rw_v1"][i - 1], LANES)
        lp["rw_v2"] = jnp.pad(p["rw_v2"][i - 1], ((0, LANES - p["rw_v2"].shape[1]), (0, 0)))
    return lp


def _rope_tables(positions):
    half = MLA_ROPE // 2
    per_row = LANES // half
    inv_freq = ROPE_BASE ** (-jnp.arange(0, MLA_ROPE, 2, dtype=F32) / MLA_ROPE)
    pos = jnp.repeat(positions.astype(F32).reshape(-1, per_row), half, axis=1)
    ang = pos * jnp.tile(inv_freq, per_row)
    return jnp.cos(ang).reshape(-1, half), jnp.sin(ang).reshape(-1, half)


def kernel(x, positions, ffn1_norm, ffn1_w_gate, ffn1_w_up, ffn1_w_down, mix_norm, w_in, mla_q_a_norm, mla_w_uq, mla_kv_a_norm, mla_w_ukv, mla_q_norm, mla_k_norm, sg_v_norm, sg_w_s, sg_b_s, rw_mu, rw_w0, rw_w2, rw_a0, rw_a2, rw_g2, rw_k_k, rw_k_a, rw_r_k, rw_ln_g, rw_ln_b, rw_v0, rw_v1, rw_v2, w_out_mla, w_out_sg, w_out_rw, w_o, ffn2_norm, ffn2_w_gate, ffn2_w_up, ffn2_w_down):
    params = dict(ffn1_norm=ffn1_norm, ffn1_w_gate=ffn1_w_gate, ffn1_w_up=ffn1_w_up, ffn1_w_down=ffn1_w_down,
                  mix_norm=mix_norm, w_in=w_in, mla_q_a_norm=mla_q_a_norm, mla_w_uq=mla_w_uq,
                  mla_kv_a_norm=mla_kv_a_norm, mla_w_ukv=mla_w_ukv, mla_q_norm=mla_q_norm, mla_k_norm=mla_k_norm,
                  sg_v_norm=sg_v_norm, sg_w_s=sg_w_s, sg_b_s=sg_b_s, rw_mu=rw_mu, rw_w0=rw_w0, rw_w2=rw_w2,
                  rw_a0=rw_a0, rw_a2=rw_a2, rw_g2=rw_g2, rw_k_k=rw_k_k, rw_k_a=rw_k_a, rw_r_k=rw_r_k,
                  rw_ln_g=rw_ln_g, rw_ln_b=rw_ln_b, rw_v0=rw_v0, rw_v1=rw_v1, rw_v2=rw_v2,
                  w_out_mla=w_out_mla, w_out_sg=w_out_sg, w_out_rw=w_out_rw, w_o=w_o,
                  ffn2_norm=ffn2_norm, ffn2_w_gate=ffn2_w_gate, ffn2_w_up=ffn2_w_up, ffn2_w_down=ffn2_w_down)
    batch, seq, d = x.shape
    depth = w_in.shape[0]
    cos, sin = _rope_tables(positions)
    h = x.reshape(batch * seq, d)
    v_first = None
    for i in range(depth):
        lp = _layer_params(i, params)
        h = ffn_half_step(h, ffn1_norm, *ffn_weights(ffn1_w_gate, ffn1_w_up, ffn1_w_down, i), i)
        proj = rms_matmul(h, mix_norm[i], lp["w_proj"], tm=PROJ_ROWS, tn=PROJ_TILE, out_dtype=BF16)
        q, k, vt = mla_prep(proj, cos, sin, lp["qa"], lp["kva"], lp["wuq"], lp["wuk"], lp["wuvt"],
                           lp["mla_gains"])
        y_a = mla_attention(q, k, vt, batch=batch, seq=seq)
        y_b = sgu(proj, lp["sg_vn"], lp["sg_w"], lp["sg_b"])
        r, lw, kr, vr, a, b, g, bonus = rw_prep(proj, lp, v_first, seq=seq)
        if v_first is None:
            v_first = vr
        y_c = rw_scan(r, lw, kr, vr, a, b, bonus, g, lp["rw_ln_g"], lp["rw_ln_b"], batch=batch, seq=seq)
        h = merge(h, y_a, y_b, y_c, proj, lp["w_out_mla"], lp["w_out_sg"], lp["w_out_rw"], lp["w_o"])
        h = ffn_half_step(h, ffn2_norm, *ffn_weights(ffn2_w_gate, ffn2_w_up, ffn2_w_down, i), i)
    return h.reshape(batch, seq, d)
```
